```python
import jax, jax.numpy as jnp
from jax import lax
import numpy as np

D_MODEL = 1024
BATCH = 4
SEQ = 8192
DEPTH = 1

D_MIX = D_MODEL
D_HGRN = D_MIX // 2
D_CONV = D_MIX - D_HGRN
HGRN_HEAD_DIM = 128
HGRN_HEADS = D_HGRN // HGRN_HEAD_DIM
CHUNK = 64
CONV_WIDTH = 31
CONV_PAD = CONV_WIDTH // 2
N_IN = 5 * D_HGRN + 2 * D_CONV
PEER_HEADS = 8
PEER_KEYS = 128
PEER_EXPERTS = PEER_KEYS * PEER_KEYS
PEER_TOPK = 16
PEER_DKEY = 256
PEER_HALF = PEER_DKEY // 2
PEER_BLOCK = 128
EPS = 1e-6

kernel_name = 'hgrn2_conformer_peer_hybrid_encoder'


def _rmsnorm(x, g):
    xf = x.astype(jnp.float32)
    y = xf * lax.rsqrt(jnp.mean(xf * xf, axis=-1, keepdims=True) + EPS)
    return (y * g.astype(jnp.float32)).astype(x.dtype)


def _modulate(h, shift, scale):
    return h * (1 + scale[:, None, :]) + shift[:, None, :]


def _chunk_recurrence(q, k, v, logf):
    nb, nh, L, dk = q.shape
    dv = v.shape[-1]
    nc = L // CHUNK
    q = q.reshape(nb, nh, nc, CHUNK, dk)
    k = k.reshape(nb, nh, nc, CHUNK, dk)
    logf = logf.reshape(nb, nh, nc, CHUNK, dk)
    v = v.reshape(nb, nh, nc, CHUNK, dv)
    b = jnp.cumsum(logf, axis=3)
    b_ref = b[:, :, :, CHUNK // 2 - 1:CHUNK // 2, :]
    b_last = b[:, :, :, -1:, :]
    qr = q * jnp.exp(b - b_ref)
    kr = k * jnp.exp(b_ref - b)
    scores = jnp.einsum('bhntd,bhnsd->bhnts', qr, kr)
    incl = jnp.tril(jnp.ones((CHUNK, CHUNK), dtype=bool))
    scores = jnp.where(incl, scores, 0.0)
    o_intra = jnp.einsum('bhnts,bhnsv->bhntv', scores, v)
    u = jnp.einsum('bhnsd,bhnsv->bhndv', k * jnp.exp(b_last - b), v)
    a = jnp.exp(b_last[:, :, :, 0, :])

    def step(S, inp):
        a_n, u_n = inp
        return a_n[..., None] * S + u_n, S

    S0 = jnp.zeros((nb, nh, dk, dv), jnp.float32)
    _, S_prev = lax.scan(step, S0, (jnp.moveaxis(a, 2, 0), jnp.moveaxis(u, 2, 0)))
    S_prev = jnp.moveaxis(S_prev, 0, 2)
    o_inter = jnp.einsum('bhntd,bhndv->bhntv', q * jnp.exp(b), S_prev)
    return (o_intra + o_inter).reshape(nb, nh, L, dv)


def _hgrn2(q_raw, ff_raw, fb_raw, i_raw, g_raw, lb_fwd, lb_bwd, norm_g):
    nb, L, _ = q_raw.shape
    f32 = jnp.float32

    def heads(t):
        return t.astype(f32).reshape(nb, L, HGRN_HEADS, HGRN_HEAD_DIM).transpose(0, 2, 1, 3)

    q = jax.nn.silu(heads(q_raw))
    v = heads(i_raw)

    def gates(f_raw, lb):
        z = heads(f_raw)
        lbh = lb.astype(f32).reshape(1, HGRN_HEADS, 1, HGRN_HEAD_DIM)
        logf = jnp.log(lbh + (1 - lbh) * jax.nn.sigmoid(z))
        k = (1 - lbh) * jax.nn.sigmoid(-z)
        return k, logf

    k_f, logf_f = gates(ff_raw, lb_fwd)
    k_b, logf_b = gates(fb_raw, lb_bwd)
    o_f = _chunk_recurrence(q, k_f, v, logf_f)
    flip = lambda t: jnp.flip(t, axis=2)
    o_b = flip(_chunk_recurrence(flip(q), flip(k_b), flip(v), flip(logf_b)))
    o = (o_f + o_b).transpose(0, 2, 1, 3)
    o = o * lax.rsqrt(jnp.mean(o * o, axis=-1, keepdims=True) + EPS)
    o = o * norm_g.astype(f32).reshape(HGRN_HEADS, HGRN_HEAD_DIM)
    o = o.reshape(nb, L, D_HGRN) * jax.nn.silu(g_raw.astype(f32))
    return o.astype(q_raw.dtype)


def _conv_module(a, gate, w, bias, ln_g, ln_b):
    h = a * jax.nn.sigmoid(gate)
    h = lax.conv_general_dilated(
        h, w[:, None, :].astype(h.dtype), window_strides=(1,),
        padding=[(CONV_PAD, CONV_PAD)], dimension_numbers=('NWC', 'WIO', 'NWC'),
        feature_group_count=D_CONV) + bias
    hf = h.astype(jnp.float32)
    mu = jnp.mean(hf, axis=-1, keepdims=True)
    var = jnp.mean(jnp.square(hf - mu), axis=-1, keepdims=True)
    hf = (hf - mu) * lax.rsqrt(var + EPS) * ln_g.astype(jnp.float32) + ln_b.astype(jnp.float32)
    return jax.nn.silu(hf).astype(a.dtype)


def _peer(h, wq, keys1, keys2, u_tab, v_tab):
    nb, L, D = h.shape
    q = (h @ wq).reshape(nb, L, PEER_HEADS, PEER_DKEY)
    q1, q2 = q[..., :PEER_HALF], q[..., PEER_HALF:]
    s1 = jnp.einsum('blhd,kd->blhk', q1, keys1).astype(jnp.float32)
    s2 = jnp.einsum('blhd,kd->blhk', q2, keys2).astype(jnp.float32)
    v1, i1 = lax.top_k(s1, PEER_TOPK)
    v2, i2 = lax.top_k(s2, PEER_TOPK)
    cand = (v1[..., :, None] + v2[..., None, :]).reshape(nb, L, PEER_HEADS, PEER_TOPK * PEER_TOPK)
    sc, ci = lax.top_k(cand, PEER_TOPK)
    e = (jnp.take_along_axis(i1, ci // PEER_TOPK, axis=-1) * PEER_KEYS
         + jnp.take_along_axis(i2, ci % PEER_TOPK, axis=-1))
    w = jax.nn.softmax(sc, axis=-1)
    T = nb * L
    nsel = PEER_HEADS * PEER_TOPK
    hb = h.reshape(T // PEER_BLOCK, PEER_BLOCK, D)
    eb = e.reshape(T // PEER_BLOCK, PEER_BLOCK, nsel)
    wb = w.reshape(T // PEER_BLOCK, PEER_BLOCK, nsel)

    def block(args):
        hx, ex, wx = args
        u_sel = u_tab[ex]
        act = jax.nn.gelu(jnp.einsum('td,tjd->tj', hx, u_sel).astype(jnp.float32), approximate=False)
        coef = (wx * act).astype(hx.dtype)
        return jnp.einsum('tj,tjd->td', coef, v_tab[ex])

    y = lax.map(block, (hb, eb, wb))
    return y.reshape(nb, L, D)


def setup_inputs(seed: int = 0) -> dict:
    key = jax.random.key(seed)
    ks = jax.random.split(key, 24)
    nrm = jax.random.normal
    f32 = jnp.float32
    Dm = D_MODEL
    return {
        'x': nrm(ks[0], (BATCH, SEQ, Dm), f32),
        'c': nrm(ks[1], (BATCH, Dm), f32),
        'ada_w': nrm(ks[2], (DEPTH, Dm, 6 * Dm), f32) * (0.5 * Dm ** -0.5),
        'ada_b': nrm(ks[3], (DEPTH, 6 * Dm), f32) * 0.02,
        'norm_mix_g': 1.0 + 0.05 * nrm(ks[4], (DEPTH, Dm), f32),
        'w_in': nrm(ks[5], (DEPTH, Dm, N_IN), f32) * Dm ** -0.5,
        'lb_gamma_fwd': 0.1 * nrm(ks[6], (DEPTH + 1, D_HGRN), f32),
        'lb_gamma_bwd': 0.1 * nrm(ks[7], (DEPTH + 1, D_HGRN), f32),
        'hgrn_norm_g': 1.0 + 0.05 * nrm(ks[8], (DEPTH, D_HGRN), f32),
        'conv_w': nrm(ks[9], (DEPTH, CONV_WIDTH, D_CONV), f32) * CONV_WIDTH ** -0.5,
        'conv_b': 0.02 * nrm(ks[10], (DEPTH, D_CONV), f32),
        'conv_ln_g': 1.0 + 0.05 * nrm(ks[11], (DEPTH, D_CONV), f32),
        'conv_ln_b': 0.02 * nrm(ks[12], (DEPTH, D_CONV), f32),
        'w_out': nrm(ks[13], (DEPTH, D_MIX, Dm), f32) * D_MIX ** -0.5,
        'norm_ffn_g': 1.0 + 0.05 * nrm(ks[14], (DEPTH, Dm), f32),
        'peer_wq': nrm(ks[15], (DEPTH, Dm, PEER_HEADS * PEER_DKEY), f32) * Dm ** -0.5,
        'peer_keys1': nrm(ks[16], (DEPTH, PEER_KEYS, PEER_HALF), f32) * PEER_HALF ** -0.5,
        'peer_keys2': nrm(ks[17], (DEPTH, PEER_KEYS, PEER_HALF), f32) * PEER_HALF ** -0.5,
        'peer_u': nrm(ks[18], (DEPTH, PEER_EXPERTS, Dm), f32) * Dm ** -0.5,
        'peer_v': nrm(ks[19], (DEPTH, PEER_EXPERTS, Dm), f32),
        'final_ada_w': nrm(ks[20], (Dm, 2 * Dm), f32) * (0.5 * Dm ** -0.5),
        'final_ada_b': 0.02 * nrm(ks[21], (2 * Dm,), f32),
        'final_norm_g': 1.0 + 0.05 * nrm(ks[22], (Dm,), f32),
    }


def reference(x, c, ada_w, ada_b, norm_mix_g, w_in, lb_gamma_fwd, lb_gamma_bwd, hgrn_norm_g,
              conv_w, conv_b, conv_ln_g, conv_ln_b, w_out, norm_ffn_g, peer_wq, peer_keys1,
              peer_keys2, peer_u, peer_v, final_ada_w, final_ada_b, final_norm_g):
    c_act = jax.nn.silu(c)
    lb_fwd_all = jnp.cumsum(jax.nn.softmax(lb_gamma_fwd.astype(jnp.float32), axis=0), axis=0)
    lb_bwd_all = jnp.cumsum(jax.nn.softmax(lb_gamma_bwd.astype(jnp.float32), axis=0), axis=0)
    splits = [D_HGRN, 2 * D_HGRN, 3 * D_HGRN, 4 * D_HGRN, 5 * D_HGRN, 5 * D_HGRN + D_CONV]
    h = x
    for l in range(DEPTH):
        mod = c_act @ ada_w[l] + ada_b[l]
        sh1, sc1, g1, sh2, sc2, g2 = jnp.split(mod, 6, axis=-1)
        hn = _modulate(_rmsnorm(h, norm_mix_g[l]), sh1, sc1)
        proj = hn @ w_in[l]
        q_r, ff_r, fb_r, i_r, g_r, c_a, c_g = jnp.split(proj, splits, axis=-1)
        o_h = _hgrn2(q_r, ff_r, fb_r, i_r, g_r, lb_fwd_all[l], lb_bwd_all[l], hgrn_norm_g[l])
        o_c = _conv_module(c_a, c_g, conv_w[l], conv_b[l], conv_ln_g[l], conv_ln_b[l])
        mix = jnp.concatenate([o_h, o_c], axis=-1) @ w_out[l]
        h = h + g1[:, None, :] * mix
        hn2 = _modulate(_rmsnorm(h, norm_ffn_g[l]), sh2, sc2)
        ffn = _peer(hn2, peer_wq[l], peer_keys1[l], peer_keys2[l], peer_u[l], peer_v[l])
        h = h + g2[:, None, :] * ffn
    fsh, fsc = jnp.split(c_act @ final_ada_w + final_ada_b, 2, axis=-1)
    return _modulate(_rmsnorm(h, final_norm_g), fsh, fsc)
```

```python
import functools

import jax
import jax.numpy as jnp
from jax import lax
from jax.experimental import pallas as pl
from jax.experimental.pallas import tpu as pltpu

F32 = jnp.float32
BF16 = jnp.bfloat16
I32 = jnp.int32

EPS = 1e-6
HEAD_DIM = 128
CHUNK = 64
CONV_WIDTH = 31
CONV_PAD = CONV_WIDTH // 2
PEER_HEADS = 8
PEER_KEYS = 128
PEER_TOPK = 16
PEER_HALF = 128
LANES = 128
SUBLANES = 8
VMEM_LIMIT = 48 * 1024 * 1024


def _cparams(sem):
    return pltpu.CompilerParams(dimension_semantics=sem, vmem_limit_bytes=VMEM_LIMIT)


def _silu(x):
    return x * jax.nn.sigmoid(x)


def _split3(x):
    x1 = x.astype(BF16)
    r1 = x - x1.astype(F32)
    x2 = r1.astype(BF16)
    x3 = (r1 - x2.astype(F32)).astype(BF16)
    return x1, x2, x3


def _dot_nt(a, b):
    return lax.dot_general(a, b, (((1,), (1,)), ((), ())), preferred_element_type=F32)


def _dot_tn(a, b):
    return lax.dot_general(a, b, (((0,), (0,)), ((), ())), preferred_element_type=F32)


def _ada_kernel(c_ref, w_ref, b_ref, o_ref):
    ca = _silu(c_ref[...])
    o_ref[...] = jnp.dot(ca, w_ref[...], preferred_element_type=F32,
                         precision=lax.Precision.HIGHEST) + b_ref[...]


def _ada(c_pad, w, b):
    d, n = w.shape
    tn = 1024
    return pl.pallas_call(
        _ada_kernel,
        grid=(n // tn,),
        in_specs=[pl.BlockSpec((SUBLANES, d), lambda j: (0, 0)),
                  pl.BlockSpec((d, tn), lambda j: (0, j)),
                  pl.BlockSpec((1, tn), lambda j: (0, j))],
        out_specs=pl.BlockSpec((SUBLANES, tn), lambda j: (0, j)),
        out_shape=jax.ShapeDtypeStruct((SUBLANES, n), F32),
        compiler_params=_cparams(("arbitrary",)),
        name="ada_proj",
    )(c_pad, w, b.reshape(1, n))


def _rms_mod(x, g, sh, sc):
    y = x * lax.rsqrt(jnp.mean(x * x, axis=-1, keepdims=True) + EPS) * g
    return y * (1.0 + sc) + sh


def _inproj_kernel(x_ref, g_ref, sh_ref, sc_ref, w_ref, o_ref):
    hn = _rms_mod(x_ref[0], g_ref[...], sh_ref[0], sc_ref[0])
    o_ref[0] = jnp.dot(hn.astype(BF16), w_ref[...], preferred_element_type=F32)


def _inproj(x, g, mod3, w_bf16):
    nb, L, d = x.shape
    n = w_bf16.shape[1]
    tt = 512
    return pl.pallas_call(
        _inproj_kernel,
        grid=(nb, L // tt),
        in_specs=[pl.BlockSpec((1, tt, d), lambda b, j: (b, j, 0)),
                  pl.BlockSpec((1, d), lambda b, j: (0, 0)),
                  pl.BlockSpec((1, 1, d), lambda b, j: (b, 0, 0)),
                  pl.BlockSpec((1, 1, d), lambda b, j: (b, 0, 1)),
                  pl.BlockSpec((d, n), lambda b, j: (0, 0))],
        out_specs=pl.BlockSpec((1, tt, n), lambda b, j: (b, j, 0)),
        out_shape=jax.ShapeDtypeStruct((nb, L, n), F32),
        compiler_params=_cparams(("arbitrary", "arbitrary")),
        name="in_proj",
    )(x, g.reshape(1, d), mod3, mod3, w_bf16)


def _hgrn_kernel(q_ref, f_ref, i_ref, lbg_ref, o_ref, st_ref, *, reverse, lblk):
    @pl.when(pl.program_id(2) == 0)
    def _():
        st_ref[...] = jnp.zeros_like(st_ref)

    lbg = lbg_ref[...]
    ex = jnp.exp(lbg - jnp.max(lbg, axis=0, keepdims=True))
    lb = ex[0:1] / jnp.sum(ex, axis=0, keepdims=True)

    row = lax.broadcasted_iota(I32, (CHUNK, CHUNK), 0)
    col = lax.broadcasted_iota(I32, (CHUNK, CHUNK), 1)
    keep = (col >= row) if reverse else (col <= row)
    tri = keep.astype(BF16)
    i_mid = CHUNK // 2 if reverse else CHUNK // 2 - 1
    i_end = 0 if reverse else CHUNK - 1
    nch = lblk // CHUNK

    def body(ci, st):
        c = (nch - 1 - ci) if reverse else ci
        r0 = pl.multiple_of(c * CHUNK, CHUNK)
        z = f_ref[0, pl.ds(r0, CHUNK), :]
        q = _silu(q_ref[0, pl.ds(r0, CHUNK), :])
        v = i_ref[0, pl.ds(r0, CHUNK), :]
        logf = jnp.log(lb + (1.0 - lb) * jax.nn.sigmoid(z))
        k = (1.0 - lb) * jax.nn.sigmoid(-z)
        l1, l2, l3 = _split3(logf)
        cum = (jnp.dot(tri, l1, preferred_element_type=F32)
               + jnp.dot(tri, l2, preferred_element_type=F32)
               + jnp.dot(tri, l3, preferred_element_type=F32))
        c_mid = cum[i_mid:i_mid + 1]
        c_end = cum[i_end:i_end + 1]
        qr = (q * jnp.exp(cum - c_mid)).astype(BF16)
        kr = (k * jnp.exp(c_mid - cum)).astype(BF16)
        scores = jnp.where(keep, _dot_nt(qr, kr), 0.0)
        vb = v.astype(BF16)
        o_intra = jnp.dot(scores.astype(BF16), vb, preferred_element_type=F32)
        o_inter = _dot_nt((q * jnp.exp(cum)).astype(BF16), st.astype(BF16))
        o_ref[0, pl.ds(r0, CHUNK), :] = o_intra + o_inter
        ku = (k * jnp.exp(c_end - cum)).astype(BF16)
        return st * jnp.exp(c_end) + _dot_tn(vb, ku)

    st_ref[...] = lax.fori_loop(0, nch, body, st_ref[...])


def _hgrn_dir(proj, lb_gamma, f_col, reverse):
    nb, L, _ = proj.shape
    nh = lb_gamma.shape[1] // HEAD_DIM
    lblk = min(1024, L)
    nblk = L // lblk

    def blk(j):
        return (nblk - 1 - j) if reverse else j

    def col_spec(c0):
        return pl.BlockSpec((1, lblk, HEAD_DIM), lambda b, h, j: (b, blk(j), c0 + h))

    return pl.pallas_call(
        functools.partial(_hgrn_kernel, reverse=reverse, lblk=lblk),
        grid=(nb, nh, nblk),
        in_specs=[col_spec(0), col_spec(f_col), col_spec(3 * nh),
                  pl.BlockSpec((lb_gamma.shape[0], HEAD_DIM), lambda b, h, j: (0, h))],
        out_specs=pl.BlockSpec((1, lblk, HEAD_DIM), lambda b, h, j: (b, blk(j), h)),
        out_shape=jax.ShapeDtypeStruct((nb, L, nh * HEAD_DIM), F32),
        scratch_shapes=[pltpu.VMEM((HEAD_DIM, HEAD_DIM), F32)],
        compiler_params=_cparams(("arbitrary", "arbitrary", "arbitrary")),
        name="hgrn_bwd" if reverse else "hgrn_fwd",
    )(proj, proj, proj, lb_gamma)


CONV_HALO = 16
CONV_ROWS = 64


def _conv_kernel(a_ref, g_ref, ap_ref, gp_ref, an_ref, gn_ref, w_ref, b_ref, lg_ref, lb_ref,
                 o_ref, hbuf, *, tl):
    j = pl.program_id(1)
    last = pl.num_programs(1) - 1
    hbuf[CONV_HALO:CONV_HALO + tl, :] = a_ref[0] * jax.nn.sigmoid(g_ref[0])
    hp = ap_ref[0] * jax.nn.sigmoid(gp_ref[0])
    hbuf[0:CONV_HALO, :] = jnp.where(j > 0, hp, 0.0)
    hn = an_ref[0] * jax.nn.sigmoid(gn_ref[0])
    hbuf[CONV_HALO + tl:2 * CONV_HALO + tl, :] = jnp.where(j < last, hn, 0.0)
    off = CONV_HALO - CONV_PAD
    for r in range(tl // CONV_ROWS):
        acc = jnp.zeros((CONV_ROWS, a_ref.shape[2]), F32)
        for k in range(CONV_WIDTH):
            s = r * CONV_ROWS + k + off
            acc = acc + w_ref[k:k + 1, :] * hbuf[s:s + CONV_ROWS, :]
        acc = acc + b_ref[...]
        mu = jnp.mean(acc, axis=-1, keepdims=True)
        cen = acc - mu
        var = jnp.mean(cen * cen, axis=-1, keepdims=True)
        y = cen * lax.rsqrt(var + EPS) * lg_ref[...] + lb_ref[...]
        o_ref[0, r * CONV_ROWS:(r + 1) * CONV_ROWS, :] = _silu(y)


def _conv(proj, w_pad, bias, ln_g, ln_b, a_col):
    nb, L, _ = proj.shape
    dc = w_pad.shape[1]
    tl = 256
    hb = tl // CONV_HALO
    nhalo = L // CONV_HALO

    def cur(c):
        return pl.BlockSpec((1, tl, dc), lambda b, j: (b, j, c))

    def prev(c):
        return pl.BlockSpec((1, CONV_HALO, dc), lambda b, j: (b, jnp.maximum(j * hb - 1, 0), c))

    def nxt(c):
        return pl.BlockSpec((1, CONV_HALO, dc),
                            lambda b, j: (b, jnp.minimum((j + 1) * hb, nhalo - 1), c))

    vec = pl.BlockSpec((1, dc), lambda b, j: (0, 0))
    return pl.pallas_call(
        functools.partial(_conv_kernel, tl=tl),
        grid=(nb, L // tl),
        in_specs=[cur(a_col), cur(a_col + 1), prev(a_col), prev(a_col + 1), nxt(a_col),
                  nxt(a_col + 1), pl.BlockSpec(w_pad.shape, lambda b, j: (0, 0)), vec, vec, vec],
        out_specs=pl.BlockSpec((1, tl, dc), lambda b, j: (b, j, 0)),
        out_shape=jax.ShapeDtypeStruct((nb, L, dc), F32),
        scratch_shapes=[pltpu.VMEM((tl + 2 * CONV_HALO, dc), F32)],
        compiler_params=_cparams(("arbitrary", "arbitrary")),
        name="conv_group",
    )(proj, proj, proj, proj, proj, proj, w_pad, bias.reshape(1, dc), ln_g.reshape(1, dc),
      ln_b.reshape(1, dc))


def _mix_kernel(of_ref, ob_ref, gr_ref, oc_ref, x_ref, hg_ref, g1_ref, sh2_ref, sc2_ref, ng_ref,
                wo_ref, wq_ref, k1_ref, k2_ref, h1_ref, hn2_ref, s_ref):
    o = of_ref[0] + ob_ref[0]
    parts = []
    for hh in range(o.shape[1] // HEAD_DIM):
        oh = o[:, hh * HEAD_DIM:(hh + 1) * HEAD_DIM]
        parts.append(oh * lax.rsqrt(jnp.mean(oh * oh, axis=-1, keepdims=True) + EPS))
    on = jnp.concatenate(parts, axis=-1) * hg_ref[...] * _silu(gr_ref[0])
    cat = jnp.concatenate([on, oc_ref[0]], axis=-1).astype(BF16)
    mix = jnp.dot(cat, wo_ref[...], preferred_element_type=F32)
    h1 = x_ref[0] + g1_ref[0] * mix
    h1_ref[0] = h1
    hn2 = _rms_mod(h1, ng_ref[...], sh2_ref[0], sc2_ref[0])
    hn2_ref[0] = hn2
    q = jnp.dot(hn2.astype(BF16), wq_ref[...], preferred_element_type=F32)
    k1h, k1l, _ = _split3(k1_ref[...])
    k2h, k2l, _ = _split3(k2_ref[...])
    for hd in range(PEER_HEADS):
        for half, (kh, kl) in enumerate(((k1h, k1l), (k2h, k2l))):
            c0 = hd * 2 * PEER_HALF + half * PEER_HALF
            qh, ql, _ = _split3(q[:, c0:c0 + PEER_HALF])
            s_ref[2 * hd + half] = _dot_nt(kh, qh) + _dot_nt(kh, ql) + _dot_nt(kl, qh)


def _mix(o_f, o_b, proj, o_c, x, hgrn_g, mod3, norm_g, w_out_bf16, wq_bf16, keys1, keys2, g_col):
    nb, L, d = x.shape
    dh = o_f.shape[2]
    dq = wq_bf16.shape[1]
    tt = 256
    nj = L // tt

    def half(c=0):
        return pl.BlockSpec((1, tt, dh), lambda b, j: (b, j, c))

    def full():
        return pl.BlockSpec((1, tt, d), lambda b, j: (b, j, 0))

    def modc(c):
        return pl.BlockSpec((1, 1, d), lambda b, j: (b, 0, c))

    def const(shape):
        return pl.BlockSpec(shape, lambda b, j: (0,) * len(shape))

    return pl.pallas_call(
        _mix_kernel,
        grid=(nb, nj),
        in_specs=[half(), half(), half(g_col), half(), full(), const((1, dh)),
                  modc(2), modc(3), modc(4), const((1, d)), const((d, d)), const((d, dq)),
                  const(keys1.shape), const(keys2.shape)],
        out_specs=[full(), full(),
                   pl.BlockSpec((2 * PEER_HEADS, PEER_KEYS, tt), lambda b, j: (0, 0, b * nj + j))],
        out_shape=[jax.ShapeDtypeStruct((nb, L, d), F32), jax.ShapeDtypeStruct((nb, L, d), F32),
                   jax.ShapeDtypeStruct((2 * PEER_HEADS, PEER_KEYS, nb * L), F32)],
        compiler_params=_cparams(("arbitrary", "arbitrary")),
        name="mix_scores",
    )(o_f, o_b, proj, o_c, x, hgrn_g.reshape(1, dh), mod3, mod3, mod3, norm_g.reshape(1, d),
      w_out_bf16, wq_bf16, keys1, keys2)


def _top16(s):
    n = s.shape[0]
    iota = lax.broadcasted_iota(I32, s.shape, 0)
    vals, idxs = [], []
    for _ in range(PEER_TOPK):
        m = jnp.max(s, axis=0, keepdims=True)
        idx = jnp.min(jnp.where(s == m, iota, n), axis=0, keepdims=True)
        vals.append(m)
        idxs.append(idx)
        s = jnp.where(iota == idx, -jnp.inf, s)
    return jnp.concatenate(vals, axis=0), jnp.concatenate(idxs, axis=0)


def _take16(table, sel):
    out = jnp.zeros_like(table)
    for a in range(PEER_TOPK):
        out = jnp.where(sel == a, table[a:a + 1], out)
    return out


def _topk_kernel(s_ref, e_ref, w_ref):
    es, ws = [], []
    for hd in range(PEER_HEADS):
        v1, i1 = _top16(s_ref[2 * hd])
        v2, i2 = _top16(s_ref[2 * hd + 1])
        cand = jnp.concatenate([v1[a:a + 1] + v2 for a in range(PEER_TOPK)], axis=0)
        sc, ci = _top16(cand)
        e = _take16(i1, ci // PEER_TOPK) * PEER_KEYS + _take16(i2, ci % PEER_TOPK)
        p = jnp.exp(sc - sc[0:1])
        es.append(e)
        ws.append(p / jnp.sum(p, axis=0, keepdims=True))
    e_ref[...] = jnp.concatenate(es, axis=0).T
    w_ref[...] = jnp.concatenate(ws, axis=0).T


def _topk(scores):
    npair, nk, T = scores.shape
    nsel = PEER_HEADS * PEER_TOPK
    tt = 256
    return pl.pallas_call(
        _topk_kernel,
        grid=(T // tt,),
        in_specs=[pl.BlockSpec((npair, nk, tt), lambda i: (0, 0, i))],
        out_specs=[pl.BlockSpec((tt, nsel), lambda i: (i, 0)),
                   pl.BlockSpec((tt, nsel), lambda i: (i, 0))],
        out_shape=[jax.ShapeDtypeStruct((T, nsel), I32), jax.ShapeDtypeStruct((T, nsel), F32)],
        compiler_params=_cparams(("arbitrary",)),
        name="peer_topk",
    )(scores)


WORDS = 4


def _pack_words(x):
    bits = pltpu.bitcast(x.astype(BF16).astype(F32), I32)
    out = []
    for c in range(WORDS):
        lo = bits[:, (2 * c) * LANES:(2 * c + 1) * LANES]
        hi = bits[:, (2 * c + 1) * LANES:(2 * c + 2) * LANES]
        out.append(lax.shift_right_logical(lo, 16) | (hi & jnp.int32(-65536)))
    return jnp.concatenate(out, axis=-1)


def _pack_kernel(u_ref, v_ref, o_ref):
    o_ref[...] = jnp.concatenate([_pack_words(u_ref[...]), _pack_words(v_ref[...])], axis=-1)


def _pack_table(u, v):
    ne, d = u.shape
    te = 512
    return pl.pallas_call(
        _pack_kernel,
        grid=(ne // te,),
        in_specs=[pl.BlockSpec((te, d), lambda i: (i, 0)), pl.BlockSpec((te, d), lambda i: (i, 0))],
        out_specs=pl.BlockSpec((te, d), lambda i: (i, 0)),
        out_shape=jax.ShapeDtypeStruct((ne, d), I32),
        compiler_params=_cparams(("arbitrary",)),
        name="pack_table",
    )(u, v)


NSEL = PEER_HEADS * PEER_TOPK
ROW_TILE = 2 * WORDS


def _unpack(words):
    lo = pltpu.bitcast(lax.shift_left(words, 16), F32)
    hi = pltpu.bitcast(words & jnp.int32(-65536), F32)
    return lo, hi


def _expert_kernel(idx_hbm, tab_hbm, hn_ref, w_ref, y_ref, idx_smem, buf, sem_idx, sems, *, tb):
    i = pl.program_id(0)
    cp = pltpu.make_async_copy(idx_hbm.at[pl.ds(i * (tb * NSEL), tb * NSEL)], idx_smem, sem_idx)
    cp.start()
    cp.wait()

    def issue(t, slot):
        def body(j, carry):
            e = idx_smem[t * NSEL + j]
            pltpu.make_async_copy(tab_hbm.at[e], buf.at[slot, pl.ds(j * ROW_TILE, ROW_TILE), :],
                                  sems.at[slot]).start()
            return carry
        lax.fori_loop(0, NSEL, body, 0)

    def wait(slot):
        pltpu.make_async_copy(buf.at[1 - slot], buf.at[slot], sems.at[slot]).wait()

    diag = (lax.broadcasted_iota(I32, (NSEL, NSEL), 0) == lax.broadcasted_iota(I32, (NSEL, NSEL), 1))

    def compute(t, slot):
        hrow = hn_ref[pl.ds(t, 1), :]
        acc = jnp.zeros((NSEL, LANES), F32)
        for c in range(WORDS):
            lo, hi = _unpack(buf[slot, pl.ds(c, NSEL, stride=ROW_TILE), :])
            acc = acc + lo * hrow[:, (2 * c) * LANES:(2 * c + 1) * LANES]
            acc = acc + hi * hrow[:, (2 * c + 1) * LANES:(2 * c + 2) * LANES]
        act = jnp.sum(acc, axis=1, keepdims=True)
        wcol = jnp.sum(jnp.where(diag, w_ref[pl.ds(t, 1), :], 0.0), axis=1, keepdims=True)
        coef = wcol * (0.5 * act * (1.0 + lax.erf(act * (2.0 ** -0.5))))
        pieces = []
        for c in range(WORDS):
            lo, hi = _unpack(buf[slot, pl.ds(WORDS + c, NSEL, stride=ROW_TILE), :])
            pieces.append(jnp.sum(coef * lo, axis=0, keepdims=True))
            pieces.append(jnp.sum(coef * hi, axis=0, keepdims=True))
        return jnp.concatenate(pieces, axis=-1)

    issue(0, 0)

    def group(g, carry):
        rows = []
        for u in range(SUBLANES):
            t = g * SUBLANES + u
            slot = u % 2

            @pl.when(t + 1 < tb)
            def _():
                issue(t + 1, 1 - slot)

            wait(slot)
            rows.append(compute(t, slot))
        y_ref[pl.ds(pl.multiple_of(g * SUBLANES, SUBLANES), SUBLANES), :] = jnp.concatenate(rows, axis=0)
        return carry

    lax.fori_loop(0, tb // SUBLANES, group, 0)


def _experts(idx_flat, table3, hn2, w):
    T, d = hn2.shape
    tb = 128
    return pl.pallas_call(
        functools.partial(_expert_kernel, tb=tb),
        grid=(T // tb,),
        in_specs=[pl.BlockSpec(memory_space=pl.ANY), pl.BlockSpec(memory_space=pl.ANY),
                  pl.BlockSpec((tb, d), lambda i: (i, 0)), pl.BlockSpec((tb, NSEL), lambda i: (i, 0))],
        out_specs=pl.BlockSpec((tb, d), lambda i: (i, 0)),
        out_shape=jax.ShapeDtypeStruct((T, d), F32),
        scratch_shapes=[pltpu.SMEM((tb * NSEL,), I32),
                        pltpu.VMEM((2, NSEL * ROW_TILE, LANES), I32),
                        pltpu.SemaphoreType.DMA(()),
                        pltpu.SemaphoreType.DMA((2,))],
        compiler_params=_cparams(("arbitrary",)),
        name="peer_experts",
    )(idx_flat, table3, hn2, w)


def _final_kernel(h1_ref, y_ref, g2_ref, fg_ref, fsh_ref, fsc_ref, o_ref):
    h = h1_ref[0] + g2_ref[0] * y_ref[0]
    o_ref[0] = _rms_mod(h, fg_ref[...], fsh_ref[0], fsc_ref[0])


def _final(h1, y, mod3, fmod3, final_g):
    nb, L, d = h1.shape
    tt = 512
    blk = pl.BlockSpec((1, tt, d), lambda b, j: (b, j, 0))
    return pl.pallas_call(
        _final_kernel,
        grid=(nb, L // tt),
        in_specs=[blk, blk, pl.BlockSpec((1, 1, d), lambda b, j: (b, 0, 5)),
                  pl.BlockSpec((1, d), lambda b, j: (0, 0)),
                  pl.BlockSpec((1, 1, d), lambda b, j: (b, 0, 0)),
                  pl.BlockSpec((1, 1, d), lambda b, j: (b, 0, 1))],
        out_specs=blk,
        out_shape=jax.ShapeDtypeStruct((nb, L, d), F32),
        compiler_params=_cparams(("arbitrary", "arbitrary")),
        name="final_norm",
    )(h1, y, mod3, final_g.reshape(1, d), fmod3, fmod3)


def kernel(x, c, ada_w, ada_b, norm_mix_g, w_in, lb_gamma_fwd, lb_gamma_bwd, hgrn_norm_g, conv_w,
           conv_b, conv_ln_g, conv_ln_b, w_out, norm_ffn_g, peer_wq, peer_keys1, peer_keys2, peer_u,
           peer_v, final_ada_w, final_ada_b, final_norm_g):
    nb, L, d = x.shape
    assert ada_w.shape[0] == 1, "single-layer trunk"
    d_hgrn = lb_gamma_fwd.shape[1]
    nh = d_hgrn // HEAD_DIM

    c_pad = jnp.pad(c, ((0, SUBLANES - nb), (0, 0)))
    mod3 = _ada(c_pad, ada_w[0], ada_b[0])[:nb].reshape(nb, 1, 6 * d)
    fmod3 = _ada(c_pad, final_ada_w, final_ada_b)[:nb].reshape(nb, 1, 2 * d)

    proj = _inproj(x, norm_mix_g[0], mod3, w_in[0].astype(BF16))
    o_f = _hgrn_dir(proj, lb_gamma_fwd, nh, reverse=False)
    o_b = _hgrn_dir(proj, lb_gamma_bwd, 2 * nh, reverse=True)
    d_conv = conv_w.shape[2]
    w_pad = jnp.pad(conv_w[0], ((0, 1), (0, 0)))
    o_c = _conv(proj, w_pad, conv_b[0], conv_ln_g[0], conv_ln_b[0], a_col=5 * d_hgrn // d_conv)

    h1, hn2, scores = _mix(o_f, o_b, proj, o_c, x, hgrn_norm_g[0], mod3, norm_ffn_g[0],
                           w_out[0].astype(BF16), peer_wq[0].astype(BF16), peer_keys1[0],
                           peer_keys2[0], g_col=4)
    e, w = _topk(scores)

    table = _pack_table(peer_u[0], peer_v[0])
    table3 = table.reshape(table.shape[0], ROW_TILE, LANES)
    T = nb * L
    y = _experts(e.reshape(T * NSEL), table3, hn2.reshape(T, d), w)
    return _final(h1, y.reshape(nb, L, d), mod3, fmod3, final_norm_g)
```

```python
import functools

import jax
import jax.numpy as jnp
from jax import lax
from jax.experimental import pallas as pl
from jax.experimental.pallas import tpu as pltpu
from jax.experimental.pallas import tpu_sc as plsc

F32 = jnp.float32
BF16 = jnp.bfloat16
I32 = jnp.int32

EPS = 1e-6
HEAD_DIM = 128
CHUNK = 64
CONV_WIDTH = 31
CONV_PAD = CONV_WIDTH // 2
PEER_HEADS = 8
PEER_KEYS = 128
PEER_TOPK = 16
PEER_HALF = 128
LANES = 128
SUBLANES = 8
VMEM_LIMIT = 48 * 1024 * 1024


def _cparams(sem):
    return pltpu.CompilerParams(dimension_semantics=sem, vmem_limit_bytes=VMEM_LIMIT)


def _silu(x):
    return x * jax.nn.sigmoid(x)


def _split3(x):
    x1 = x.astype(BF16)
    r1 = x - x1.astype(F32)
    x2 = r1.astype(BF16)
    x3 = (r1 - x2.astype(F32)).astype(BF16)
    return x1, x2, x3


def _dot_nt(a, b):
    return lax.dot_general(a, b, (((1,), (1,)), ((), ())), preferred_element_type=F32)


def _dot_tn(a, b):
    return lax.dot_general(a, b, (((0,), (0,)), ((), ())), preferred_element_type=F32)


def _ada_kernel(c_ref, w_ref, b_ref, o_ref):
    ca = _silu(c_ref[...])
    o_ref[...] = jnp.dot(ca, w_ref[...], preferred_element_type=F32,
                         precision=lax.Precision.HIGHEST) + b_ref[...]


def _ada(c_pad, w, b):
    d, n = w.shape
    tn = 1024
    return pl.pallas_call(
        _ada_kernel,
        grid=(n // tn,),
        in_specs=[pl.BlockSpec((SUBLANES, d), lambda j: (0, 0)),
                  pl.BlockSpec((d, tn), lambda j: (0, j)),
                  pl.BlockSpec((1, tn), lambda j: (0, j))],
        out_specs=pl.BlockSpec((SUBLANES, tn), lambda j: (0, j)),
        out_shape=jax.ShapeDtypeStruct((SUBLANES, n), F32),
        compiler_params=_cparams(("arbitrary",)),
        name="ada_proj",
    )(c_pad, w, b.reshape(1, n))


def _rms_mod(x, g, sh, sc):
    y = x * lax.rsqrt(jnp.mean(x * x, axis=-1, keepdims=True) + EPS) * g
    return y * (1.0 + sc) + sh


def _inproj_kernel(x_ref, g_ref, sh_ref, sc_ref, w_ref, o_ref):
    hn = _rms_mod(x_ref[0], g_ref[...], sh_ref[0], sc_ref[0])
    o_ref[0] = jnp.dot(hn.astype(BF16), w_ref[...], preferred_element_type=F32)


def _inproj(x, g, mod3, w_bf16):
    nb, L, d = x.shape
    n = w_bf16.shape[1]
    tt = 512
    return pl.pallas_call(
        _inproj_kernel,
        grid=(nb, L // tt),
        in_specs=[pl.BlockSpec((1, tt, d), lambda b, j: (b, j, 0)),
                  pl.BlockSpec((1, d), lambda b, j: (0, 0)),
                  pl.BlockSpec((1, 1, d), lambda b, j: (b, 0, 0)),
                  pl.BlockSpec((1, 1, d), lambda b, j: (b, 0, 1)),
                  pl.BlockSpec((d, n), lambda b, j: (0, 0))],
        out_specs=pl.BlockSpec((1, tt, n), lambda b, j: (b, j, 0)),
        out_shape=jax.ShapeDtypeStruct((nb, L, n), F32),
        compiler_params=_cparams(("arbitrary", "arbitrary")),
        name="in_proj",
    )(x, g.reshape(1, d), mod3, mod3, w_bf16)


def _hgrn_kernel(q_ref, f_ref, i_ref, lbg_ref, o_ref, st_ref, *, reverse, lblk):
    @pl.when(pl.program_id(2) == 0)
    def _():
        st_ref[...] = jnp.zeros_like(st_ref)

    lbg = lbg_ref[...]
    ex = jnp.exp(lbg - jnp.max(lbg, axis=0, keepdims=True))
    lb = ex[0:1] / jnp.sum(ex, axis=0, keepdims=True)

    row = lax.broadcasted_iota(I32, (CHUNK, CHUNK), 0)
    col = lax.broadcasted_iota(I32, (CHUNK, CHUNK), 1)
    keep = (col >= row) if reverse else (col <= row)
    tri = keep.astype(BF16)
    i_mid = CHUNK // 2 if reverse else CHUNK // 2 - 1
    i_end = 0 if reverse else CHUNK - 1
    nch = lblk // CHUNK

    def body(ci, st):
        c = (nch - 1 - ci) if reverse else ci
        r0 = pl.multiple_of(c * CHUNK, CHUNK)
        z = f_ref[0, pl.ds(r0, CHUNK), :]
        q = _silu(q_ref[0, pl.ds(r0, CHUNK), :])
        v = i_ref[0, pl.ds(r0, CHUNK), :]
        logf = jnp.log(lb + (1.0 - lb) * jax.nn.sigmoid(z))
        k = (1.0 - lb) * jax.nn.sigmoid(-z)
        l1, l2, l3 = _split3(logf)
        cum = (jnp.dot(tri, l1, preferred_element_type=F32)
               + jnp.dot(tri, l2, preferred_element_type=F32)
               + jnp.dot(tri, l3, preferred_element_type=F32))
        c_mid = cum[i_mid:i_mid + 1]
        c_end = cum[i_end:i_end + 1]
        qr = (q * jnp.exp(cum - c_mid)).astype(BF16)
        kr = (k * jnp.exp(c_mid - cum)).astype(BF16)
        scores = jnp.where(keep, _dot_nt(qr, kr), 0.0)
        vb = v.astype(BF16)
        o_intra = jnp.dot(scores.astype(BF16), vb, preferred_element_type=F32)
        o_inter = _dot_nt((q * jnp.exp(cum)).astype(BF16), st.astype(BF16))
        o_ref[0, pl.ds(r0, CHUNK), :] = o_intra + o_inter
        ku = (k * jnp.exp(c_end - cum)).astype(BF16)
        return st * jnp.exp(c_end) + _dot_tn(vb, ku)

    st_ref[...] = lax.fori_loop(0, nch, body, st_ref[...])


def _hgrn_dir(proj, lb_gamma, f_col, reverse):
    nb, L, _ = proj.shape
    nh = lb_gamma.shape[1] // HEAD_DIM
    lblk = min(1024, L)
    nblk = L // lblk

    def blk(j):
        return (nblk - 1 - j) if reverse else j

    def col_spec(c0):
        return pl.BlockSpec((1, lblk, HEAD_DIM), lambda b, h, j: (b, blk(j), c0 + h))

    return pl.pallas_call(
        functools.partial(_hgrn_kernel, reverse=reverse, lblk=lblk),
        grid=(nb, nh, nblk),
        in_specs=[col_spec(0), col_spec(f_col), col_spec(3 * nh),
                  pl.BlockSpec((lb_gamma.shape[0], HEAD_DIM), lambda b, h, j: (0, h))],
        out_specs=pl.BlockSpec((1, lblk, HEAD_DIM), lambda b, h, j: (b, blk(j), h)),
        out_shape=jax.ShapeDtypeStruct((nb, L, nh * HEAD_DIM), F32),
        scratch_shapes=[pltpu.VMEM((HEAD_DIM, HEAD_DIM), F32)],
        compiler_params=_cparams(("arbitrary", "arbitrary", "arbitrary")),
        name="hgrn_bwd" if reverse else "hgrn_fwd",
    )(proj, proj, proj, lb_gamma)


CONV_HALO = 16
CONV_ROWS = 64


def _conv_kernel(a_ref, g_ref, ap_ref, gp_ref, an_ref, gn_ref, w_ref, b_ref, lg_ref, lb_ref,
                 o_ref, hbuf, *, tl):
    j = pl.program_id(1)
    last = pl.num_programs(1) - 1
    hbuf[CONV_HALO:CONV_HALO + tl, :] = a_ref[0] * jax.nn.sigmoid(g_ref[0])
    hp = ap_ref[0] * jax.nn.sigmoid(gp_ref[0])
    hbuf[0:CONV_HALO, :] = jnp.where(j > 0, hp, 0.0)
    hn = an_ref[0] * jax.nn.sigmoid(gn_ref[0])
    hbuf[CONV_HALO + tl:2 * CONV_HALO + tl, :] = jnp.where(j < last, hn, 0.0)
    off = CONV_HALO - CONV_PAD
    for r in range(tl // CONV_ROWS):
        acc = jnp.zeros((CONV_ROWS, a_ref.shape[2]), F32)
        for k in range(CONV_WIDTH):
            s = r * CONV_ROWS + k + off
            acc = acc + w_ref[k:k + 1, :] * hbuf[s:s + CONV_ROWS, :]
        acc = acc + b_ref[...]
        mu = jnp.mean(acc, axis=-1, keepdims=True)
        cen = acc - mu
        var = jnp.mean(cen * cen, axis=-1, keepdims=True)
        y = cen * lax.rsqrt(var + EPS) * lg_ref[...] + lb_ref[...]
        o_ref[0, r * CONV_ROWS:(r + 1) * CONV_ROWS, :] = _silu(y)


def _conv(proj, w_pad, bias, ln_g, ln_b, a_col):
    nb, L, _ = proj.shape
    dc = w_pad.shape[1]
    tl = 256
    hb = tl // CONV_HALO
    nhalo = L // CONV_HALO

    def cur(c):
        return pl.BlockSpec((1, tl, dc), lambda b, j: (b, j, c))

    def prev(c):
        return pl.BlockSpec((1, CONV_HALO, dc), lambda b, j: (b, jnp.maximum(j * hb - 1, 0), c))

    def nxt(c):
        return pl.BlockSpec((1, CONV_HALO, dc),
                            lambda b, j: (b, jnp.minimum((j + 1) * hb, nhalo - 1), c))

    vec = pl.BlockSpec((1, dc), lambda b, j: (0, 0))
    return pl.pallas_call(
        functools.partial(_conv_kernel, tl=tl),
        grid=(nb, L // tl),
        in_specs=[cur(a_col), cur(a_col + 1), prev(a_col), prev(a_col + 1), nxt(a_col),
                  nxt(a_col + 1), pl.BlockSpec(w_pad.shape, lambda b, j: (0, 0)), vec, vec, vec],
        out_specs=pl.BlockSpec((1, tl, dc), lambda b, j: (b, j, 0)),
        out_shape=jax.ShapeDtypeStruct((nb, L, dc), F32),
        scratch_shapes=[pltpu.VMEM((tl + 2 * CONV_HALO, dc), F32)],
        compiler_params=_cparams(("arbitrary", "arbitrary")),
        name="conv_group",
    )(proj, proj, proj, proj, proj, proj, w_pad, bias.reshape(1, dc), ln_g.reshape(1, dc),
      ln_b.reshape(1, dc))


def _mix_kernel(of_ref, ob_ref, gr_ref, oc_ref, x_ref, hg_ref, g1_ref, sh2_ref, sc2_ref, ng_ref,
                wo_ref, wq_ref, k1_ref, k2_ref, h1_ref, hn2_ref, s_ref):
    o = of_ref[0] + ob_ref[0]
    parts = []
    for hh in range(o.shape[1] // HEAD_DIM):
        oh = o[:, hh * HEAD_DIM:(hh + 1) * HEAD_DIM]
        parts.append(oh * lax.rsqrt(jnp.mean(oh * oh, axis=-1, keepdims=True) + EPS))
    on = jnp.concatenate(parts, axis=-1) * hg_ref[...] * _silu(gr_ref[0])
    cat = jnp.concatenate([on, oc_ref[0]], axis=-1).astype(BF16)
    mix = jnp.dot(cat, wo_ref[...], preferred_element_type=F32)
    h1 = x_ref[0] + g1_ref[0] * mix
    h1_ref[0] = h1
    hn2 = _rms_mod(h1, ng_ref[...], sh2_ref[0], sc2_ref[0])
    hn2_ref[0] = hn2
    q = jnp.dot(hn2.astype(BF16), wq_ref[...], preferred_element_type=F32)
    k1h, k1l, _ = _split3(k1_ref[...])
    k2h, k2l, _ = _split3(k2_ref[...])
    for hd in range(PEER_HEADS):
        for half, (kh, kl) in enumerate(((k1h, k1l), (k2h, k2l))):
            c0 = hd * 2 * PEER_HALF + half * PEER_HALF
            qh, ql, _ = _split3(q[:, c0:c0 + PEER_HALF])
            s_ref[2 * hd + half] = _dot_nt(kh, qh) + _dot_nt(kh, ql) + _dot_nt(kl, qh)


def _mix(o_f, o_b, proj, o_c, x, hgrn_g, mod3, norm_g, w_out_bf16, wq_bf16, keys1, keys2, g_col):
    nb, L, d = x.shape
    dh = o_f.shape[2]
    dq = wq_bf16.shape[1]
    tt = 256
    nj = L // tt

    def half(c=0):
        return pl.BlockSpec((1, tt, dh), lambda b, j: (b, j, c))

    def full():
        return pl.BlockSpec((1, tt, d), lambda b, j: (b, j, 0))

    def modc(c):
        return pl.BlockSpec((1, 1, d), lambda b, j: (b, 0, c))

    def const(shape):
        return pl.BlockSpec(shape, lambda b, j: (0,) * len(shape))

    return pl.pallas_call(
        _mix_kernel,
        grid=(nb, nj),
        in_specs=[half(), half(), half(g_col), half(), full(), const((1, dh)),
                  modc(2), modc(3), modc(4), const((1, d)), const((d, d)), const((d, dq)),
                  const(keys1.shape), const(keys2.shape)],
        out_specs=[full(), full(),
                   pl.BlockSpec((2 * PEER_HEADS, PEER_KEYS, tt), lambda b, j: (0, 0, b * nj + j))],
        out_shape=[jax.ShapeDtypeStruct((nb, L, d), F32), jax.ShapeDtypeStruct((nb, L, d), F32),
                   jax.ShapeDtypeStruct((2 * PEER_HEADS, PEER_KEYS, nb * L), F32)],
        compiler_params=_cparams(("arbitrary", "arbitrary")),
        name="mix_scores",
    )(o_f, o_b, proj, o_c, x, hgrn_g.reshape(1, dh), mod3, mod3, mod3, norm_g.reshape(1, d),
      w_out_bf16, wq_bf16, keys1, keys2)


def _top16(s):
    n = s.shape[0]
    iota = lax.broadcasted_iota(I32, s.shape, 0)
    vals, idxs = [], []
    for _ in range(PEER_TOPK):
        m = jnp.max(s, axis=0, keepdims=True)
        idx = jnp.min(jnp.where(s == m, iota, n), axis=0, keepdims=True)
        vals.append(m)
        idxs.append(idx)
        s = jnp.where(iota == idx, -jnp.inf, s)
    return jnp.concatenate(vals, axis=0), jnp.concatenate(idxs, axis=0)


def _take16(table, sel):
    out = jnp.zeros_like(table)
    for a in range(PEER_TOPK):
        out = jnp.where(sel == a, table[a:a + 1], out)
    return out


def _topk_kernel(s_ref, e_ref, w_ref):
    es, ws = [], []
    for hd in range(PEER_HEADS):
        v1, i1 = _top16(s_ref[2 * hd])
        v2, i2 = _top16(s_ref[2 * hd + 1])
        cand = jnp.concatenate([v1[a:a + 1] + v2 for a in range(PEER_TOPK)], axis=0)
        sc, ci = _top16(cand)
        e = _take16(i1, ci // PEER_TOPK) * PEER_KEYS + _take16(i2, ci % PEER_TOPK)
        p = jnp.exp(sc - sc[0:1])
        es.append(e)
        ws.append(p / jnp.sum(p, axis=0, keepdims=True))
    e_ref[...] = jnp.concatenate(es, axis=0).T
    w_ref[...] = jnp.concatenate(ws, axis=0).T


def _topk(scores):
    npair, nk, T = scores.shape
    nsel = PEER_HEADS * PEER_TOPK
    tt = 256
    return pl.pallas_call(
        _topk_kernel,
        grid=(T // tt,),
        in_specs=[pl.BlockSpec((npair, nk, tt), lambda i: (0, 0, i))],
        out_specs=[pl.BlockSpec((tt, nsel), lambda i: (i, 0)),
                   pl.BlockSpec((tt, nsel), lambda i: (i, 0))],
        out_shape=[jax.ShapeDtypeStruct((T, nsel), I32), jax.ShapeDtypeStruct((T, nsel), F32)],
        compiler_params=_cparams(("arbitrary",)),
        name="peer_topk",
    )(scores)


WORDS = 4


def _pack_words(x):
    bits = pltpu.bitcast(x.astype(BF16).astype(F32), I32)
    out = []
    for c in range(WORDS):
        lo = bits[:, (2 * c) * LANES:(2 * c + 1) * LANES]
        hi = bits[:, (2 * c + 1) * LANES:(2 * c + 2) * LANES]
        out.append(lax.shift_right_logical(lo, 16) | (hi & jnp.int32(-65536)))
    return jnp.concatenate(out, axis=-1)


def _pack_kernel(u_ref, v_ref, o_ref):
    o_ref[...] = jnp.concatenate([_pack_words(u_ref[...]), _pack_words(v_ref[...])], axis=-1)


def _pack_table(u, v):
    ne, d = u.shape
    te = 512
    return pl.pallas_call(
        _pack_kernel,
        grid=(ne // te,),
        in_specs=[pl.BlockSpec((te, d), lambda i: (i, 0)), pl.BlockSpec((te, d), lambda i: (i, 0))],
        out_specs=pl.BlockSpec((te, d), lambda i: (i, 0)),
        out_shape=jax.ShapeDtypeStruct((ne, d), I32),
        compiler_params=_cparams(("arbitrary",)),
        name="pack_table",
    )(u, v)


NSEL = PEER_HEADS * PEER_TOPK
ROW_TILE = 2 * WORDS


def _unpack(words):
    lo = pltpu.bitcast(lax.shift_left(words, 16), F32)
    hi = pltpu.bitcast(words & jnp.int32(-65536), F32)
    return lo, hi


def _apply_kernel(g_ref, hn_ref, w_ref, y_ref, *, tb):
    diag = (lax.broadcasted_iota(I32, (NSEL, NSEL), 0) == lax.broadcasted_iota(I32, (NSEL, NSEL), 1))

    def compute(t):
        hrow = hn_ref[pl.ds(t, 1), :]
        acc = jnp.zeros((NSEL, LANES), F32)
        for c in range(WORDS):
            lo, hi = _unpack(g_ref[t, pl.ds(c, NSEL, stride=ROW_TILE), :])
            acc = acc + lo * hrow[:, (2 * c) * LANES:(2 * c + 1) * LANES]
            acc = acc + hi * hrow[:, (2 * c + 1) * LANES:(2 * c + 2) * LANES]
        act = jnp.sum(acc, axis=1, keepdims=True)
        wcol = jnp.sum(jnp.where(diag, w_ref[pl.ds(t, 1), :], 0.0), axis=1, keepdims=True)
        coef = wcol * (0.5 * act * (1.0 + lax.erf(act * (2.0 ** -0.5))))
        pieces = []
        for c in range(WORDS):
            lo, hi = _unpack(g_ref[t, pl.ds(WORDS + c, NSEL, stride=ROW_TILE), :])
            pieces.append(jnp.sum(coef * lo, axis=0, keepdims=True))
            pieces.append(jnp.sum(coef * hi, axis=0, keepdims=True))
        return jnp.concatenate(pieces, axis=-1)

    def group(g, carry):
        rows = [compute(g * SUBLANES + u) for u in range(SUBLANES)]
        y_ref[pl.ds(pl.multiple_of(g * SUBLANES, SUBLANES), SUBLANES), :] = jnp.concatenate(rows, axis=0)
        return carry

    lax.fori_loop(0, tb // SUBLANES, group, 0)


def _apply(g, hn2, w):
    tk, d = hn2.shape
    tb = 16
    return pl.pallas_call(
        functools.partial(_apply_kernel, tb=tb),
        grid=(tk // tb,),
        in_specs=[pl.BlockSpec((tb, NSEL * ROW_TILE, LANES), lambda i: (i, 0, 0)),
                  pl.BlockSpec((tb, d), lambda i: (i, 0)), pl.BlockSpec((tb, NSEL), lambda i: (i, 0))],
        out_specs=pl.BlockSpec((tb, d), lambda i: (i, 0)),
        out_shape=jax.ShapeDtypeStruct((tk, d), F32),
        compiler_params=_cparams(("arbitrary",)),
        name="peer_apply",
    )(g, hn2, w)


SC_ROWS = 32
SC_IDX = 4096


def _sc_gather(table3, idx):
    n = idx.shape[0]
    info = plsc.get_sparse_core_info()
    nw = info.num_cores * info.num_subcores
    per_w = n // nw
    assert n % nw == 0 and per_w % SC_IDX == 0 and SC_IDX % (2 * SC_ROWS) == 0
    nchunk = per_w // SC_IDX
    nstep = SC_IDX // SC_ROWS
    mesh = plsc.VectorSubcoreMesh(core_axis_name="c", subcore_axis_name="s")

    @functools.partial(
        pl.kernel, mesh=mesh,
        out_type=jax.ShapeDtypeStruct((n, ROW_TILE, LANES), I32),
        scratch_types=[pltpu.VMEM((SC_IDX,), I32),
                       pltpu.VMEM((2, SC_ROWS, ROW_TILE, LANES), I32),
                       pltpu.SemaphoreType.DMA((2,)),
                       pltpu.SemaphoreType.DMA((2,))],
        compiler_params=pltpu.CompilerParams(use_tc_tiling_on_sc=True),
        name="sc_row_gather",
    )
    def k(table_hbm, idx_hbm, out_hbm, idx_v, rows_v, sem_g, sem_w):
        wid = lax.axis_index("s") * info.num_cores + lax.axis_index("c")
        base = wid * per_w

        def gather(s, slot):
            return pltpu.make_async_copy(table_hbm.at[idx_v.at[pl.ds(s * SC_ROWS, SC_ROWS)]],
                                         rows_v.at[slot], sem_g.at[slot])

        def write(off, slot):
            return pltpu.make_async_copy(rows_v.at[slot], out_hbm.at[pl.ds(off, SC_ROWS)], sem_w.at[slot])

        @pl.loop(0, nchunk)
        def _(ch):
            cbase = base + ch * SC_IDX
            pltpu.sync_copy(idx_hbm.at[pl.ds(cbase, SC_IDX)], idx_v)
            gather(0, 0).start()

            @pl.loop(0, nstep, step=2)
            def _(s):
                for u in range(2):
                    st = s + u

                    @pl.when(st + 1 < nstep)
                    def _():
                        @pl.when(st >= 1)
                        def _():
                            write(0, 1 - u).wait()
                        gather(st + 1, 1 - u).start()

                    gather(st, u).wait()
                    write(cbase + st * SC_ROWS, u).start()

            write(0, 0).wait()
            write(0, 1).wait()

    return k(table3, idx)


def _experts(e, table3, hn2, w):
    T, d = hn2.shape
    tk = min(T, 2048)
    ys = []
    for k in range(T // tk):
        sl = slice(k * tk, (k + 1) * tk)
        g = _sc_gather(table3, e[sl].reshape(tk * NSEL))
        ys.append(_apply(g.reshape(tk, NSEL * ROW_TILE, LANES), hn2[sl], w[sl]))
    return jnp.concatenate(ys, axis=0)


def _final_kernel(h1_ref, y_ref, g2_ref, fg_ref, fsh_ref, fsc_ref, o_ref):
    h = h1_ref[0] + g2_ref[0] * y_ref[0]
    o_ref[0] = _rms_mod(h, fg_ref[...], fsh_ref[0], fsc_ref[0])


def _final(h1, y, mod3, fmod3, final_g):
    nb, L, d = h1.shape
    tt = 512
    blk = pl.BlockSpec((1, tt, d), lambda b, j: (b, j, 0))
    return pl.pallas_call(
        _final_kernel,
        grid=(nb, L // tt),
        in_specs=[blk, blk, pl.BlockSpec((1, 1, d), lambda b, j: (b, 0, 5)),
                  pl.BlockSpec((1, d), lambda b, j: (0, 0)),
                  pl.BlockSpec((1, 1, d), lambda b, j: (b, 0, 0)),
                  pl.BlockSpec((1, 1, d), lambda b, j: (b, 0, 1))],
        out_specs=blk,
        out_shape=jax.ShapeDtypeStruct((nb, L, d), F32),
        compiler_params=_cparams(("arbitrary", "arbitrary")),
        name="final_norm",
    )(h1, y, mod3, final_g.reshape(1, d), fmod3, fmod3)


def kernel(x, c, ada_w, ada_b, norm_mix_g, w_in, lb_gamma_fwd, lb_gamma_bwd, hgrn_norm_g, conv_w,
           conv_b, conv_ln_g, conv_ln_b, w_out, norm_ffn_g, peer_wq, peer_keys1, peer_keys2, peer_u,
           peer_v, final_ada_w, final_ada_b, final_norm_g):
    nb, L, d = x.shape
    assert ada_w.shape[0] == 1, "single-layer trunk"
    d_hgrn = lb_gamma_fwd.shape[1]
    nh = d_hgrn // HEAD_DIM

    c_pad = jnp.pad(c, ((0, SUBLANES - nb), (0, 0)))
    mod3 = _ada(c_pad, ada_w[0], ada_b[0])[:nb].reshape(nb, 1, 6 * d)
    fmod3 = _ada(c_pad, final_ada_w, final_ada_b)[:nb].reshape(nb, 1, 2 * d)

    proj = _inproj(x, norm_mix_g[0], mod3, w_in[0].astype(BF16))
    o_f = _hgrn_dir(proj, lb_gamma_fwd, nh, reverse=False)
    o_b = _hgrn_dir(proj, lb_gamma_bwd, 2 * nh, reverse=True)
    d_conv = conv_w.shape[2]
    w_pad = jnp.pad(conv_w[0], ((0, 1), (0, 0)))
    o_c = _conv(proj, w_pad, conv_b[0], conv_ln_g[0], conv_ln_b[0], a_col=5 * d_hgrn // d_conv)

    h1, hn2, scores = _mix(o_f, o_b, proj, o_c, x, hgrn_norm_g[0], mod3, norm_ffn_g[0],
                           w_out[0].astype(BF16), peer_wq[0].astype(BF16), peer_keys1[0],
                           peer_keys2[0], g_col=4)
    e, w = _topk(scores)

    table = _pack_table(peer_u[0], peer_v[0])
    table3 = table.reshape(table.shape[0], ROW_TILE, LANES)
    T = nb * L
    y = _experts(e, table3, hn2.reshape(T, d), w)
    return _final(h1, y.reshape(nb, L, d), mod3, fmod3, final_norm_g)
```

```python
import functools

import jax
import jax.numpy as jnp
from jax import lax
from jax.experimental import pallas as pl
from jax.experimental.pallas import tpu as pltpu
from jax.experimental.pallas import tpu_sc as plsc

F32 = jnp.float32
BF16 = jnp.bfloat16
I32 = jnp.int32

EPS = 1e-6
HEAD_DIM = 128
CHUNK = 64
CONV_WIDTH = 31
CONV_PAD = CONV_WIDTH // 2
PEER_HEADS = 8
PEER_KEYS = 128
PEER_TOPK = 16
PEER_HALF = 128
LANES = 128
SUBLANES = 8
VMEM_LIMIT = 48 * 1024 * 1024


def _cparams(sem):
    return pltpu.CompilerParams(dimension_semantics=sem, vmem_limit_bytes=VMEM_LIMIT)


def _silu(x):
    return x * jax.nn.sigmoid(x)


def _split3(x):
    x1 = x.astype(BF16)
    r1 = x - x1.astype(F32)
    x2 = r1.astype(BF16)
    x3 = (r1 - x2.astype(F32)).astype(BF16)
    return x1, x2, x3


def _dot_nt(a, b):
    return lax.dot_general(a, b, (((1,), (1,)), ((), ())), preferred_element_type=F32)


def _dot_tn(a, b):
    return lax.dot_general(a, b, (((0,), (0,)), ((), ())), preferred_element_type=F32)


def _ada_kernel(c_ref, w_ref, b_ref, o_ref):
    ca = _silu(c_ref[...])
    o_ref[...] = jnp.dot(ca, w_ref[...], preferred_element_type=F32,
                         precision=lax.Precision.HIGHEST) + b_ref[...]


def _ada(c_pad, w, b):
    d, n = w.shape
    tn = 1024
    return pl.pallas_call(
        _ada_kernel,
        grid=(n // tn,),
        in_specs=[pl.BlockSpec((SUBLANES, d), lambda j: (0, 0)),
                  pl.BlockSpec((d, tn), lambda j: (0, j)),
                  pl.BlockSpec((1, tn), lambda j: (0, j))],
        out_specs=pl.BlockSpec((SUBLANES, tn), lambda j: (0, j)),
        out_shape=jax.ShapeDtypeStruct((SUBLANES, n), F32),
        compiler_params=_cparams(("arbitrary",)),
        name="ada_proj",
    )(c_pad, w, b.reshape(1, n))


def _rms_mod(x, g, sh, sc):
    y = x * lax.rsqrt(jnp.mean(x * x, axis=-1, keepdims=True) + EPS) * g
    return y * (1.0 + sc) + sh


def _inproj_kernel(x_ref, g_ref, sh_ref, sc_ref, w_ref, o_ref):
    hn = _rms_mod(x_ref[0], g_ref[...], sh_ref[0], sc_ref[0])
    o_ref[0] = jnp.dot(hn.astype(BF16), w_ref[...], preferred_element_type=F32)


def _inproj(x, g, mod3, w_bf16):
    nb, L, d = x.shape
    n = w_bf16.shape[1]
    tt = 512
    return pl.pallas_call(
        _inproj_kernel,
        grid=(nb, L // tt),
        in_specs=[pl.BlockSpec((1, tt, d), lambda b, j: (b, j, 0)),
                  pl.BlockSpec((1, d), lambda b, j: (0, 0)),
                  pl.BlockSpec((1, 1, d), lambda b, j: (b, 0, 0)),
                  pl.BlockSpec((1, 1, d), lambda b, j: (b, 0, 1)),
                  pl.BlockSpec((d, n), lambda b, j: (0, 0))],
        out_specs=pl.BlockSpec((1, tt, n), lambda b, j: (b, j, 0)),
        out_shape=jax.ShapeDtypeStruct((nb, L, n), F32),
        compiler_params=_cparams(("arbitrary", "arbitrary")),
        name="in_proj",
    )(x, g.reshape(1, d), mod3, mod3, w_bf16)


def _hgrn_kernel(q_ref, f_ref, i_ref, lbg_ref, o_ref, st_ref, *, reverse, lblk):
    @pl.when(pl.program_id(2) == 0)
    def _():
        st_ref[...] = jnp.zeros_like(st_ref)

    lbg = lbg_ref[...]
    ex = jnp.exp(lbg - jnp.max(lbg, axis=0, keepdims=True))
    lb = ex[0:1] / jnp.sum(ex, axis=0, keepdims=True)

    row = lax.broadcasted_iota(I32, (CHUNK, CHUNK), 0)
    col = lax.broadcasted_iota(I32, (CHUNK, CHUNK), 1)
    keep = (col >= row) if reverse else (col <= row)
    tri = keep.astype(BF16)
    i_mid = CHUNK // 2 if reverse else CHUNK // 2 - 1
    i_end = 0 if reverse else CHUNK - 1
    nch = lblk // CHUNK

    def body(ci, st):
        c = (nch - 1 - ci) if reverse else ci
        r0 = pl.multiple_of(c * CHUNK, CHUNK)
        z = f_ref[0, pl.ds(r0, CHUNK), :]
        q = _silu(q_ref[0, pl.ds(r0, CHUNK), :])
        v = i_ref[0, pl.ds(r0, CHUNK), :]
        logf = jnp.log(lb + (1.0 - lb) * jax.nn.sigmoid(z))
        k = (1.0 - lb) * jax.nn.sigmoid(-z)
        l1, l2, l3 = _split3(logf)
        cum = (jnp.dot(tri, l1, preferred_element_type=F32)
               + jnp.dot(tri, l2, preferred_element_type=F32)
               + jnp.dot(tri, l3, preferred_element_type=F32))
        c_mid = cum[i_mid:i_mid + 1]
        c_end = cum[i_end:i_end + 1]
        qr = (q * jnp.exp(cum - c_mid)).astype(BF16)
        kr = (k * jnp.exp(c_mid - cum)).astype(BF16)
        scores = jnp.where(keep, _dot_nt(qr, kr), 0.0)
        vb = v.astype(BF16)
        o_intra = jnp.dot(scores.astype(BF16), vb, preferred_element_type=F32)
        o_inter = _dot_nt((q * jnp.exp(cum)).astype(BF16), st.astype(BF16))
        o_ref[0, pl.ds(r0, CHUNK), :] = o_intra + o_inter
        ku = (k * jnp.exp(c_end - cum)).astype(BF16)
        return st * jnp.exp(c_end) + _dot_tn(vb, ku)

    st_ref[...] = lax.fori_loop(0, nch, body, st_ref[...])


def _hgrn_dir(proj, lb_gamma, f_col, reverse):
    nb, L, _ = proj.shape
    nh = lb_gamma.shape[1] // HEAD_DIM
    lblk = min(1024, L)
    nblk = L // lblk

    def blk(j):
        return (nblk - 1 - j) if reverse else j

    def col_spec(c0):
        return pl.BlockSpec((1, lblk, HEAD_DIM), lambda b, h, j: (b, blk(j), c0 + h))

    return pl.pallas_call(
        functools.partial(_hgrn_kernel, reverse=reverse, lblk=lblk),
        grid=(nb, nh, nblk),
        in_specs=[col_spec(0), col_spec(f_col), col_spec(3 * nh),
                  pl.BlockSpec((lb_gamma.shape[0], HEAD_DIM), lambda b, h, j: (0, h))],
        out_specs=pl.BlockSpec((1, lblk, HEAD_DIM), lambda b, h, j: (b, blk(j), h)),
        out_shape=jax.ShapeDtypeStruct((nb, L, nh * HEAD_DIM), F32),
        scratch_shapes=[pltpu.VMEM((HEAD_DIM, HEAD_DIM), F32)],
        compiler_params=_cparams(("arbitrary", "arbitrary", "arbitrary")),
        name="hgrn_bwd" if reverse else "hgrn_fwd",
    )(proj, proj, proj, lb_gamma)


CONV_HALO = 16
CONV_ROWS = 64


def _conv_kernel(a_ref, g_ref, ap_ref, gp_ref, an_ref, gn_ref, w_ref, b_ref, lg_ref, lb_ref,
                 o_ref, hbuf, *, tl):
    j = pl.program_id(1)
    last = pl.num_programs(1) - 1
    hbuf[CONV_HALO:CONV_HALO + tl, :] = a_ref[0] * jax.nn.sigmoid(g_ref[0])
    hp = ap_ref[0] * jax.nn.sigmoid(gp_ref[0])
    hbuf[0:CONV_HALO, :] = jnp.where(j > 0, hp, 0.0)
    hn = an_ref[0] * jax.nn.sigmoid(gn_ref[0])
    hbuf[CONV_HALO + tl:2 * CONV_HALO + tl, :] = jnp.where(j < last, hn, 0.0)
    off = CONV_HALO - CONV_PAD
    for r in range(tl // CONV_ROWS):
        acc = jnp.zeros((CONV_ROWS, a_ref.shape[2]), F32)
        for k in range(CONV_WIDTH):
            s = r * CONV_ROWS + k + off
            acc = acc + w_ref[k:k + 1, :] * hbuf[s:s + CONV_ROWS, :]
        acc = acc + b_ref[...]
        mu = jnp.mean(acc, axis=-1, keepdims=True)
        cen = acc - mu
        var = jnp.mean(cen * cen, axis=-1, keepdims=True)
        y = cen * lax.rsqrt(var + EPS) * lg_ref[...] + lb_ref[...]
        o_ref[0, r * CONV_ROWS:(r + 1) * CONV_ROWS, :] = _silu(y)


def _conv(proj, w_pad, bias, ln_g, ln_b, a_col):
    nb, L, _ = proj.shape
    dc = w_pad.shape[1]
    tl = 256
    hb = tl // CONV_HALO
    nhalo = L // CONV_HALO

    def cur(c):
        return pl.BlockSpec((1, tl, dc), lambda b, j: (b, j, c))

    def prev(c):
        return pl.BlockSpec((1, CONV_HALO, dc), lambda b, j: (b, jnp.maximum(j * hb - 1, 0), c))

    def nxt(c):
        return pl.BlockSpec((1, CONV_HALO, dc),
                            lambda b, j: (b, jnp.minimum((j + 1) * hb, nhalo - 1), c))

    vec = pl.BlockSpec((1, dc), lambda b, j: (0, 0))
    return pl.pallas_call(
        functools.partial(_conv_kernel, tl=tl),
        grid=(nb, L // tl),
        in_specs=[cur(a_col), cur(a_col + 1), prev(a_col), prev(a_col + 1), nxt(a_col),
                  nxt(a_col + 1), pl.BlockSpec(w_pad.shape, lambda b, j: (0, 0)), vec, vec, vec],
        out_specs=pl.BlockSpec((1, tl, dc), lambda b, j: (b, j, 0)),
        out_shape=jax.ShapeDtypeStruct((nb, L, dc), F32),
        scratch_shapes=[pltpu.VMEM((tl + 2 * CONV_HALO, dc), F32)],
        compiler_params=_cparams(("arbitrary", "arbitrary")),
        name="conv_group",
    )(proj, proj, proj, proj, proj, proj, w_pad, bias.reshape(1, dc), ln_g.reshape(1, dc),
      ln_b.reshape(1, dc))


def _mix_kernel(of_ref, ob_ref, gr_ref, oc_ref, x_ref, hg_ref, g1_ref, sh2_ref, sc2_ref, ng_ref,
                wo_ref, wq_ref, k1_ref, k2_ref, h1_ref, hn2_ref, s_ref):
    o = of_ref[0] + ob_ref[0]
    parts = []
    for hh in range(o.shape[1] // HEAD_DIM):
        oh = o[:, hh * HEAD_DIM:(hh + 1) * HEAD_DIM]
        parts.append(oh * lax.rsqrt(jnp.mean(oh * oh, axis=-1, keepdims=True) + EPS))
    on = jnp.concatenate(parts, axis=-1) * hg_ref[...] * _silu(gr_ref[0])
    cat = jnp.concatenate([on, oc_ref[0]], axis=-1).astype(BF16)
    mix = jnp.dot(cat, wo_ref[...], preferred_element_type=F32)
    h1 = x_ref[0] + g1_ref[0] * mix
    h1_ref[0] = h1
    hn2 = _rms_mod(h1, ng_ref[...], sh2_ref[0], sc2_ref[0])
    hn2_ref[0] = hn2
    q = jnp.dot(hn2.astype(BF16), wq_ref[...], preferred_element_type=F32)
    k1h, k1l, _ = _split3(k1_ref[...])
    k2h, k2l, _ = _split3(k2_ref[...])
    for hd in range(PEER_HEADS):
        for half, (kh, kl) in enumerate(((k1h, k1l), (k2h, k2l))):
            c0 = hd * 2 * PEER_HALF + half * PEER_HALF
            qh, ql, _ = _split3(q[:, c0:c0 + PEER_HALF])
            s_ref[2 * hd + half] = _dot_nt(kh, qh) + _dot_nt(kh, ql) + _dot_nt(kl, qh)


def _mix(o_f, o_b, proj, o_c, x, hgrn_g, mod3, norm_g, w_out_bf16, wq_bf16, keys1, keys2, g_col):
    nb, L, d = x.shape
    dh = o_f.shape[2]
    dq = wq_bf16.shape[1]
    tt = 256
    nj = L // tt

    def half(c=0):
        return pl.BlockSpec((1, tt, dh), lambda b, j: (b, j, c))

    def full():
        return pl.BlockSpec((1, tt, d), lambda b, j: (b, j, 0))

    def modc(c):
        return pl.BlockSpec((1, 1, d), lambda b, j: (b, 0, c))

    def const(shape):
        return pl.BlockSpec(shape, lambda b, j: (0,) * len(shape))

    return pl.pallas_call(
        _mix_kernel,
        grid=(nb, nj),
        in_specs=[half(), half(), half(g_col), half(), full(), const((1, dh)),
                  modc(2), modc(3), modc(4), const((1, d)), const((d, d)), const((d, dq)),
                  const(keys1.shape), const(keys2.shape)],
        out_specs=[full(), full(),
                   pl.BlockSpec((2 * PEER_HEADS, PEER_KEYS, tt), lambda b, j: (0, 0, b * nj + j))],
        out_shape=[jax.ShapeDtypeStruct((nb, L, d), F32), jax.ShapeDtypeStruct((nb, L, d), F32),
                   jax.ShapeDtypeStruct((2 * PEER_HEADS, PEER_KEYS, nb * L), F32)],
        compiler_params=_cparams(("arbitrary", "arbitrary")),
        name="mix_scores",
    )(o_f, o_b, proj, o_c, x, hgrn_g.reshape(1, dh), mod3, mod3, mod3, norm_g.reshape(1, d),
      w_out_bf16, wq_bf16, keys1, keys2)


def _top16(s):
    n = s.shape[0]
    iota = lax.broadcasted_iota(I32, s.shape, 0)
    vals, idxs = [], []
    for _ in range(PEER_TOPK):
        m = jnp.max(s, axis=0, keepdims=True)
        idx = jnp.min(jnp.where(s == m, iota, n), axis=0, keepdims=True)
        vals.append(m)
        idxs.append(idx)
        s = jnp.where(iota == idx, -jnp.inf, s)
    return jnp.concatenate(vals, axis=0), jnp.concatenate(idxs, axis=0)


def _take16(table, sel):
    out = jnp.zeros_like(table)
    for a in range(PEER_TOPK):
        out = jnp.where(sel == a, table[a:a + 1], out)
    return out


def _topk_kernel(s_ref, e_ref, w_ref):
    es, ws = [], []
    for hd in range(PEER_HEADS):
        v1, i1 = _top16(s_ref[2 * hd])
        v2, i2 = _top16(s_ref[2 * hd + 1])
        cand = jnp.concatenate([v1[a:a + 1] + v2 for a in range(PEER_TOPK)], axis=0)
        sc, ci = _top16(cand)
        e = _take16(i1, ci // PEER_TOPK) * PEER_KEYS + _take16(i2, ci % PEER_TOPK)
        p = jnp.exp(sc - sc[0:1])
        es.append(e)
        ws.append(p / jnp.sum(p, axis=0, keepdims=True))
    e_ref[...] = jnp.concatenate(es, axis=0).T
    w_ref[...] = jnp.concatenate(ws, axis=0).T


def _topk(scores):
    npair, nk, T = scores.shape
    nsel = PEER_HEADS * PEER_TOPK
    tt = 256
    return pl.pallas_call(
        _topk_kernel,
        grid=(T // tt,),
        in_specs=[pl.BlockSpec((npair, nk, tt), lambda i: (0, 0, i))],
        out_specs=[pl.BlockSpec((tt, nsel), lambda i: (i, 0)),
                   pl.BlockSpec((tt, nsel), lambda i: (i, 0))],
        out_shape=[jax.ShapeDtypeStruct((T, nsel), I32), jax.ShapeDtypeStruct((T, nsel), F32)],
        compiler_params=_cparams(("arbitrary",)),
        name="peer_topk",
    )(scores)


WORDS = 4


def _pack_words(x):
    bits = pltpu.bitcast(x.astype(BF16).astype(F32), I32)
    out = []
    for c in range(WORDS):
        lo = bits[:, (2 * c) * LANES:(2 * c + 1) * LANES]
        hi = bits[:, (2 * c + 1) * LANES:(2 * c + 2) * LANES]
        out.append(lax.shift_right_logical(lo, 16) | (hi & jnp.int32(-65536)))
    return jnp.concatenate(out, axis=-1)


def _pack_kernel(x_ref, o_ref):
    o_ref[...] = _pack_words(x_ref[...])


def _pack_table(x):
    ne, d = x.shape
    te = 512
    packed = pl.pallas_call(
        _pack_kernel,
        grid=(ne // te,),
        in_specs=[pl.BlockSpec((te, d), lambda i: (i, 0))],
        out_specs=pl.BlockSpec((te, d // 2), lambda i: (i, 0)),
        out_shape=jax.ShapeDtypeStruct((ne, d // 2), I32),
        compiler_params=_cparams(("arbitrary",)),
        name="pack_table",
    )(x)
    return packed.reshape(ne, WORDS, LANES)


NSEL = PEER_HEADS * PEER_TOPK
ROW_TILE = 2 * WORDS


SC_LANES = 16
SC_ROWS = 64
SC_TOK = 16
SC_GROUP = 8
VS_GROUP = 2
PART_ROWS = NSEL * SC_LANES // LANES


def _sc_unpack(w):
    lo = plsc.bitcast(lax.shift_left(w, jnp.int32(16)), F32)
    hi = plsc.bitcast(w & jnp.int32(-65536), F32)
    return lo, hi


def _sc_mesh():
    return plsc.VectorSubcoreMesh(core_axis_name="c", subcore_axis_name="s")


def _sc_params():
    return pltpu.CompilerParams(use_tc_tiling_on_sc=True, needs_layout_passes=False)


def _sc_udot(table3, idx, hn3):
    T = hn3.shape[0]
    info = plsc.get_sparse_core_info()
    nw = info.num_cores * info.num_subcores
    tpw = T // nw
    assert T % nw == 0 and tpw % SC_TOK == 0 and NSEL % SC_ROWS == 0
    nchunk = tpw // SC_TOK
    qsteps = NSEL // SC_ROWS
    nstep = SC_TOK * qsteps

    @functools.partial(
        pl.kernel, mesh=_sc_mesh(),
        out_type=jax.ShapeDtypeStruct((T, PART_ROWS, LANES), F32),
        scratch_types=[pltpu.VMEM((SC_TOK * NSEL,), I32),
                       pltpu.VMEM((2, SC_ROWS, WORDS, LANES), I32),
                       pltpu.VMEM((SC_TOK, ROW_TILE, LANES), F32),
                       pltpu.VMEM((SC_TOK, PART_ROWS, LANES), F32),
                       pltpu.SemaphoreType.DMA((2,))],
        compiler_params=_sc_params(),
        name="sc_udot",
    )
    def k(table_hbm, idx_hbm, hn_hbm, out_hbm, idx_v, rows_v, h_v, p_v, sem_g):
        wid = lax.axis_index("s") * info.num_cores + lax.axis_index("c")
        tbase = wid * tpw

        def gather(st, slot):
            return pltpu.make_async_copy(table_hbm.at[idx_v.at[pl.ds(st * SC_ROWS, SC_ROWS)]],
                                         rows_v.at[slot], sem_g.at[slot])

        def compute(tk, q, slot):
            @pl.loop(0, SC_ROWS // SC_GROUP)
            def _(g):
                acc = [jnp.zeros((SC_LANES,), F32) for _ in range(SC_GROUP)]
                for c in range(WORDS):
                    for lv in range(LANES // SC_LANES):
                        ls = pl.ds(lv * SC_LANES, SC_LANES)
                        hl = h_v[tk, 2 * c, ls]
                        hh = h_v[tk, 2 * c + 1, ls]
                        for i in range(SC_GROUP):
                            lo, hi = _sc_unpack(rows_v[slot, g * SC_GROUP + i, c, ls])
                            acc[i] = acc[i] + lo * hl + hi * hh
                for i in range(SC_GROUP):
                    p_v[tk, q * (SC_ROWS // SC_GROUP) + g, pl.ds(i * SC_LANES, SC_LANES)] = acc[i]

        @pl.loop(0, nchunk)
        def _(ch):
            t0 = tbase + ch * SC_TOK
            pltpu.sync_copy(idx_hbm.at[pl.ds(t0 * NSEL, SC_TOK * NSEL)], idx_v)
            pltpu.sync_copy(hn_hbm.at[pl.ds(t0, SC_TOK)], h_v)
            gather(0, 0).start()

            @pl.loop(0, SC_TOK)
            def _(tk):
                for q in range(qsteps):
                    st = tk * qsteps + q
                    slot = q % 2

                    @pl.when(st + 1 < nstep)
                    def _():
                        gather(st + 1, 1 - slot).start()

                    gather(st, slot).wait()
                    compute(tk, q, slot)

            pltpu.sync_copy(p_v, out_hbm.at[pl.ds(t0, SC_TOK)])

    return k(table3, idx, hn3)


def _sc_vsum(table3, idx, coefx):
    T = coefx.shape[0]
    info = plsc.get_sparse_core_info()
    nw = info.num_cores * info.num_subcores
    tpw = T // nw
    assert T % nw == 0 and tpw % SC_TOK == 0 and NSEL % SC_ROWS == 0 and SC_GROUP % VS_GROUP == 0
    nchunk = tpw // SC_TOK
    qsteps = NSEL // SC_ROWS
    nstep = SC_TOK * qsteps
    nlv = LANES // SC_LANES
    sub = SC_GROUP // VS_GROUP

    @functools.partial(
        pl.kernel, mesh=_sc_mesh(),
        out_type=jax.ShapeDtypeStruct((T, ROW_TILE, LANES), F32),
        scratch_types=[pltpu.VMEM((SC_TOK * NSEL,), I32),
                       pltpu.VMEM((2, SC_ROWS, WORDS, LANES), I32),
                       pltpu.VMEM((SC_TOK, PART_ROWS, LANES), F32),
                       pltpu.VMEM((SC_TOK, ROW_TILE, LANES), F32),
                       pltpu.SemaphoreType.DMA((2,))],
        compiler_params=_sc_params(),
        name="sc_vsum",
    )
    def k(table_hbm, idx_hbm, coef_hbm, out_hbm, idx_v, rows_v, c_v, y_v, sem_g):
        wid = lax.axis_index("s") * info.num_cores + lax.axis_index("c")
        tbase = wid * tpw

        def gather(st, slot):
            return pltpu.make_async_copy(table_hbm.at[idx_v.at[pl.ds(st * SC_ROWS, SC_ROWS)]],
                                         rows_v.at[slot], sem_g.at[slot])

        def compute(tk, q, slot):
            for c in range(WORDS):
                if q == 0:
                    init = tuple(jnp.zeros((SC_LANES,), F32) for _ in range(2 * nlv))
                else:
                    init = tuple(y_v[tk, 2 * c + p, pl.ds(lv * SC_LANES, SC_LANES)]
                                 for p in range(2) for lv in range(nlv))

                def body(g, acc):
                    acc = list(acc)
                    for i in range(VS_GROUP):
                        cv = c_v[tk, q * (SC_ROWS // SC_GROUP) + g // sub,
                                 pl.ds(((g % sub) * VS_GROUP + i) * SC_LANES, SC_LANES)]
                        for lv in range(nlv):
                            lo, hi = _sc_unpack(rows_v[slot, g * VS_GROUP + i, c,
                                                       pl.ds(lv * SC_LANES, SC_LANES)])
                            acc[lv] = acc[lv] + cv * lo
                            acc[nlv + lv] = acc[nlv + lv] + cv * hi
                    return tuple(acc)

                acc = lax.fori_loop(0, SC_ROWS // VS_GROUP, body, init)
                for p in range(2):
                    for lv in range(nlv):
                        y_v[tk, 2 * c + p, pl.ds(lv * SC_LANES, SC_LANES)] = acc[p * nlv + lv]

        @pl.loop(0, nchunk)
        def _(ch):
            t0 = tbase + ch * SC_TOK
            pltpu.sync_copy(idx_hbm.at[pl.ds(t0 * NSEL, SC_TOK * NSEL)], idx_v)
            pltpu.sync_copy(coef_hbm.at[pl.ds(t0, SC_TOK)], c_v)
            gather(0, 0).start()

            @pl.loop(0, SC_TOK)
            def _(tk):
                for q in range(qsteps):
                    st = tk * qsteps + q
                    slot = q % 2

                    @pl.when(st + 1 < nstep)
                    def _():
                        gather(st + 1, 1 - slot).start()

                    gather(st, slot).wait()
                    compute(tk, q, slot)

            pltpu.sync_copy(y_v, out_hbm.at[pl.ds(t0, SC_TOK)])

    return k(table3, idx, coefx)


def _coef_kernel(p_ref, w_ref, cx_ref, *, tt):
    row = lax.broadcasted_iota(I32, (LANES, NSEL), 0)
    col = lax.broadcasted_iota(I32, (LANES, NSEL), 1)
    per_row = LANES // SC_LANES
    act = jnp.zeros((tt, NSEL), F32)
    for s in range(PART_ROWS):
        fold = (col == per_row * s + row // SC_LANES).astype(BF16)
        p1, p2, p3 = _split3(p_ref[pl.ds(s, tt, stride=PART_ROWS), :])
        act = act + (jnp.dot(p1, fold, preferred_element_type=F32) + jnp.dot(p2, fold, preferred_element_type=F32)
                     + jnp.dot(p3, fold, preferred_element_type=F32))
    coef = w_ref[...] * (0.5 * act * (1.0 + lax.erf(act * (2.0 ** -0.5))))
    c1, c2, c3 = _split3(coef)
    for s in range(PART_ROWS):
        spread = (row == per_row * s + col // SC_LANES).astype(BF16)
        cx_ref[pl.ds(s, tt, stride=PART_ROWS), :] = (
            jnp.dot(c1, spread, preferred_element_type=F32) + jnp.dot(c2, spread, preferred_element_type=F32)
            + jnp.dot(c3, spread, preferred_element_type=F32))


def _coef(p2d, w):
    T = w.shape[0]
    tt = 256
    return pl.pallas_call(
        functools.partial(_coef_kernel, tt=tt),
        grid=(T // tt,),
        in_specs=[pl.BlockSpec((tt * PART_ROWS, LANES), lambda i: (i, 0)),
                  pl.BlockSpec((tt, NSEL), lambda i: (i, 0))],
        out_specs=pl.BlockSpec((tt * PART_ROWS, LANES), lambda i: (i, 0)),
        out_shape=jax.ShapeDtypeStruct((T * PART_ROWS, LANES), F32),
        compiler_params=_cparams(("arbitrary",)),
        name="peer_coef",
    )(p2d, w)


def _experts(e, u_rows, v_rows, hn2, w):
    T = hn2.shape[0]
    idx = e.reshape(T * NSEL)
    p = _sc_udot(u_rows, idx, hn2.reshape(T, ROW_TILE, LANES))
    cx = _coef(p.reshape(T * PART_ROWS, LANES), w)
    return _sc_vsum(v_rows, idx, cx.reshape(T, PART_ROWS, LANES)).reshape(hn2.shape)


def _final_kernel(h1_ref, y_ref, g2_ref, fg_ref, fsh_ref, fsc_ref, o_ref):
    h = h1_ref[0] + g2_ref[0] * y_ref[0]
    o_ref[0] = _rms_mod(h, fg_ref[...], fsh_ref[0], fsc_ref[0])


def _final(h1, y, mod3, fmod3, final_g):
    nb, L, d = h1.shape
    tt = 512
    blk = pl.BlockSpec((1, tt, d), lambda b, j: (b, j, 0))
    return pl.pallas_call(
        _final_kernel,
        grid=(nb, L // tt),
        in_specs=[blk, blk, pl.BlockSpec((1, 1, d), lambda b, j: (b, 0, 5)),
                  pl.BlockSpec((1, d), lambda b, j: (0, 0)),
                  pl.BlockSpec((1, 1, d), lambda b, j: (b, 0, 0)),
                  pl.BlockSpec((1, 1, d), lambda b, j: (b, 0, 1))],
        out_specs=blk,
        out_shape=jax.ShapeDtypeStruct((nb, L, d), F32),
        compiler_params=_cparams(("arbitrary", "arbitrary")),
        name="final_norm",
    )(h1, y, mod3, final_g.reshape(1, d), fmod3, fmod3)


def kernel(x, c, ada_w, ada_b, norm_mix_g, w_in, lb_gamma_fwd, lb_gamma_bwd, hgrn_norm_g, conv_w,
           conv_b, conv_ln_g, conv_ln_b, w_out, norm_ffn_g, peer_wq, peer_keys1, peer_keys2, peer_u,
           peer_v, final_ada_w, final_ada_b, final_norm_g):
    nb, L, d = x.shape
    assert ada_w.shape[0] == 1, "single-layer trunk"
    d_hgrn = lb_gamma_fwd.shape[1]
    nh = d_hgrn // HEAD_DIM

    c_pad = jnp.pad(c, ((0, SUBLANES - nb), (0, 0)))
    mod3 = _ada(c_pad, ada_w[0], ada_b[0])[:nb].reshape(nb, 1, 6 * d)
    fmod3 = _ada(c_pad, final_ada_w, final_ada_b)[:nb].reshape(nb, 1, 2 * d)

    proj = _inproj(x, norm_mix_g[0], mod3, w_in[0].astype(BF16))
    o_f = _hgrn_dir(proj, lb_gamma_fwd, nh, reverse=False)
    o_b = _hgrn_dir(proj, lb_gamma_bwd, 2 * nh, reverse=True)
    d_conv = conv_w.shape[2]
    w_pad = jnp.pad(conv_w[0], ((0, 1), (0, 0)))
    o_c = _conv(proj, w_pad, conv_b[0], conv_ln_g[0], conv_ln_b[0], a_col=5 * d_hgrn // d_conv)

    h1, hn2, scores = _mix(o_f, o_b, proj, o_c, x, hgrn_norm_g[0], mod3, norm_ffn_g[0],
                           w_out[0].astype(BF16), peer_wq[0].astype(BF16), peer_keys1[0],
                           peer_keys2[0], g_col=4)
    e, w = _topk(scores)

    T = nb * L
    y = _experts(e, _pack_table(peer_u[0]), _pack_table(peer_v[0]), hn2.reshape(T, d), w)
    return _final(h1, y.reshape(nb, L, d), mod3, fmod3, final_norm_g)
```

```python
import functools

import jax
import jax.numpy as jnp
from jax import lax
from jax.experimental import pallas as pl
from jax.experimental.pallas import tpu as pltpu
from jax.experimental.pallas import tpu_sc as plsc

F32 = jnp.float32
BF16 = jnp.bfloat16
I32 = jnp.int32

EPS = 1e-6
HEAD_DIM = 128
CHUNK = 64
CONV_WIDTH = 31
CONV_PAD = CONV_WIDTH // 2
PEER_HEADS = 8
PEER_KEYS = 128
PEER_TOPK = 16
PEER_HALF = 128
LANES = 128
SUBLANES = 8
VMEM_LIMIT = 48 * 1024 * 1024


def _cparams(sem):
    return pltpu.CompilerParams(dimension_semantics=sem, vmem_limit_bytes=VMEM_LIMIT)


def _silu(x):
    return x * jax.nn.sigmoid(x)


def _split3(x):
    x1 = x.astype(BF16)
    r1 = x - x1.astype(F32)
    x2 = r1.astype(BF16)
    x3 = (r1 - x2.astype(F32)).astype(BF16)
    return x1, x2, x3


def _dot_nt(a, b):
    return lax.dot_general(a, b, (((1,), (1,)), ((), ())), preferred_element_type=F32)


def _dot_tn(a, b):
    return lax.dot_general(a, b, (((0,), (0,)), ((), ())), preferred_element_type=F32)


def _ada_kernel(c_ref, w_ref, b_ref, o_ref):
    ca = _silu(c_ref[...])
    o_ref[...] = jnp.dot(ca, w_ref[...], preferred_element_type=F32,
                         precision=lax.Precision.HIGHEST) + b_ref[...]


def _ada(c_pad, w, b):
    d, n = w.shape
    tn = 1024
    return pl.pallas_call(
        _ada_kernel,
        grid=(n // tn,),
        in_specs=[pl.BlockSpec((SUBLANES, d), lambda j: (0, 0)),
                  pl.BlockSpec((d, tn), lambda j: (0, j)),
                  pl.BlockSpec((1, tn), lambda j: (0, j))],
        out_specs=pl.BlockSpec((SUBLANES, tn), lambda j: (0, j)),
        out_shape=jax.ShapeDtypeStruct((SUBLANES, n), F32),
        compiler_params=_cparams(("arbitrary",)),
        name="ada_proj",
    )(c_pad, w, b.reshape(1, n))


def _rms_mod(x, g, sh, sc):
    y = x * lax.rsqrt(jnp.mean(x * x, axis=-1, keepdims=True) + EPS) * g
    return y * (1.0 + sc) + sh


def _inproj_kernel(x_ref, g_ref, sh_ref, sc_ref, w_ref, o_ref):
    hn = _rms_mod(x_ref[0], g_ref[...], sh_ref[0], sc_ref[0])
    o_ref[0] = jnp.dot(hn.astype(BF16), w_ref[...], preferred_element_type=F32)


def _inproj(x, g, mod3, w_bf16):
    nb, L, d = x.shape
    n = w_bf16.shape[1]
    tt = 512
    return pl.pallas_call(
        _inproj_kernel,
        grid=(nb, L // tt),
        in_specs=[pl.BlockSpec((1, tt, d), lambda b, j: (b, j, 0)),
                  pl.BlockSpec((1, d), lambda b, j: (0, 0)),
                  pl.BlockSpec((1, 1, d), lambda b, j: (b, 0, 0)),
                  pl.BlockSpec((1, 1, d), lambda b, j: (b, 0, 1)),
                  pl.BlockSpec((d, n), lambda b, j: (0, 0))],
        out_specs=pl.BlockSpec((1, tt, n), lambda b, j: (b, j, 0)),
        out_shape=jax.ShapeDtypeStruct((nb, L, n), F32),
        compiler_params=_cparams(("arbitrary", "arbitrary")),
        name="in_proj",
    )(x, g.reshape(1, d), mod3, mod3, w_bf16)


def _hgrn_kernel(q_ref, f_ref, i_ref, lbg_ref, o_ref, st_ref, *, reverse, lblk):
    @pl.when(pl.program_id(2) == 0)
    def _():
        st_ref[...] = jnp.zeros_like(st_ref)

    lbg = lbg_ref[...]
    ex = jnp.exp(lbg - jnp.max(lbg, axis=0, keepdims=True))
    lb = ex[0:1] / jnp.sum(ex, axis=0, keepdims=True)

    row = lax.broadcasted_iota(I32, (CHUNK, CHUNK), 0)
    col = lax.broadcasted_iota(I32, (CHUNK, CHUNK), 1)
    keep = (col >= row) if reverse else (col <= row)
    tri = keep.astype(BF16)
    i_mid = CHUNK // 2 if reverse else CHUNK // 2 - 1
    i_end = 0 if reverse else CHUNK - 1
    nch = lblk // CHUNK

    def body(ci, st):
        c = (nch - 1 - ci) if reverse else ci
        r0 = pl.multiple_of(c * CHUNK, CHUNK)
        z = f_ref[0, pl.ds(r0, CHUNK), :]
        q = _silu(q_ref[0, pl.ds(r0, CHUNK), :])
        v = i_ref[0, pl.ds(r0, CHUNK), :]
        logf = jnp.log(lb + (1.0 - lb) * jax.nn.sigmoid(z))
        k = (1.0 - lb) * jax.nn.sigmoid(-z)
        l1, l2, l3 = _split3(logf)
        cum = (jnp.dot(tri, l1, preferred_element_type=F32)
               + jnp.dot(tri, l2, preferred_element_type=F32)
               + jnp.dot(tri, l3, preferred_element_type=F32))
        c_mid = cum[i_mid:i_mid + 1]
        c_end = cum[i_end:i_end + 1]
        qr = (q * jnp.exp(cum - c_mid)).astype(BF16)
        kr = (k * jnp.exp(c_mid - cum)).astype(BF16)
        scores = jnp.where(keep, _dot_nt(qr, kr), 0.0)
        vb = v.astype(BF16)
        o_intra = jnp.dot(scores.astype(BF16), vb, preferred_element_type=F32)
        o_inter = _dot_nt((q * jnp.exp(cum)).astype(BF16), st.astype(BF16))
        o_ref[0, pl.ds(r0, CHUNK), :] = o_intra + o_inter
        ku = (k * jnp.exp(c_end - cum)).astype(BF16)
        return st * jnp.exp(c_end) + _dot_tn(vb, ku)

    st_ref[...] = lax.fori_loop(0, nch, body, st_ref[...])


def _hgrn_dir(proj, lb_gamma, f_col, reverse):
    nb, L, _ = proj.shape
    nh = lb_gamma.shape[1] // HEAD_DIM
    lblk = min(1024, L)
    nblk = L // lblk

    def blk(j):
        return (nblk - 1 - j) if reverse else j

    def col_spec(c0):
        return pl.BlockSpec((1, lblk, HEAD_DIM), lambda b, h, j: (b, blk(j), c0 + h))

    return pl.pallas_call(
        functools.partial(_hgrn_kernel, reverse=reverse, lblk=lblk),
        grid=(nb, nh, nblk),
        in_specs=[col_spec(0), col_spec(f_col), col_spec(3 * nh),
                  pl.BlockSpec((lb_gamma.shape[0], HEAD_DIM), lambda b, h, j: (0, h))],
        out_specs=pl.BlockSpec((1, lblk, HEAD_DIM), lambda b, h, j: (b, blk(j), h)),
        out_shape=jax.ShapeDtypeStruct((nb, L, nh * HEAD_DIM), F32),
        scratch_shapes=[pltpu.VMEM((HEAD_DIM, HEAD_DIM), F32)],
        compiler_params=_cparams(("arbitrary", "arbitrary", "arbitrary")),
        name="hgrn_bwd" if reverse else "hgrn_fwd",
    )(proj, proj, proj, lb_gamma)


CONV_HALO = 16
CONV_ROWS = 64


def _conv_kernel(a_ref, g_ref, ap_ref, gp_ref, an_ref, gn_ref, w_ref, b_ref, lg_ref, lb_ref,
                 o_ref, hbuf, *, tl):
    j = pl.program_id(1)
    last = pl.num_programs(1) - 1
    hbuf[CONV_HALO:CONV_HALO + tl, :] = a_ref[0] * jax.nn.sigmoid(g_ref[0])
    hp = ap_ref[0] * jax.nn.sigmoid(gp_ref[0])
    hbuf[0:CONV_HALO, :] = jnp.where(j > 0, hp, 0.0)
    hn = an_ref[0] * jax.nn.sigmoid(gn_ref[0])
    hbuf[CONV_HALO + tl:2 * CONV_HALO + tl, :] = jnp.where(j < last, hn, 0.0)
    off = CONV_HALO - CONV_PAD
    for r in range(tl // CONV_ROWS):
        acc = jnp.zeros((CONV_ROWS, a_ref.shape[2]), F32)
        for k in range(CONV_WIDTH):
            s = r * CONV_ROWS + k + off
            acc = acc + w_ref[k:k + 1, :] * hbuf[s:s + CONV_ROWS, :]
        acc = acc + b_ref[...]
        mu = jnp.mean(acc, axis=-1, keepdims=True)
        cen = acc - mu
        var = jnp.mean(cen * cen, axis=-1, keepdims=True)
        y = cen * lax.rsqrt(var + EPS) * lg_ref[...] + lb_ref[...]
        o_ref[0, r * CONV_ROWS:(r + 1) * CONV_ROWS, :] = _silu(y)


def _conv(proj, w_pad, bias, ln_g, ln_b, a_col):
    nb, L, _ = proj.shape
    dc = w_pad.shape[1]
    tl = 256
    hb = tl // CONV_HALO
    nhalo = L // CONV_HALO

    def cur(c):
        return pl.BlockSpec((1, tl, dc), lambda b, j: (b, j, c))

    def prev(c):
        return pl.BlockSpec((1, CONV_HALO, dc), lambda b, j: (b, jnp.maximum(j * hb - 1, 0), c))

    def nxt(c):
        return pl.BlockSpec((1, CONV_HALO, dc),
                            lambda b, j: (b, jnp.minimum((j + 1) * hb, nhalo - 1), c))

    vec = pl.BlockSpec((1, dc), lambda b, j: (0, 0))
    return pl.pallas_call(
        functools.partial(_conv_kernel, tl=tl),
        grid=(nb, L // tl),
        in_specs=[cur(a_col), cur(a_col + 1), prev(a_col), prev(a_col + 1), nxt(a_col),
                  nxt(a_col + 1), pl.BlockSpec(w_pad.shape, lambda b, j: (0, 0)), vec, vec, vec],
        out_specs=pl.BlockSpec((1, tl, dc), lambda b, j: (b, j, 0)),
        out_shape=jax.ShapeDtypeStruct((nb, L, dc), F32),
        scratch_shapes=[pltpu.VMEM((tl + 2 * CONV_HALO, dc), F32)],
        compiler_params=_cparams(("arbitrary", "arbitrary")),
        name="conv_group",
    )(proj, proj, proj, proj, proj, proj, w_pad, bias.reshape(1, dc), ln_g.reshape(1, dc),
      ln_b.reshape(1, dc))


def _mix_kernel(of_ref, ob_ref, gr_ref, oc_ref, x_ref, hg_ref, g1_ref, sh2_ref, sc2_ref, ng_ref,
                wo_ref, wq_ref, k1_ref, k2_ref, h1_ref, hn2_ref, s_ref):
    o = of_ref[0] + ob_ref[0]
    parts = []
    for hh in range(o.shape[1] // HEAD_DIM):
        oh = o[:, hh * HEAD_DIM:(hh + 1) * HEAD_DIM]
        parts.append(oh * lax.rsqrt(jnp.mean(oh * oh, axis=-1, keepdims=True) + EPS))
    on = jnp.concatenate(parts, axis=-1) * hg_ref[...] * _silu(gr_ref[0])
    cat = jnp.concatenate([on, oc_ref[0]], axis=-1).astype(BF16)
    mix = jnp.dot(cat, wo_ref[...], preferred_element_type=F32)
    h1 = x_ref[0] + g1_ref[0] * mix
    h1_ref[0] = h1
    hn2 = _rms_mod(h1, ng_ref[...], sh2_ref[0], sc2_ref[0])
    hn2_ref[0] = hn2
    q = jnp.dot(hn2.astype(BF16), wq_ref[...], preferred_element_type=F32)
    k1h, k1l, _ = _split3(k1_ref[...])
    k2h, k2l, _ = _split3(k2_ref[...])
    for hd in range(PEER_HEADS):
        for half, (kh, kl) in enumerate(((k1h, k1l), (k2h, k2l))):
            c0 = hd * 2 * PEER_HALF + half * PEER_HALF
            qh, ql, _ = _split3(q[:, c0:c0 + PEER_HALF])
            s_ref[2 * hd + half] = _dot_nt(kh, qh) + _dot_nt(kh, ql) + _dot_nt(kl, qh)


def _mix(o_f, o_b, proj, o_c, x, hgrn_g, mod3, norm_g, w_out_bf16, wq_bf16, keys1, keys2, g_col):
    nb, L, d = x.shape
    dh = o_f.shape[2]
    dq = wq_bf16.shape[1]
    tt = 256
    nj = L // tt

    def half(c=0):
        return pl.BlockSpec((1, tt, dh), lambda b, j: (b, j, c))

    def full():
        return pl.BlockSpec((1, tt, d), lambda b, j: (b, j, 0))

    def modc(c):
        return pl.BlockSpec((1, 1, d), lambda b, j: (b, 0, c))

    def const(shape):
        return pl.BlockSpec(shape, lambda b, j: (0,) * len(shape))

    return pl.pallas_call(
        _mix_kernel,
        grid=(nb, nj),
        in_specs=[half(), half(), half(g_col), half(), full(), const((1, dh)),
                  modc(2), modc(3), modc(4), const((1, d)), const((d, d)), const((d, dq)),
                  const(keys1.shape), const(keys2.shape)],
        out_specs=[full(), full(),
                   pl.BlockSpec((2 * PEER_HEADS, PEER_KEYS, tt), lambda b, j: (0, 0, b * nj + j))],
        out_shape=[jax.ShapeDtypeStruct((nb, L, d), F32), jax.ShapeDtypeStruct((nb, L, d), F32),
                   jax.ShapeDtypeStruct((2 * PEER_HEADS, PEER_KEYS, nb * L), F32)],
        compiler_params=_cparams(("arbitrary", "arbitrary")),
        name="mix_scores",
    )(o_f, o_b, proj, o_c, x, hgrn_g.reshape(1, dh), mod3, mod3, mod3, norm_g.reshape(1, d),
      w_out_bf16, wq_bf16, keys1, keys2)


def _top16(s):
    n = s.shape[0]
    iota = lax.broadcasted_iota(I32, s.shape, 0)
    vals, idxs = [], []
    for _ in range(PEER_TOPK):
        m = jnp.max(s, axis=0, keepdims=True)
        idx = jnp.min(jnp.where(s == m, iota, n), axis=0, keepdims=True)
        vals.append(m)
        idxs.append(idx)
        s = jnp.where(iota == idx, -jnp.inf, s)
    return jnp.concatenate(vals, axis=0), jnp.concatenate(idxs, axis=0)


def _take16(table, sel):
    out = jnp.zeros_like(table)
    for a in range(PEER_TOPK):
        out = jnp.where(sel == a, table[a:a + 1], out)
    return out


def _topk_kernel(s_ref, e_ref, w_ref):
    es, ws = [], []
    for hd in range(PEER_HEADS):
        v1, i1 = _top16(s_ref[2 * hd])
        v2, i2 = _top16(s_ref[2 * hd + 1])
        cand = jnp.concatenate([v1[a:a + 1] + v2 for a in range(PEER_TOPK)], axis=0)
        sc, ci = _top16(cand)
        e = _take16(i1, ci // PEER_TOPK) * PEER_KEYS + _take16(i2, ci % PEER_TOPK)
        p = jnp.exp(sc - sc[0:1])
        es.append(e)
        ws.append(p / jnp.sum(p, axis=0, keepdims=True))
    e_ref[...] = jnp.concatenate(es, axis=0).T
    w_ref[...] = jnp.concatenate(ws, axis=0).T


def _topk(scores):
    npair, nk, T = scores.shape
    nsel = PEER_HEADS * PEER_TOPK
    tt = 256
    return pl.pallas_call(
        _topk_kernel,
        grid=(T // tt,),
        in_specs=[pl.BlockSpec((npair, nk, tt), lambda i: (0, 0, i))],
        out_specs=[pl.BlockSpec((tt, nsel), lambda i: (i, 0)),
                   pl.BlockSpec((tt, nsel), lambda i: (i, 0))],
        out_shape=[jax.ShapeDtypeStruct((T, nsel), I32), jax.ShapeDtypeStruct((T, nsel), F32)],
        compiler_params=_cparams(("arbitrary",)),
        name="peer_topk",
    )(scores)


WORDS = 4


def _pack_words(x):
    bits = pltpu.bitcast(x.astype(BF16).astype(F32), I32)
    out = []
    for c in range(WORDS):
        lo = bits[:, (2 * c) * LANES:(2 * c + 1) * LANES]
        hi = bits[:, (2 * c + 1) * LANES:(2 * c + 2) * LANES]
        out.append(lax.shift_right_logical(lo, 16) | (hi & jnp.int32(-65536)))
    return jnp.concatenate(out, axis=-1)


def _pack_kernel(x_ref, o_ref):
    o_ref[...] = _pack_words(x_ref[...])


def _pack_table(x):
    ne, d = x.shape
    te = 512
    packed = pl.pallas_call(
        _pack_kernel,
        grid=(ne // te,),
        in_specs=[pl.BlockSpec((te, d), lambda i: (i, 0))],
        out_specs=pl.BlockSpec((te, d // 2), lambda i: (i, 0)),
        out_shape=jax.ShapeDtypeStruct((ne, d // 2), I32),
        compiler_params=_cparams(("arbitrary",)),
        name="pack_table",
    )(x)
    return packed.reshape(ne, WORDS, LANES)


NSEL = PEER_HEADS * PEER_TOPK
ROW_TILE = 2 * WORDS


SC_LANES = 16
SC_ROWS = 64
SC_TOK = 16
SC_GROUP = 8
VS_GROUP = 2
PART_ROWS = NSEL * SC_LANES // LANES


def _sc_unpack(w):
    lo = plsc.bitcast(lax.shift_left(w, jnp.int32(16)), F32)
    hi = plsc.bitcast(w & jnp.int32(-65536), F32)
    return lo, hi


def _sc_mesh():
    return plsc.VectorSubcoreMesh(core_axis_name="c", subcore_axis_name="s")


def _sc_params():
    return pltpu.CompilerParams(use_tc_tiling_on_sc=True, needs_layout_passes=False)


def _sc_udot(table3, idx, hn3):
    T = hn3.shape[0]
    info = plsc.get_sparse_core_info()
    nw = info.num_cores * info.num_subcores
    tpw = T // nw
    assert T % nw == 0 and tpw % SC_TOK == 0 and NSEL % SC_ROWS == 0
    nchunk = tpw // SC_TOK
    qsteps = NSEL // SC_ROWS
    nstep = SC_TOK * qsteps

    @functools.partial(
        pl.kernel, mesh=_sc_mesh(),
        out_type=jax.ShapeDtypeStruct((T, PART_ROWS, LANES), F32),
        scratch_types=[pltpu.VMEM((SC_TOK * NSEL,), I32),
                       pltpu.VMEM((2, SC_ROWS, WORDS, LANES), I32),
                       pltpu.VMEM((SC_TOK, ROW_TILE * LANES), F32),
                       pltpu.VMEM((SC_TOK, PART_ROWS, LANES), F32),
                       pltpu.SemaphoreType.DMA((2,))],
        compiler_params=_sc_params(),
        name="sc_udot",
    )
    def k(table_hbm, idx_hbm, hn_hbm, out_hbm, idx_v, rows_v, h_v, p_v, sem_g):
        wid = lax.axis_index("s") * info.num_cores + lax.axis_index("c")
        tbase = wid * tpw

        def gather(st, slot):
            return pltpu.make_async_copy(table_hbm.at[idx_v.at[pl.ds(st * SC_ROWS, SC_ROWS)]],
                                         rows_v.at[slot], sem_g.at[slot])

        def compute(tk, q, slot):
            @pl.loop(0, SC_ROWS // SC_GROUP)
            def _(g):
                acc = [jnp.zeros((SC_LANES,), F32) for _ in range(SC_GROUP)]
                for c in range(WORDS):
                    for lv in range(LANES // SC_LANES):
                        ls = pl.ds(lv * SC_LANES, SC_LANES)
                        hl = h_v[tk, pl.ds((2 * c) * LANES + lv * SC_LANES, SC_LANES)]
                        hh = h_v[tk, pl.ds((2 * c + 1) * LANES + lv * SC_LANES, SC_LANES)]
                        for i in range(SC_GROUP):
                            lo, hi = _sc_unpack(rows_v[slot, g * SC_GROUP + i, c, ls])
                            acc[i] = acc[i] + lo * hl + hi * hh
                for i in range(SC_GROUP):
                    p_v[tk, q * (SC_ROWS // SC_GROUP) + g, pl.ds(i * SC_LANES, SC_LANES)] = acc[i]

        @pl.loop(0, nchunk)
        def _(ch):
            t0 = tbase + ch * SC_TOK
            pltpu.sync_copy(idx_hbm.at[pl.ds(t0 * NSEL, SC_TOK * NSEL)], idx_v)
            pltpu.sync_copy(hn_hbm.at[pl.ds(t0, SC_TOK)], h_v)
            gather(0, 0).start()

            @pl.loop(0, SC_TOK)
            def _(tk):
                for q in range(qsteps):
                    st = tk * qsteps + q
                    slot = q % 2

                    @pl.when(st + 1 < nstep)
                    def _():
                        gather(st + 1, 1 - slot).start()

                    gather(st, slot).wait()
                    compute(tk, q, slot)

            pltpu.sync_copy(p_v, out_hbm.at[pl.ds(t0, SC_TOK)])

    return k(table3, idx, hn3)


def _sc_vsum(table3, idx, coefx):
    T = coefx.shape[0]
    info = plsc.get_sparse_core_info()
    nw = info.num_cores * info.num_subcores
    tpw = T // nw
    assert T % nw == 0 and tpw % SC_TOK == 0 and NSEL % SC_ROWS == 0 and SC_GROUP % VS_GROUP == 0
    nchunk = tpw // SC_TOK
    qsteps = NSEL // SC_ROWS
    nstep = SC_TOK * qsteps
    nlv = LANES // SC_LANES
    sub = SC_GROUP // VS_GROUP

    @functools.partial(
        pl.kernel, mesh=_sc_mesh(),
        out_type=jax.ShapeDtypeStruct((T, ROW_TILE * LANES), F32),
        scratch_types=[pltpu.VMEM((SC_TOK * NSEL,), I32),
                       pltpu.VMEM((2, SC_ROWS, WORDS, LANES), I32),
                       pltpu.VMEM((SC_TOK, PART_ROWS, LANES), F32),
                       pltpu.VMEM((SC_TOK, ROW_TILE * LANES), F32),
                       pltpu.SemaphoreType.DMA((2,))],
        compiler_params=_sc_params(),
        name="sc_vsum",
    )
    def k(table_hbm, idx_hbm, coef_hbm, out_hbm, idx_v, rows_v, c_v, y_v, sem_g):
        wid = lax.axis_index("s") * info.num_cores + lax.axis_index("c")
        tbase = wid * tpw

        def gather(st, slot):
            return pltpu.make_async_copy(table_hbm.at[idx_v.at[pl.ds(st * SC_ROWS, SC_ROWS)]],
                                         rows_v.at[slot], sem_g.at[slot])

        def compute(tk, q, slot):
            for c in range(WORDS):
                if q == 0:
                    init = tuple(jnp.zeros((SC_LANES,), F32) for _ in range(2 * nlv))
                else:
                    init = tuple(y_v[tk, pl.ds((2 * c + p) * LANES + lv * SC_LANES, SC_LANES)]
                                 for p in range(2) for lv in range(nlv))

                def body(g, acc):
                    acc = list(acc)
                    for i in range(VS_GROUP):
                        cv = c_v[tk, q * (SC_ROWS // SC_GROUP) + g // sub,
                                 pl.ds(((g % sub) * VS_GROUP + i) * SC_LANES, SC_LANES)]
                        for lv in range(nlv):
                            lo, hi = _sc_unpack(rows_v[slot, g * VS_GROUP + i, c,
                                                       pl.ds(lv * SC_LANES, SC_LANES)])
                            acc[lv] = acc[lv] + cv * lo
                            acc[nlv + lv] = acc[nlv + lv] + cv * hi
                    return tuple(acc)

                acc = lax.fori_loop(0, SC_ROWS // VS_GROUP, body, init)
                for p in range(2):
                    for lv in range(nlv):
                        y_v[tk, pl.ds((2 * c + p) * LANES + lv * SC_LANES, SC_LANES)] = acc[p * nlv + lv]

        @pl.loop(0, nchunk)
        def _(ch):
            t0 = tbase + ch * SC_TOK
            pltpu.sync_copy(idx_hbm.at[pl.ds(t0 * NSEL, SC_TOK * NSEL)], idx_v)
            pltpu.sync_copy(coef_hbm.at[pl.ds(t0, SC_TOK)], c_v)
            gather(0, 0).start()

            @pl.loop(0, SC_TOK)
            def _(tk):
                for q in range(qsteps):
                    st = tk * qsteps + q
                    slot = q % 2

                    @pl.when(st + 1 < nstep)
                    def _():
                        gather(st + 1, 1 - slot).start()

                    gather(st, slot).wait()
                    compute(tk, q, slot)

            pltpu.sync_copy(y_v, out_hbm.at[pl.ds(t0, SC_TOK)])

    return k(table3, idx, coefx)


def _coef_kernel(p_ref, w_ref, cx_ref, *, tt):
    row = lax.broadcasted_iota(I32, (LANES, NSEL), 0)
    col = lax.broadcasted_iota(I32, (LANES, NSEL), 1)
    per_row = LANES // SC_LANES
    act = jnp.zeros((tt, NSEL), F32)
    for s in range(PART_ROWS):
        fold = (col == per_row * s + row // SC_LANES).astype(BF16)
        p1, p2, p3 = _split3(p_ref[pl.ds(s, tt, stride=PART_ROWS), :])
        act = act + (jnp.dot(p1, fold, preferred_element_type=F32) + jnp.dot(p2, fold, preferred_element_type=F32)
                     + jnp.dot(p3, fold, preferred_element_type=F32))
    coef = w_ref[...] * (0.5 * act * (1.0 + lax.erf(act * (2.0 ** -0.5))))
    c1, c2, c3 = _split3(coef)
    for s in range(PART_ROWS):
        spread = (row == per_row * s + col // SC_LANES).astype(BF16)
        cx_ref[pl.ds(s, tt, stride=PART_ROWS), :] = (
            jnp.dot(c1, spread, preferred_element_type=F32) + jnp.dot(c2, spread, preferred_element_type=F32)
            + jnp.dot(c3, spread, preferred_element_type=F32))


def _coef(p2d, w):
    T = w.shape[0]
    tt = 256
    return pl.pallas_call(
        functools.partial(_coef_kernel, tt=tt),
        grid=(T // tt,),
        in_specs=[pl.BlockSpec((tt * PART_ROWS, LANES), lambda i: (i, 0)),
                  pl.BlockSpec((tt, NSEL), lambda i: (i, 0))],
        out_specs=pl.BlockSpec((tt * PART_ROWS, LANES), lambda i: (i, 0)),
        out_shape=jax.ShapeDtypeStruct((T * PART_ROWS, LANES), F32),
        compiler_params=_cparams(("arbitrary",)),
        name="peer_coef",
    )(p2d, w)


def _experts(e, u_rows, v_rows, hn2, w):
    T = hn2.shape[0]
    idx = e.reshape(T * NSEL)
    p = _sc_udot(u_rows, idx, hn2)
    cx = _coef(p.reshape(T * PART_ROWS, LANES), w)
    return _sc_vsum(v_rows, idx, cx.reshape(T, PART_ROWS, LANES))


def _final_kernel(h1_ref, y_ref, g2_ref, fg_ref, fsh_ref, fsc_ref, o_ref):
    h = h1_ref[0] + g2_ref[0] * y_ref[0]
    o_ref[0] = _rms_mod(h, fg_ref[...], fsh_ref[0], fsc_ref[0])


def _final(h1, y, mod3, fmod3, final_g):
    nb, L, d = h1.shape
    tt = 512
    blk = pl.BlockSpec((1, tt, d), lambda b, j: (b, j, 0))
    return pl.pallas_call(
        _final_kernel,
        grid=(nb, L // tt),
        in_specs=[blk, blk, pl.BlockSpec((1, 1, d), lambda b, j: (b, 0, 5)),
                  pl.BlockSpec((1, d), lambda b, j: (0, 0)),
                  pl.BlockSpec((1, 1, d), lambda b, j: (b, 0, 0)),
                  pl.BlockSpec((1, 1, d), lambda b, j: (b, 0, 1))],
        out_specs=blk,
        out_shape=jax.ShapeDtypeStruct((nb, L, d), F32),
        compiler_params=_cparams(("arbitrary", "arbitrary")),
        name="final_norm",
    )(h1, y, mod3, final_g.reshape(1, d), fmod3, fmod3)


def kernel(x, c, ada_w, ada_b, norm_mix_g, w_in, lb_gamma_fwd, lb_gamma_bwd, hgrn_norm_g, conv_w,
           conv_b, conv_ln_g, conv_ln_b, w_out, norm_ffn_g, peer_wq, peer_keys1, peer_keys2, peer_u,
           peer_v, final_ada_w, final_ada_b, final_norm_g):
    nb, L, d = x.shape
    assert ada_w.shape[0] == 1, "single-layer trunk"
    d_hgrn = lb_gamma_fwd.shape[1]
    nh = d_hgrn // HEAD_DIM

    c_pad = jnp.pad(c, ((0, SUBLANES - nb), (0, 0)))
    mod3 = _ada(c_pad, ada_w[0], ada_b[0])[:nb].reshape(nb, 1, 6 * d)
    fmod3 = _ada(c_pad, final_ada_w, final_ada_b)[:nb].reshape(nb, 1, 2 * d)

    d_conv = conv_w.shape[2]
    w_pad = jnp.pad(conv_w[0], ((0, 1), (0, 0)))
    w_in_bf, w_out_bf, wq_bf = w_in[0].astype(BF16), w_out[0].astype(BF16), peer_wq[0].astype(BF16)
    u_rows, v_rows = _pack_table(peer_u[0]), _pack_table(peer_v[0])

    outs = []
    xb, u_rows, v_rows = lax.optimization_barrier((x[0:1], u_rows, v_rows))
    for b in range(nb):
        mb, fb = mod3[b:b + 1], fmod3[b:b + 1]
        proj = _inproj(xb, norm_mix_g[0], mb, w_in_bf)
        o_f = _hgrn_dir(proj, lb_gamma_fwd, nh, reverse=False)
        o_b = _hgrn_dir(proj, lb_gamma_bwd, 2 * nh, reverse=True)
        o_c = _conv(proj, w_pad, conv_b[0], conv_ln_g[0], conv_ln_b[0], a_col=5 * d_hgrn // d_conv)
        h1, hn2, scores = _mix(o_f, o_b, proj, o_c, xb, hgrn_norm_g[0], mb, norm_ffn_g[0], w_out_bf,
                               wq_bf, peer_keys1[0], peer_keys2[0], g_col=4)
        e, w = _topk(scores)
        if b + 1 < nb:
            e, w, xb = lax.optimization_barrier((e, w, x[b + 1:b + 2]))
        y = _experts(e, u_rows, v_rows, hn2.reshape(L, d), w)
        outs.append(_final(h1, y.reshape(1, L, d), mb, fb, final_norm_g))
    return jnp.concatenate(outs, axis=0)
```

```python
import functools

import jax
import jax.numpy as jnp
from jax import lax
from jax.experimental import pallas as pl
from jax.experimental.pallas import tpu as pltpu
from jax.experimental.pallas import tpu_sc as plsc

F32 = jnp.float32
BF16 = jnp.bfloat16
I32 = jnp.int32

EPS = 1e-6
HEAD_DIM = 128
CHUNK = 64
CONV_WIDTH = 31
CONV_PAD = CONV_WIDTH // 2
PEER_HEADS = 8
PEER_KEYS = 128
PEER_TOPK = 16
PEER_HALF = 128
LANES = 128
SUBLANES = 8
VMEM_LIMIT = 48 * 1024 * 1024


def _cparams(sem):
    return pltpu.CompilerParams(dimension_semantics=sem, vmem_limit_bytes=VMEM_LIMIT)


def _silu(x):
    return x * jax.nn.sigmoid(x)


def _split3(x):
    x1 = x.astype(BF16)
    r1 = x - x1.astype(F32)
    x2 = r1.astype(BF16)
    x3 = (r1 - x2.astype(F32)).astype(BF16)
    return x1, x2, x3


def _dot_nt(a, b):
    return lax.dot_general(a, b, (((1,), (1,)), ((), ())), preferred_element_type=F32)


def _dot_tn(a, b):
    return lax.dot_general(a, b, (((0,), (0,)), ((), ())), preferred_element_type=F32)


def _ada_kernel(c_ref, w_ref, b_ref, o_ref):
    ca = _silu(c_ref[...])
    o_ref[...] = jnp.dot(ca, w_ref[...], preferred_element_type=F32,
                         precision=lax.Precision.HIGHEST) + b_ref[...]


def _ada(c_pad, w, b):
    d, n = w.shape
    tn = 1024
    return pl.pallas_call(
        _ada_kernel,
        grid=(n // tn,),
        in_specs=[pl.BlockSpec((SUBLANES, d), lambda j: (0, 0)),
                  pl.BlockSpec((d, tn), lambda j: (0, j)),
                  pl.BlockSpec((1, tn), lambda j: (0, j))],
        out_specs=pl.BlockSpec((SUBLANES, tn), lambda j: (0, j)),
        out_shape=jax.ShapeDtypeStruct((SUBLANES, n), F32),
        compiler_params=_cparams(("arbitrary",)),
        name="ada_proj",
    )(c_pad, w, b.reshape(1, n))


def _rms_mod(x, g, sh, sc):
    y = x * lax.rsqrt(jnp.mean(x * x, axis=-1, keepdims=True) + EPS) * g
    return y * (1.0 + sc) + sh


def _inproj_kernel(x_ref, g_ref, sh_ref, sc_ref, w_ref, o_ref):
    hn = _rms_mod(x_ref[0], g_ref[...], sh_ref[0], sc_ref[0])
    o_ref[0] = jnp.dot(hn.astype(BF16), w_ref[...], preferred_element_type=F32)


def _inproj(x, g, mod3, w_bf16):
    nb, L, d = x.shape
    n = w_bf16.shape[1]
    tt = 512
    return pl.pallas_call(
        _inproj_kernel,
        grid=(nb, L // tt),
        in_specs=[pl.BlockSpec((1, tt, d), lambda b, j: (b, j, 0)),
                  pl.BlockSpec((1, d), lambda b, j: (0, 0)),
                  pl.BlockSpec((1, 1, d), lambda b, j: (b, 0, 0)),
                  pl.BlockSpec((1, 1, d), lambda b, j: (b, 0, 1)),
                  pl.BlockSpec((d, n), lambda b, j: (0, 0))],
        out_specs=pl.BlockSpec((1, tt, n), lambda b, j: (b, j, 0)),
        out_shape=jax.ShapeDtypeStruct((nb, L, n), F32),
        compiler_params=_cparams(("arbitrary", "arbitrary")),
        name="in_proj",
    )(x, g.reshape(1, d), mod3, mod3, w_bf16)


def _hgrn_kernel(q_ref, f_ref, i_ref, lbg_ref, o_ref, st_ref, *, reverse, lblk):
    @pl.when(pl.program_id(2) == 0)
    def _():
        st_ref[...] = jnp.zeros_like(st_ref)

    lbg = lbg_ref[...]
    ex = jnp.exp(lbg - jnp.max(lbg, axis=0, keepdims=True))
    lb = ex[0:1] / jnp.sum(ex, axis=0, keepdims=True)

    row = lax.broadcasted_iota(I32, (CHUNK, CHUNK), 0)
    col = lax.broadcasted_iota(I32, (CHUNK, CHUNK), 1)
    keep = (col >= row) if reverse else (col <= row)
    tri = keep.astype(BF16)
    i_mid = CHUNK // 2 if reverse else CHUNK // 2 - 1
    i_end = 0 if reverse else CHUNK - 1
    nch = lblk // CHUNK

    def body(ci, st):
        c = (nch - 1 - ci) if reverse else ci
        r0 = pl.multiple_of(c * CHUNK, CHUNK)
        z = f_ref[0, pl.ds(r0, CHUNK), :]
        q = _silu(q_ref[0, pl.ds(r0, CHUNK), :])
        v = i_ref[0, pl.ds(r0, CHUNK), :]
        logf = jnp.log(lb + (1.0 - lb) * jax.nn.sigmoid(z))
        k = (1.0 - lb) * jax.nn.sigmoid(-z)
        l1, l2, l3 = _split3(logf)
        cum = (jnp.dot(tri, l1, preferred_element_type=F32)
               + jnp.dot(tri, l2, preferred_element_type=F32)
               + jnp.dot(tri, l3, preferred_element_type=F32))
        c_mid = cum[i_mid:i_mid + 1]
        c_end = cum[i_end:i_end + 1]
        qr = (q * jnp.exp(cum - c_mid)).astype(BF16)
        kr = (k * jnp.exp(c_mid - cum)).astype(BF16)
        scores = jnp.where(keep, _dot_nt(qr, kr), 0.0)
        vb = v.astype(BF16)
        o_intra = jnp.dot(scores.astype(BF16), vb, preferred_element_type=F32)
        o_inter = _dot_nt((q * jnp.exp(cum)).astype(BF16), st.astype(BF16))
        o_ref[0, pl.ds(r0, CHUNK), :] = o_intra + o_inter
        ku = (k * jnp.exp(c_end - cum)).astype(BF16)
        return st * jnp.exp(c_end) + _dot_tn(vb, ku)

    st_ref[...] = lax.fori_loop(0, nch, body, st_ref[...])


def _hgrn_dir(proj, lb_gamma, f_col, reverse):
    nb, L, _ = proj.shape
    nh = lb_gamma.shape[1] // HEAD_DIM
    lblk = min(1024, L)
    nblk = L // lblk

    def blk(j):
        return (nblk - 1 - j) if reverse else j

    def col_spec(c0):
        return pl.BlockSpec((1, lblk, HEAD_DIM), lambda b, h, j: (b, blk(j), c0 + h))

    return pl.pallas_call(
        functools.partial(_hgrn_kernel, reverse=reverse, lblk=lblk),
        grid=(nb, nh, nblk),
        in_specs=[col_spec(0), col_spec(f_col), col_spec(3 * nh),
                  pl.BlockSpec((lb_gamma.shape[0], HEAD_DIM), lambda b, h, j: (0, h))],
        out_specs=pl.BlockSpec((1, lblk, HEAD_DIM), lambda b, h, j: (b, blk(j), h)),
        out_shape=jax.ShapeDtypeStruct((nb, L, nh * HEAD_DIM), F32),
        scratch_shapes=[pltpu.VMEM((HEAD_DIM, HEAD_DIM), F32)],
        compiler_params=_cparams(("arbitrary", "arbitrary", "arbitrary")),
        name="hgrn_bwd" if reverse else "hgrn_fwd",
    )(proj, proj, proj, lb_gamma)


CONV_HALO = 16
CONV_ROWS = 64


def _conv_kernel(a_ref, g_ref, ap_ref, gp_ref, an_ref, gn_ref, w_ref, b_ref, lg_ref, lb_ref,
                 o_ref, hbuf, *, tl):
    j = pl.program_id(1)
    last = pl.num_programs(1) - 1
    hbuf[CONV_HALO:CONV_HALO + tl, :] = a_ref[0] * jax.nn.sigmoid(g_ref[0])
    hp = ap_ref[0] * jax.nn.sigmoid(gp_ref[0])
    hbuf[0:CONV_HALO, :] = jnp.where(j > 0, hp, 0.0)
    hn = an_ref[0] * jax.nn.sigmoid(gn_ref[0])
    hbuf[CONV_HALO + tl:2 * CONV_HALO + tl, :] = jnp.where(j < last, hn, 0.0)
    off = CONV_HALO - CONV_PAD
    for r in range(tl // CONV_ROWS):
        acc = jnp.zeros((CONV_ROWS, a_ref.shape[2]), F32)
        for k in range(CONV_WIDTH):
            s = r * CONV_ROWS + k + off
            acc = acc + w_ref[k:k + 1, :] * hbuf[s:s + CONV_ROWS, :]
        acc = acc + b_ref[...]
        mu = jnp.mean(acc, axis=-1, keepdims=True)
        cen = acc - mu
        var = jnp.mean(cen * cen, axis=-1, keepdims=True)
        y = cen * lax.rsqrt(var + EPS) * lg_ref[...] + lb_ref[...]
        o_ref[0, r * CONV_ROWS:(r + 1) * CONV_ROWS, :] = _silu(y)


def _conv(proj, w_pad, bias, ln_g, ln_b, a_col):
    nb, L, _ = proj.shape
    dc = w_pad.shape[1]
    tl = 256
    hb = tl // CONV_HALO
    nhalo = L // CONV_HALO

    def cur(c):
        return pl.BlockSpec((1, tl, dc), lambda b, j: (b, j, c))

    def prev(c):
        return pl.BlockSpec((1, CONV_HALO, dc), lambda b, j: (b, jnp.maximum(j * hb - 1, 0), c))

    def nxt(c):
        return pl.BlockSpec((1, CONV_HALO, dc),
                            lambda b, j: (b, jnp.minimum((j + 1) * hb, nhalo - 1), c))

    vec = pl.BlockSpec((1, dc), lambda b, j: (0, 0))
    return pl.pallas_call(
        functools.partial(_conv_kernel, tl=tl),
        grid=(nb, L // tl),
        in_specs=[cur(a_col), cur(a_col + 1), prev(a_col), prev(a_col + 1), nxt(a_col),
                  nxt(a_col + 1), pl.BlockSpec(w_pad.shape, lambda b, j: (0, 0)), vec, vec, vec],
        out_specs=pl.BlockSpec((1, tl, dc), lambda b, j: (b, j, 0)),
        out_shape=jax.ShapeDtypeStruct((nb, L, dc), F32),
        scratch_shapes=[pltpu.VMEM((tl + 2 * CONV_HALO, dc), F32)],
        compiler_params=_cparams(("arbitrary", "arbitrary")),
        name="conv_group",
    )(proj, proj, proj, proj, proj, proj, w_pad, bias.reshape(1, dc), ln_g.reshape(1, dc),
      ln_b.reshape(1, dc))


def _mix_kernel(of_ref, ob_ref, gr_ref, oc_ref, x_ref, hg_ref, g1_ref, sh2_ref, sc2_ref, ng_ref,
                wo_ref, wq_ref, k1_ref, k2_ref, h1_ref, hp_ref, s_ref):
    o = of_ref[0] + ob_ref[0]
    parts = []
    for hh in range(o.shape[1] // HEAD_DIM):
        oh = o[:, hh * HEAD_DIM:(hh + 1) * HEAD_DIM]
        parts.append(oh * lax.rsqrt(jnp.mean(oh * oh, axis=-1, keepdims=True) + EPS))
    on = jnp.concatenate(parts, axis=-1) * hg_ref[...] * _silu(gr_ref[0])
    cat = jnp.concatenate([on, oc_ref[0]], axis=-1).astype(BF16)
    mix = jnp.dot(cat, wo_ref[...], preferred_element_type=F32)
    h1 = x_ref[0] + g1_ref[0] * mix
    h1_ref[0] = h1
    hn2 = _rms_mod(h1, ng_ref[...], sh2_ref[0], sc2_ref[0])
    hp_ref[0] = _pack_words(hn2)
    q = jnp.dot(hn2.astype(BF16), wq_ref[...], preferred_element_type=F32)
    k1h, k1l, _ = _split3(k1_ref[...])
    k2h, k2l, _ = _split3(k2_ref[...])
    for hd in range(PEER_HEADS):
        for half, (kh, kl) in enumerate(((k1h, k1l), (k2h, k2l))):
            c0 = hd * 2 * PEER_HALF + half * PEER_HALF
            qh, ql, _ = _split3(q[:, c0:c0 + PEER_HALF])
            s_ref[2 * hd + half] = _dot_nt(kh, qh) + _dot_nt(kh, ql) + _dot_nt(kl, qh)


def _mix(o_f, o_b, proj, o_c, x, hgrn_g, mod3, norm_g, w_out_bf16, wq_bf16, keys1, keys2, g_col):
    nb, L, d = x.shape
    dh = o_f.shape[2]
    dq = wq_bf16.shape[1]
    tt = 256
    nj = L // tt

    def half(c=0):
        return pl.BlockSpec((1, tt, dh), lambda b, j: (b, j, c))

    def full():
        return pl.BlockSpec((1, tt, d), lambda b, j: (b, j, 0))

    def modc(c):
        return pl.BlockSpec((1, 1, d), lambda b, j: (b, 0, c))

    def const(shape):
        return pl.BlockSpec(shape, lambda b, j: (0,) * len(shape))

    return pl.pallas_call(
        _mix_kernel,
        grid=(nb, nj),
        in_specs=[half(), half(), half(g_col), half(), full(), const((1, dh)),
                  modc(2), modc(3), modc(4), const((1, d)), const((d, d)), const((d, dq)),
                  const(keys1.shape), const(keys2.shape)],
        out_specs=[full(), pl.BlockSpec((1, tt, d // 2), lambda b, j: (b, j, 0)),
                   pl.BlockSpec((2 * PEER_HEADS, PEER_KEYS, tt), lambda b, j: (0, 0, b * nj + j))],
        out_shape=[jax.ShapeDtypeStruct((nb, L, d), F32), jax.ShapeDtypeStruct((nb, L, d // 2), I32),
                   jax.ShapeDtypeStruct((2 * PEER_HEADS, PEER_KEYS, nb * L), F32)],
        compiler_params=_cparams(("arbitrary", "arbitrary")),
        name="mix_scores",
    )(o_f, o_b, proj, o_c, x, hgrn_g.reshape(1, dh), mod3, mod3, mod3, norm_g.reshape(1, d),
      w_out_bf16, wq_bf16, keys1, keys2)


def _top16(s):
    n = s.shape[0]
    iota = lax.broadcasted_iota(I32, s.shape, 0)
    vals, idxs = [], []
    for _ in range(PEER_TOPK):
        m = jnp.max(s, axis=0, keepdims=True)
        idx = jnp.min(jnp.where(s == m, iota, n), axis=0, keepdims=True)
        vals.append(m)
        idxs.append(idx)
        s = jnp.where(iota == idx, -jnp.inf, s)
    return jnp.concatenate(vals, axis=0), jnp.concatenate(idxs, axis=0)


def _take16(table, sel):
    out = jnp.zeros_like(table)
    for a in range(PEER_TOPK):
        out = jnp.where(sel == a, table[a:a + 1], out)
    return out


def _topk_kernel(s_ref, e_ref, w_ref):
    es, ws = [], []
    for hd in range(PEER_HEADS):
        v1, i1 = _top16(s_ref[2 * hd])
        v2, i2 = _top16(s_ref[2 * hd + 1])
        cand = jnp.concatenate([v1[a:a + 1] + v2 for a in range(PEER_TOPK)], axis=0)
        sc, ci = _top16(cand)
        e = _take16(i1, ci // PEER_TOPK) * PEER_KEYS + _take16(i2, ci % PEER_TOPK)
        p = jnp.exp(sc - sc[0:1])
        es.append(e)
        ws.append(p / jnp.sum(p, axis=0, keepdims=True))
    e_ref[...] = jnp.concatenate(es, axis=0).T
    w_ref[...] = jnp.concatenate(ws, axis=0).T


def _topk(scores):
    npair, nk, T = scores.shape
    nsel = PEER_HEADS * PEER_TOPK
    tt = 256
    return pl.pallas_call(
        _topk_kernel,
        grid=(T // tt,),
        in_specs=[pl.BlockSpec((npair, nk, tt), lambda i: (0, 0, i))],
        out_specs=[pl.BlockSpec((tt, nsel), lambda i: (i, 0)),
                   pl.BlockSpec((tt, nsel), lambda i: (i, 0))],
        out_shape=[jax.ShapeDtypeStruct((T, nsel), I32), jax.ShapeDtypeStruct((T, nsel), F32)],
        compiler_params=_cparams(("arbitrary",)),
        name="peer_topk",
    )(scores)


WORDS = 4


def _pack_words(x):
    bits = pltpu.bitcast(x.astype(BF16).astype(F32), I32)
    out = []
    for c in range(WORDS):
        lo = bits[:, (2 * c) * LANES:(2 * c + 1) * LANES]
        hi = bits[:, (2 * c + 1) * LANES:(2 * c + 2) * LANES]
        out.append(lax.shift_right_logical(lo, 16) | (hi & jnp.int32(-65536)))
    return jnp.concatenate(out, axis=-1)


def _pack_kernel(x_ref, o_ref):
    o_ref[...] = _pack_words(x_ref[...])


def _pack_table(x):
    ne, d = x.shape
    te = 512
    packed = pl.pallas_call(
        _pack_kernel,
        grid=(ne // te,),
        in_specs=[pl.BlockSpec((te, d), lambda i: (i, 0))],
        out_specs=pl.BlockSpec((te, d // 2), lambda i: (i, 0)),
        out_shape=jax.ShapeDtypeStruct((ne, d // 2), I32),
        compiler_params=_cparams(("arbitrary",)),
        name="pack_table",
    )(x)
    return packed.reshape(ne, WORDS, LANES)


NSEL = PEER_HEADS * PEER_TOPK
ROW_TILE = 2 * WORDS


SC_LANES = 16
SC_ROWS = 32
SC_BUFS = 4
SC_TOK = 16
SC_GROUP = 8
VS_GROUP = 2
PART_ROWS = NSEL * SC_LANES // LANES


def _sc_unpack(w):
    lo = plsc.bitcast(lax.shift_left(w, jnp.int32(16)), F32)
    hi = plsc.bitcast(w & jnp.int32(-65536), F32)
    return lo, hi


def _sc_mesh():
    return plsc.VectorSubcoreMesh(core_axis_name="c", subcore_axis_name="s")


def _sc_params():
    return pltpu.CompilerParams(use_tc_tiling_on_sc=True, needs_layout_passes=False)


def _sc_udot(table3, idx, hn3):
    T = hn3.shape[0]
    info = plsc.get_sparse_core_info()
    nw = info.num_cores * info.num_subcores
    tpw = T // nw
    assert T % nw == 0 and tpw % SC_TOK == 0 and NSEL % SC_ROWS == 0
    nchunk = tpw // SC_TOK
    qsteps = NSEL // SC_ROWS
    nstep = SC_TOK * qsteps

    @functools.partial(
        pl.kernel, mesh=_sc_mesh(),
        out_type=jax.ShapeDtypeStruct((T, PART_ROWS, LANES), F32),
        scratch_types=[pltpu.VMEM((SC_TOK * NSEL,), I32),
                       pltpu.VMEM((SC_BUFS, SC_ROWS, WORDS, LANES), I32),
                       pltpu.VMEM((SC_TOK, WORDS * LANES), I32),
                       pltpu.VMEM((SC_TOK, PART_ROWS, LANES), F32),
                       pltpu.SemaphoreType.DMA((SC_BUFS,))],
        compiler_params=_sc_params(),
        name="sc_udot",
    )
    def k(table_hbm, idx_hbm, hn_hbm, out_hbm, idx_v, rows_v, h_v, p_v, sem_g):
        wid = lax.axis_index("s") * info.num_cores + lax.axis_index("c")
        tbase = wid * tpw

        def gather(st, slot):
            return pltpu.make_async_copy(table_hbm.at[idx_v.at[pl.ds(st * SC_ROWS, SC_ROWS)]],
                                         rows_v.at[slot], sem_g.at[slot])

        def compute(tk, q, slot):
            @pl.loop(0, SC_ROWS // SC_GROUP)
            def _(g):
                acc = [jnp.zeros((SC_LANES,), F32) for _ in range(SC_GROUP)]
                for c in range(WORDS):
                    for lv in range(0, LANES // SC_LANES, 2):
                        ls = [pl.ds((lv + d) * SC_LANES, SC_LANES) for d in range(2)]
                        hb = [plsc.bitcast(h_v[tk, pl.ds(c * LANES + (lv + d) * SC_LANES, SC_LANES)], BF16)
                              for d in range(2)]
                        for i in range(SC_GROUP):
                            pr = [plsc.bitcast(rows_v[slot, g * SC_GROUP + i, c, ls[d]], BF16) * hb[d]
                                  for d in range(2)]
                            lo, hi = _sc_unpack(plsc.bitcast(pr[0] + pr[1], I32))
                            acc[i] = acc[i] + lo + hi
                for i in range(SC_GROUP):
                    p_v[tk, q * (SC_ROWS // SC_GROUP) + g, pl.ds(i * SC_LANES, SC_LANES)] = acc[i]

        @pl.loop(0, nchunk)
        def _(ch):
            t0 = tbase + ch * SC_TOK
            pltpu.sync_copy(idx_hbm.at[pl.ds(t0 * NSEL, SC_TOK * NSEL)], idx_v)
            pltpu.sync_copy(hn_hbm.at[pl.ds(t0, SC_TOK)], h_v)
            for s in range(SC_BUFS - 1):
                gather(s, s).start()

            @pl.loop(0, SC_TOK)
            def _(tk):
                for q in range(qsteps):
                    st = tk * qsteps + q
                    slot = q % SC_BUFS

                    @pl.when(st + SC_BUFS - 1 < nstep)
                    def _():
                        gather(st + SC_BUFS - 1, (q + SC_BUFS - 1) % SC_BUFS).start()

                    gather(st, slot).wait()
                    compute(tk, q, slot)

            pltpu.sync_copy(p_v, out_hbm.at[pl.ds(t0, SC_TOK)])

    return k(table3, idx, hn3)


def _sc_vsum(table3, idx, coefx):
    assert VS_GROUP == 2, "the loop body sums the products of exactly two experts in bf16"
    T = coefx.shape[0]
    info = plsc.get_sparse_core_info()
    nw = info.num_cores * info.num_subcores
    tpw = T // nw
    assert T % nw == 0 and tpw % SC_TOK == 0 and NSEL % SC_ROWS == 0 and SC_GROUP % VS_GROUP == 0
    nchunk = tpw // SC_TOK
    qsteps = NSEL // SC_ROWS
    nstep = SC_TOK * qsteps
    nlv = LANES // SC_LANES
    sub = SC_GROUP // VS_GROUP

    @functools.partial(
        pl.kernel, mesh=_sc_mesh(),
        out_type=jax.ShapeDtypeStruct((T, ROW_TILE * LANES), F32),
        scratch_types=[pltpu.VMEM((SC_TOK * NSEL,), I32),
                       pltpu.VMEM((SC_BUFS, SC_ROWS, WORDS, LANES), I32),
                       pltpu.VMEM((SC_TOK, PART_ROWS, LANES), I32),
                       pltpu.VMEM((SC_TOK, ROW_TILE * LANES), F32),
                       pltpu.SemaphoreType.DMA((SC_BUFS,))],
        compiler_params=_sc_params(),
        name="sc_vsum",
    )
    def k(table_hbm, idx_hbm, coef_hbm, out_hbm, idx_v, rows_v, c_v, y_v, sem_g):
        wid = lax.axis_index("s") * info.num_cores + lax.axis_index("c")
        tbase = wid * tpw

        def gather(st, slot):
            return pltpu.make_async_copy(table_hbm.at[idx_v.at[pl.ds(st * SC_ROWS, SC_ROWS)]],
                                         rows_v.at[slot], sem_g.at[slot])

        def compute(tk, q, slot):
            for c in range(WORDS):
                if q == 0:
                    init = tuple(jnp.zeros((SC_LANES,), F32) for _ in range(2 * nlv))
                else:
                    init = tuple(y_v[tk, pl.ds((2 * c + p) * LANES + lv * SC_LANES, SC_LANES)]
                                 for p in range(2) for lv in range(nlv))

                def body(g, acc):
                    acc = list(acc)
                    cb = [plsc.bitcast(c_v[tk, q * (SC_ROWS // SC_GROUP) + g // sub,
                                           pl.ds(((g % sub) * VS_GROUP + i) * SC_LANES, SC_LANES)], BF16)
                          for i in range(VS_GROUP)]
                    for lv in range(nlv):
                        pr = [cb[i] * plsc.bitcast(rows_v[slot, g * VS_GROUP + i, c,
                                                          pl.ds(lv * SC_LANES, SC_LANES)], BF16)
                              for i in range(VS_GROUP)]
                        lo, hi = _sc_unpack(plsc.bitcast(pr[0] + pr[1], I32))
                        acc[lv] = acc[lv] + lo
                        acc[nlv + lv] = acc[nlv + lv] + hi
                    return tuple(acc)

                acc = lax.fori_loop(0, SC_ROWS // VS_GROUP, body, init)
                for p in range(2):
                    for lv in range(nlv):
                        y_v[tk, pl.ds((2 * c + p) * LANES + lv * SC_LANES, SC_LANES)] = acc[p * nlv + lv]

        @pl.loop(0, nchunk)
        def _(ch):
            t0 = tbase + ch * SC_TOK
            pltpu.sync_copy(idx_hbm.at[pl.ds(t0 * NSEL, SC_TOK * NSEL)], idx_v)
            pltpu.sync_copy(coef_hbm.at[pl.ds(t0, SC_TOK)], c_v)
            for s in range(SC_BUFS - 1):
                gather(s, s).start()

            @pl.loop(0, SC_TOK)
            def _(tk):
                for q in range(qsteps):
                    st = tk * qsteps + q
                    slot = q % SC_BUFS

                    @pl.when(st + SC_BUFS - 1 < nstep)
                    def _():
                        gather(st + SC_BUFS - 1, (q + SC_BUFS - 1) % SC_BUFS).start()

                    gather(st, slot).wait()
                    compute(tk, q, slot)

            pltpu.sync_copy(y_v, out_hbm.at[pl.ds(t0, SC_TOK)])

    return k(table3, idx, coefx)


def _coef_kernel(p_ref, w_ref, cx_ref, *, tt):
    row = lax.broadcasted_iota(I32, (LANES, NSEL), 0)
    col = lax.broadcasted_iota(I32, (LANES, NSEL), 1)
    per_row = LANES // SC_LANES
    act = jnp.zeros((tt, NSEL), F32)
    for s in range(PART_ROWS):
        fold = (col == per_row * s + row // SC_LANES).astype(BF16)
        p1, p2, p3 = _split3(p_ref[pl.ds(s, tt, stride=PART_ROWS), :])
        act = act + (jnp.dot(p1, fold, preferred_element_type=F32) + jnp.dot(p2, fold, preferred_element_type=F32)
                     + jnp.dot(p3, fold, preferred_element_type=F32))
    coef = w_ref[...] * (0.5 * act * (1.0 + lax.erf(act * (2.0 ** -0.5))))
    cb = coef.astype(BF16)
    for s in range(PART_ROWS):
        spread = (row == per_row * s + col // SC_LANES).astype(BF16)
        bits = pltpu.bitcast(jnp.dot(cb, spread, preferred_element_type=F32), I32)
        cx_ref[pl.ds(s, tt, stride=PART_ROWS), :] = bits | lax.shift_right_logical(bits, 16)


def _coef(p2d, w):
    T = w.shape[0]
    tt = 256
    return pl.pallas_call(
        functools.partial(_coef_kernel, tt=tt),
        grid=(T // tt,),
        in_specs=[pl.BlockSpec((tt * PART_ROWS, LANES), lambda i: (i, 0)),
                  pl.BlockSpec((tt, NSEL), lambda i: (i, 0))],
        out_specs=pl.BlockSpec((tt * PART_ROWS, LANES), lambda i: (i, 0)),
        out_shape=jax.ShapeDtypeStruct((T * PART_ROWS, LANES), I32),
        compiler_params=_cparams(("arbitrary",)),
        name="peer_coef",
    )(p2d, w)


def _experts(e, u_rows, v_rows, hn2, w):
    T = hn2.shape[0]
    idx = e.reshape(T * NSEL)
    p = _sc_udot(u_rows, idx, hn2)
    cx = _coef(p.reshape(T * PART_ROWS, LANES), w)
    return _sc_vsum(v_rows, idx, cx.reshape(T, PART_ROWS, LANES))


def _final_kernel(h1_ref, y_ref, g2_ref, fg_ref, fsh_ref, fsc_ref, o_ref):
    h = h1_ref[0] + g2_ref[0] * y_ref[0]
    o_ref[0] = _rms_mod(h, fg_ref[...], fsh_ref[0], fsc_ref[0])


def _final(h1, y, mod3, fmod3, final_g):
    nb, L, d = h1.shape
    tt = 512
    blk = pl.BlockSpec((1, tt, d), lambda b, j: (b, j, 0))
    return pl.pallas_call(
        _final_kernel,
        grid=(nb, L // tt),
        in_specs=[blk, blk, pl.BlockSpec((1, 1, d), lambda b, j: (b, 0, 5)),
                  pl.BlockSpec((1, d), lambda b, j: (0, 0)),
                  pl.BlockSpec((1, 1, d), lambda b, j: (b, 0, 0)),
                  pl.BlockSpec((1, 1, d), lambda b, j: (b, 0, 1))],
        out_specs=blk,
        out_shape=jax.ShapeDtypeStruct((nb, L, d), F32),
        compiler_params=_cparams(("arbitrary", "arbitrary")),
        name="final_norm",
    )(h1, y, mod3, final_g.reshape(1, d), fmod3, fmod3)


def kernel(x, c, ada_w, ada_b, norm_mix_g, w_in, lb_gamma_fwd, lb_gamma_bwd, hgrn_norm_g, conv_w,
           conv_b, conv_ln_g, conv_ln_b, w_out, norm_ffn_g, peer_wq, peer_keys1, peer_keys2, peer_u,
           peer_v, final_ada_w, final_ada_b, final_norm_g):
    nb, L, d = x.shape
    assert ada_w.shape[0] == 1, "single-layer trunk"
    d_hgrn = lb_gamma_fwd.shape[1]
    nh = d_hgrn // HEAD_DIM

    c_pad = jnp.pad(c, ((0, SUBLANES - nb), (0, 0)))
    mod3 = _ada(c_pad, ada_w[0], ada_b[0])[:nb].reshape(nb, 1, 6 * d)
    fmod3 = _ada(c_pad, final_ada_w, final_ada_b)[:nb].reshape(nb, 1, 2 * d)

    d_conv = conv_w.shape[2]
    w_pad = jnp.pad(conv_w[0], ((0, 1), (0, 0)))
    w_in_bf, w_out_bf, wq_bf = w_in[0].astype(BF16), w_out[0].astype(BF16), peer_wq[0].astype(BF16)
    u_rows, v_rows = _pack_table(peer_u[0]), _pack_table(peer_v[0])

    outs = []
    xb, u_rows, v_rows = lax.optimization_barrier((x[0:1], u_rows, v_rows))
    for b in range(nb):
        mb, fb = mod3[b:b + 1], fmod3[b:b + 1]
        proj = _inproj(xb, norm_mix_g[0], mb, w_in_bf)
        o_f = _hgrn_dir(proj, lb_gamma_fwd, nh, reverse=False)
        o_b = _hgrn_dir(proj, lb_gamma_bwd, 2 * nh, reverse=True)
        o_c = _conv(proj, w_pad, conv_b[0], conv_ln_g[0], conv_ln_b[0], a_col=5 * d_hgrn // d_conv)
        h1, hn2, scores = _mix(o_f, o_b, proj, o_c, xb, hgrn_norm_g[0], mb, norm_ffn_g[0], w_out_bf,
                               wq_bf, peer_keys1[0], peer_keys2[0], g_col=4)
        e, w = _topk(scores)
        if b + 1 < nb:
            e, w, xb = lax.optimization_barrier((e, w, x[b + 1:b + 2]))
        y = _experts(e, u_rows, v_rows, hn2.reshape(L, d // 2), w)
        outs.append(_final(h1, y.reshape(1, L, d), mb, fb, final_norm_g))
    return jnp.concatenate(outs, axis=0)
```

```python
import functools

import jax
import jax.numpy as jnp
from jax import lax
from jax.experimental import pallas as pl
from jax.experimental.pallas import tpu as pltpu
from jax.experimental.pallas import tpu_sc as plsc

F32 = jnp.float32
BF16 = jnp.bfloat16
I32 = jnp.int32

EPS = 1e-6
HEAD_DIM = 128
CHUNK = 64
CONV_WIDTH = 31
CONV_PAD = CONV_WIDTH // 2
PEER_HEADS = 8
PEER_KEYS = 128
PEER_TOPK = 16
PEER_HALF = 128
LANES = 128
SUBLANES = 8
VMEM_LIMIT = 48 * 1024 * 1024


def _cparams(sem):
    return pltpu.CompilerParams(dimension_semantics=sem, vmem_limit_bytes=VMEM_LIMIT)


def _silu(x):
    return x * jax.nn.sigmoid(x)


def _split3(x):
    x1 = x.astype(BF16)
    r1 = x - x1.astype(F32)
    x2 = r1.astype(BF16)
    x3 = (r1 - x2.astype(F32)).astype(BF16)
    return x1, x2, x3


def _dot_nt(a, b):
    return lax.dot_general(a, b, (((1,), (1,)), ((), ())), preferred_element_type=F32)


def _dot_tn(a, b):
    return lax.dot_general(a, b, (((0,), (0,)), ((), ())), preferred_element_type=F32)


def _ada_kernel(c_ref, w_ref, b_ref, o_ref):
    ca = _silu(c_ref[...])
    o_ref[...] = jnp.dot(ca, w_ref[...], preferred_element_type=F32,
                         precision=lax.Precision.HIGHEST) + b_ref[...]


def _ada(c_pad, w, b):
    d, n = w.shape
    tn = 1024
    return pl.pallas_call(
        _ada_kernel,
        grid=(n // tn,),
        in_specs=[pl.BlockSpec((SUBLANES, d), lambda j: (0, 0)),
                  pl.BlockSpec((d, tn), lambda j: (0, j)),
                  pl.BlockSpec((1, tn), lambda j: (0, j))],
        out_specs=pl.BlockSpec((SUBLANES, tn), lambda j: (0, j)),
        out_shape=jax.ShapeDtypeStruct((SUBLANES, n), F32),
        compiler_params=_cparams(("arbitrary",)),
        name="ada_proj",
    )(c_pad, w, b.reshape(1, n))


def _rms_mod(x, g, sh, sc):
    y = x * lax.rsqrt(jnp.mean(x * x, axis=-1, keepdims=True) + EPS) * g
    return y * (1.0 + sc) + sh


def _inproj_kernel(x_ref, g_ref, sh_ref, sc_ref, w_ref, o_ref):
    hn = _rms_mod(x_ref[0], g_ref[...], sh_ref[0], sc_ref[0])
    o_ref[0] = jnp.dot(hn.astype(BF16), w_ref[...], preferred_element_type=F32)


def _inproj(x, g, mod3, w_bf16):
    nb, L, d = x.shape
    n = w_bf16.shape[1]
    tt = 512
    return pl.pallas_call(
        _inproj_kernel,
        grid=(nb, L // tt),
        in_specs=[pl.BlockSpec((1, tt, d), lambda b, j: (b, j, 0)),
                  pl.BlockSpec((1, d), lambda b, j: (0, 0)),
                  pl.BlockSpec((1, 1, d), lambda b, j: (b, 0, 0)),
                  pl.BlockSpec((1, 1, d), lambda b, j: (b, 0, 1)),
                  pl.BlockSpec((d, n), lambda b, j: (0, 0))],
        out_specs=pl.BlockSpec((1, tt, n), lambda b, j: (b, j, 0)),
        out_shape=jax.ShapeDtypeStruct((nb, L, n), F32),
        compiler_params=_cparams(("arbitrary", "arbitrary")),
        name="in_proj",
    )(x, g.reshape(1, d), mod3, mod3, w_bf16)


def _lower_bound(lbg):
    ex = jnp.exp(lbg - jnp.max(lbg, axis=0, keepdims=True))
    return ex[0:1] / jnp.sum(ex, axis=0, keepdims=True)


def _hgrn_chunk(q_ref, f_ref, i_ref, o_ref, lb, st, c, reverse):
    row = lax.broadcasted_iota(I32, (CHUNK, CHUNK), 0)
    col = lax.broadcasted_iota(I32, (CHUNK, CHUNK), 1)
    keep = (col >= row) if reverse else (col <= row)
    tri = keep.astype(BF16)
    i_mid = CHUNK // 2 if reverse else CHUNK // 2 - 1
    i_end = 0 if reverse else CHUNK - 1
    r0 = pl.multiple_of(c * CHUNK, CHUNK)
    z = f_ref[0, pl.ds(r0, CHUNK), :]
    q = _silu(q_ref[0, pl.ds(r0, CHUNK), :])
    v = i_ref[0, pl.ds(r0, CHUNK), :]
    logf = jnp.log(lb + (1.0 - lb) * jax.nn.sigmoid(z))
    k = (1.0 - lb) * jax.nn.sigmoid(-z)
    l1, l2, l3 = _split3(logf)
    cum = (jnp.dot(tri, l1, preferred_element_type=F32)
           + jnp.dot(tri, l2, preferred_element_type=F32)
           + jnp.dot(tri, l3, preferred_element_type=F32))
    c_mid = cum[i_mid:i_mid + 1]
    c_end = cum[i_end:i_end + 1]
    qr = (q * jnp.exp(cum - c_mid)).astype(BF16)
    kr = (k * jnp.exp(c_mid - cum)).astype(BF16)
    scores = jnp.where(keep, _dot_nt(qr, kr), 0.0)
    vb = v.astype(BF16)
    o_intra = jnp.dot(scores.astype(BF16), vb, preferred_element_type=F32)
    o_inter = _dot_nt((q * jnp.exp(cum)).astype(BF16), st.astype(BF16))
    o_ref[0, pl.ds(r0, CHUNK), :] = o_intra + o_inter
    ku = (k * jnp.exp(c_end - cum)).astype(BF16)
    return st * jnp.exp(c_end) + _dot_tn(vb, ku)


def _hgrn_kernel(qf_ref, ff_ref, if_ref, qb_ref, fb_ref, ib_ref, lbf_ref, lbb_ref, of_ref, ob_ref,
                 stf_ref, stb_ref, *, lblk):
    @pl.when(pl.program_id(2) == 0)
    def _():
        stf_ref[...] = jnp.zeros_like(stf_ref)
        stb_ref[...] = jnp.zeros_like(stb_ref)

    lb_f = _lower_bound(lbf_ref[...])
    lb_b = _lower_bound(lbb_ref[...])
    nch = lblk // CHUNK

    def body(ci, carry):
        st_f, st_b = carry
        st_f = _hgrn_chunk(qf_ref, ff_ref, if_ref, of_ref, lb_f, st_f, ci, False)
        st_b = _hgrn_chunk(qb_ref, fb_ref, ib_ref, ob_ref, lb_b, st_b, nch - 1 - ci, True)
        return st_f, st_b

    st_f, st_b = lax.fori_loop(0, nch, body, (stf_ref[...], stb_ref[...]))
    stf_ref[...] = st_f
    stb_ref[...] = st_b


def _hgrn(proj, lb_gamma_fwd, lb_gamma_bwd):
    nb, L, _ = proj.shape
    nh = lb_gamma_fwd.shape[1] // HEAD_DIM
    lblk = min(1024, L)
    nblk = L // lblk

    def fwd(c0):
        return pl.BlockSpec((1, lblk, HEAD_DIM), lambda b, h, j: (b, j, c0 + h))

    def bwd(c0):
        return pl.BlockSpec((1, lblk, HEAD_DIM), lambda b, h, j: (b, nblk - 1 - j, c0 + h))

    lbs = pl.BlockSpec((lb_gamma_fwd.shape[0], HEAD_DIM), lambda b, h, j: (0, h))
    out = jax.ShapeDtypeStruct((nb, L, nh * HEAD_DIM), F32)
    return pl.pallas_call(
        functools.partial(_hgrn_kernel, lblk=lblk),
        grid=(nb, nh, nblk),
        in_specs=[fwd(0), fwd(nh), fwd(3 * nh), bwd(0), bwd(2 * nh), bwd(3 * nh), lbs, lbs],
        out_specs=[fwd(0), bwd(0)],
        out_shape=[out, out],
        scratch_shapes=[pltpu.VMEM((HEAD_DIM, HEAD_DIM), F32), pltpu.VMEM((HEAD_DIM, HEAD_DIM), F32)],
        compiler_params=_cparams(("arbitrary", "arbitrary", "arbitrary")),
        name="hgrn_bidir",
    )(proj, proj, proj, proj, proj, proj, lb_gamma_fwd, lb_gamma_bwd)


CONV_HALO = 16
CONV_ROWS = 64


def _conv_kernel(a_ref, g_ref, ap_ref, gp_ref, an_ref, gn_ref, w_ref, b_ref, lg_ref, lb_ref,
                 o_ref, hbuf, *, tl):
    j = pl.program_id(1)
    last = pl.num_programs(1) - 1
    hbuf[CONV_HALO:CONV_HALO + tl, :] = a_ref[0] * jax.nn.sigmoid(g_ref[0])
    hp = ap_ref[0] * jax.nn.sigmoid(gp_ref[0])
    hbuf[0:CONV_HALO, :] = jnp.where(j > 0, hp, 0.0)
    hn = an_ref[0] * jax.nn.sigmoid(gn_ref[0])
    hbuf[CONV_HALO + tl:2 * CONV_HALO + tl, :] = jnp.where(j < last, hn, 0.0)
    off = CONV_HALO - CONV_PAD
    for r in range(tl // CONV_ROWS):
        acc = jnp.zeros((CONV_ROWS, a_ref.shape[2]), F32)
        for k in range(CONV_WIDTH):
            s = r * CONV_ROWS + k + off
            acc = acc + w_ref[k:k + 1, :] * hbuf[s:s + CONV_ROWS, :]
        acc = acc + b_ref[...]
        mu = jnp.mean(acc, axis=-1, keepdims=True)
        cen = acc - mu
        var = jnp.mean(cen * cen, axis=-1, keepdims=True)
        y = cen * lax.rsqrt(var + EPS) * lg_ref[...] + lb_ref[...]
        o_ref[0, r * CONV_ROWS:(r + 1) * CONV_ROWS, :] = _silu(y)


def _conv(proj, w_pad, bias, ln_g, ln_b, a_col):
    nb, L, _ = proj.shape
    dc = w_pad.shape[1]
    tl = 256
    hb = tl // CONV_HALO
    nhalo = L // CONV_HALO

    def cur(c):
        return pl.BlockSpec((1, tl, dc), lambda b, j: (b, j, c))

    def prev(c):
        return pl.BlockSpec((1, CONV_HALO, dc), lambda b, j: (b, jnp.maximum(j * hb - 1, 0), c))

    def nxt(c):
        return pl.BlockSpec((1, CONV_HALO, dc),
                            lambda b, j: (b, jnp.minimum((j + 1) * hb, nhalo - 1), c))

    vec = pl.BlockSpec((1, dc), lambda b, j: (0, 0))
    return pl.pallas_call(
        functools.partial(_conv_kernel, tl=tl),
        grid=(nb, L // tl),
        in_specs=[cur(a_col), cur(a_col + 1), prev(a_col), prev(a_col + 1), nxt(a_col),
                  nxt(a_col + 1), pl.BlockSpec(w_pad.shape, lambda b, j: (0, 0)), vec, vec, vec],
        out_specs=pl.BlockSpec((1, tl, dc), lambda b, j: (b, j, 0)),
        out_shape=jax.ShapeDtypeStruct((nb, L, dc), F32),
        scratch_shapes=[pltpu.VMEM((tl + 2 * CONV_HALO, dc), F32)],
        compiler_params=_cparams(("arbitrary", "arbitrary")),
        name="conv_group",
    )(proj, proj, proj, proj, proj, proj, w_pad, bias.reshape(1, dc), ln_g.reshape(1, dc),
      ln_b.reshape(1, dc))


def _mix_kernel(of_ref, ob_ref, gr_ref, oc_ref, x_ref, hg_ref, g1_ref, sh2_ref, sc2_ref, ng_ref,
                wo_ref, wq_ref, k1_ref, k2_ref, h1_ref, hp_ref, s_ref):
    o = of_ref[0] + ob_ref[0]
    parts = []
    for hh in range(o.shape[1] // HEAD_DIM):
        oh = o[:, hh * HEAD_DIM:(hh + 1) * HEAD_DIM]
        parts.append(oh * lax.rsqrt(jnp.mean(oh * oh, axis=-1, keepdims=True) + EPS))
    on = jnp.concatenate(parts, axis=-1) * hg_ref[...] * _silu(gr_ref[0])
    cat = jnp.concatenate([on, oc_ref[0]], axis=-1).astype(BF16)
    mix = jnp.dot(cat, wo_ref[...], preferred_element_type=F32)
    h1 = x_ref[0] + g1_ref[0] * mix
    h1_ref[0] = h1
    hn2 = _rms_mod(h1, ng_ref[...], sh2_ref[0], sc2_ref[0])
    hp_ref[0] = _pack_words(hn2)
    q = jnp.dot(hn2.astype(BF16), wq_ref[...], preferred_element_type=F32)
    k1h, k1l, _ = _split3(k1_ref[...])
    k2h, k2l, _ = _split3(k2_ref[...])
    for hd in range(PEER_HEADS):
        for half, (kh, kl) in enumerate(((k1h, k1l), (k2h, k2l))):
            c0 = hd * 2 * PEER_HALF + half * PEER_HALF
            qh, ql, _ = _split3(q[:, c0:c0 + PEER_HALF])
            s_ref[2 * hd + half] = _dot_nt(kh, qh) + _dot_nt(kh, ql) + _dot_nt(kl, qh)


def _mix(o_f, o_b, proj, o_c, x, hgrn_g, mod3, norm_g, w_out_bf16, wq_bf16, keys1, keys2, g_col):
    nb, L, d = x.shape
    dh = o_f.shape[2]
    dq = wq_bf16.shape[1]
    tt = 256
    nj = L // tt

    def half(c=0):
        return pl.BlockSpec((1, tt, dh), lambda b, j: (b, j, c))

    def full():
        return pl.BlockSpec((1, tt, d), lambda b, j: (b, j, 0))

    def modc(c):
        return pl.BlockSpec((1, 1, d), lambda b, j: (b, 0, c))

    def const(shape):
        return pl.BlockSpec(shape, lambda b, j: (0,) * len(shape))

    return pl.pallas_call(
        _mix_kernel,
        grid=(nb, nj),
        in_specs=[half(), half(), half(g_col), half(), full(), const((1, dh)),
                  modc(2), modc(3), modc(4), const((1, d)), const((d, d)), const((d, dq)),
                  const(keys1.shape), const(keys2.shape)],
        out_specs=[full(), pl.BlockSpec((1, tt, d // 2), lambda b, j: (b, j, 0)),
                   pl.BlockSpec((2 * PEER_HEADS, PEER_KEYS, tt), lambda b, j: (0, 0, b * nj + j))],
        out_shape=[jax.ShapeDtypeStruct((nb, L, d), F32), jax.ShapeDtypeStruct((nb, L, d // 2), I32),
                   jax.ShapeDtypeStruct((2 * PEER_HEADS, PEER_KEYS, nb * L), F32)],
        compiler_params=_cparams(("arbitrary", "arbitrary")),
        name="mix_scores",
    )(o_f, o_b, proj, o_c, x, hgrn_g.reshape(1, dh), mod3, mod3, mod3, norm_g.reshape(1, d),
      w_out_bf16, wq_bf16, keys1, keys2)


def _top16(s, payload=None):
    n = s.shape[0]
    iota = lax.broadcasted_iota(I32, s.shape, 0).astype(F32)
    vals, tags = [], []
    for _ in range(PEER_TOPK):
        m = jnp.max(s, axis=0, keepdims=True)
        idx = jnp.min(jnp.where(s == m, iota, float(n)), axis=0, keepdims=True)
        hit = iota == idx
        vals.append(m)
        tags.append(idx if payload is None else jnp.max(jnp.where(hit, payload, -1.0), axis=0, keepdims=True))
        s = jnp.where(hit, -jnp.inf, s)
    return jnp.concatenate(vals, axis=0), jnp.concatenate(tags, axis=0)


def _pruned_pairs(x1, x2, combine):
    rows = [combine(x1[a:a + 1], x2[0:PEER_TOPK // (a + 1)]) for a in range(PEER_TOPK // 2)]
    rows.append(combine(x1[PEER_TOPK // 2:], x2[0:1]))
    return jnp.concatenate(rows, axis=0)


def _topk_kernel(s_ref, e_ref, w_ref):
    es, ws = [], []
    for hd in range(PEER_HEADS):
        v1, i1 = _top16(s_ref[2 * hd])
        v2, i2 = _top16(s_ref[2 * hd + 1])
        cand = _pruned_pairs(v1, v2, lambda a, b: a + b)
        cand_e = _pruned_pairs(i1, i2, lambda a, b: a * float(PEER_KEYS) + b)
        sc, e = _top16(cand, cand_e)
        p = jnp.exp(sc - sc[0:1])
        es.append(e.astype(I32))
        ws.append(p / jnp.sum(p, axis=0, keepdims=True))
    e_ref[...] = jnp.concatenate(es, axis=0).T
    w_ref[...] = jnp.concatenate(ws, axis=0).T


def _topk(scores):
    npair, nk, T = scores.shape
    nsel = PEER_HEADS * PEER_TOPK
    tt = 256
    return pl.pallas_call(
        _topk_kernel,
        grid=(T // tt,),
        in_specs=[pl.BlockSpec((npair, nk, tt), lambda i: (0, 0, i))],
        out_specs=[pl.BlockSpec((tt, nsel), lambda i: (i, 0)),
                   pl.BlockSpec((tt, nsel), lambda i: (i, 0))],
        out_shape=[jax.ShapeDtypeStruct((T, nsel), I32), jax.ShapeDtypeStruct((T, nsel), F32)],
        compiler_params=_cparams(("arbitrary",)),
        name="peer_topk",
    )(scores)


WORDS = 4


def _pack_words(x):
    bits = pltpu.bitcast(x.astype(BF16).astype(F32), I32)
    out = []
    for c in range(WORDS):
        lo = bits[:, (2 * c) * LANES:(2 * c + 1) * LANES]
        hi = bits[:, (2 * c + 1) * LANES:(2 * c + 2) * LANES]
        out.append(lax.shift_right_logical(lo, 16) | (hi & jnp.int32(-65536)))
    return jnp.concatenate(out, axis=-1)


def _pack_kernel(x_ref, o_ref):
    o_ref[...] = _pack_words(x_ref[...])


def _pack_table(x):
    ne, d = x.shape
    te = 512
    packed = pl.pallas_call(
        _pack_kernel,
        grid=(ne // te,),
        in_specs=[pl.BlockSpec((te, d), lambda i: (i, 0))],
        out_specs=pl.BlockSpec((te, d // 2), lambda i: (i, 0)),
        out_shape=jax.ShapeDtypeStruct((ne, d // 2), I32),
        compiler_params=_cparams(("arbitrary",)),
        name="pack_table",
    )(x)
    return packed.reshape(ne, WORDS, LANES)


NSEL = PEER_HEADS * PEER_TOPK
ROW_TILE = 2 * WORDS


SC_LANES = 16
SC_ROWS = 32
SC_BUFS = 4
SC_TOK = 16
SC_GROUP = 8
VS_GROUP = 2
PART_ROWS = NSEL * SC_LANES // LANES


def _sc_unpack(w):
    lo = plsc.bitcast(lax.shift_left(w, jnp.int32(16)), F32)
    hi = plsc.bitcast(w & jnp.int32(-65536), F32)
    return lo, hi


def _sc_mesh():
    return plsc.VectorSubcoreMesh(core_axis_name="c", subcore_axis_name="s")


def _sc_params():
    return pltpu.CompilerParams(use_tc_tiling_on_sc=True, needs_layout_passes=False)


def _sc_udot(table3, idx, hn3):
    T = hn3.shape[0]
    info = plsc.get_sparse_core_info()
    nw = info.num_cores * info.num_subcores
    tpw = T // nw
    assert T % nw == 0 and tpw % SC_TOK == 0 and NSEL % SC_ROWS == 0
    nchunk = tpw // SC_TOK
    qsteps = NSEL // SC_ROWS
    nstep = SC_TOK * qsteps

    @functools.partial(
        pl.kernel, mesh=_sc_mesh(),
        out_type=jax.ShapeDtypeStruct((T, PART_ROWS, LANES), F32),
        scratch_types=[pltpu.VMEM((SC_TOK * NSEL,), I32),
                       pltpu.VMEM((SC_BUFS, SC_ROWS, WORDS, LANES), I32),
                       pltpu.VMEM((SC_TOK, WORDS * LANES), I32),
                       pltpu.VMEM((SC_TOK, PART_ROWS, LANES), F32),
                       pltpu.SemaphoreType.DMA((SC_BUFS,))],
        compiler_params=_sc_params(),
        name="sc_udot",
    )
    def k(table_hbm, idx_hbm, hn_hbm, out_hbm, idx_v, rows_v, h_v, p_v, sem_g):
        wid = lax.axis_index("s") * info.num_cores + lax.axis_index("c")
        tbase = wid * tpw

        def gather(st, slot):
            return pltpu.make_async_copy(table_hbm.at[idx_v.at[pl.ds(st * SC_ROWS, SC_ROWS)]],
                                         rows_v.at[slot], sem_g.at[slot])

        def compute(tk, q, slot):
            @pl.loop(0, SC_ROWS // SC_GROUP)
            def _(g):
                acc = [jnp.zeros((SC_LANES,), F32) for _ in range(SC_GROUP)]
                for c in range(WORDS):
                    for lv in range(0, LANES // SC_LANES, 2):
                        ls = [pl.ds((lv + d) * SC_LANES, SC_LANES) for d in range(2)]
                        hb = [plsc.bitcast(h_v[tk, pl.ds(c * LANES + (lv + d) * SC_LANES, SC_LANES)], BF16)
                              for d in range(2)]
                        for i in range(SC_GROUP):
                            pr = [plsc.bitcast(rows_v[slot, g * SC_GROUP + i, c, ls[d]], BF16) * hb[d]
                                  for d in range(2)]
                            lo, hi = _sc_unpack(plsc.bitcast(pr[0] + pr[1], I32))
                            acc[i] = acc[i] + lo + hi
                for i in range(SC_GROUP):
                    p_v[tk, q * (SC_ROWS // SC_GROUP) + g, pl.ds(i * SC_LANES, SC_LANES)] = acc[i]

        @pl.loop(0, nchunk)
        def _(ch):
            t0 = tbase + ch * SC_TOK
            pltpu.sync_copy(idx_hbm.at[pl.ds(t0 * NSEL, SC_TOK * NSEL)], idx_v)
            pltpu.sync_copy(hn_hbm.at[pl.ds(t0, SC_TOK)], h_v)
            for s in range(SC_BUFS - 1):
                gather(s, s).start()

            @pl.loop(0, SC_TOK)
            def _(tk):
                for q in range(qsteps):
                    st = tk * qsteps + q
                    slot = q % SC_BUFS

                    @pl.when(st + SC_BUFS - 1 < nstep)
                    def _():
                        gather(st + SC_BUFS - 1, (q + SC_BUFS - 1) % SC_BUFS).start()

                    gather(st, slot).wait()
                    compute(tk, q, slot)

            pltpu.sync_copy(p_v, out_hbm.at[pl.ds(t0, SC_TOK)])

    return k(table3, idx, hn3)


def _sc_vsum(table3, idx, coefx):
    assert VS_GROUP == 2, "the loop body sums the products of exactly two experts in bf16"
    T = coefx.shape[0]
    info = plsc.get_sparse_core_info()
    nw = info.num_cores * info.num_subcores
    tpw = T // nw
    assert T % nw == 0 and tpw % SC_TOK == 0 and NSEL % SC_ROWS == 0 and SC_GROUP % VS_GROUP == 0
    nchunk = tpw // SC_TOK
    qsteps = NSEL // SC_ROWS
    nstep = SC_TOK * qsteps
    nlv = LANES // SC_LANES
    sub = SC_GROUP // VS_GROUP

    @functools.partial(
        pl.kernel, mesh=_sc_mesh(),
        out_type=jax.ShapeDtypeStruct((T, ROW_TILE * LANES), F32),
        scratch_types=[pltpu.VMEM((SC_TOK * NSEL,), I32),
                       pltpu.VMEM((SC_BUFS, SC_ROWS, WORDS, LANES), I32),
                       pltpu.VMEM((SC_TOK, PART_ROWS, LANES), I32),
                       pltpu.VMEM((SC_TOK, ROW_TILE * LANES), F32),
                       pltpu.SemaphoreType.DMA((SC_BUFS,))],
        compiler_params=_sc_params(),
        name="sc_vsum",
    )
    def k(table_hbm, idx_hbm, coef_hbm, out_hbm, idx_v, rows_v, c_v, y_v, sem_g):
        wid = lax.axis_index("s") * info.num_cores + lax.axis_index("c")
        tbase = wid * tpw

        def gather(st, slot):
            return pltpu.make_async_copy(table_hbm.at[idx_v.at[pl.ds(st * SC_ROWS, SC_ROWS)]],
                                         rows_v.at[slot], sem_g.at[slot])

        def compute(tk, q, slot):
            for c in range(WORDS):
                if q == 0:
                    init = tuple(jnp.zeros((SC_LANES,), F32) for _ in range(2 * nlv))
                else:
                    init = tuple(y_v[tk, pl.ds((2 * c + p) * LANES + lv * SC_LANES, SC_LANES)]
                                 for p in range(2) for lv in range(nlv))

                def body(g, acc):
                    acc = list(acc)
                    cb = [plsc.bitcast(c_v[tk, q * (SC_ROWS // SC_GROUP) + g // sub,
                                           pl.ds(((g % sub) * VS_GROUP + i) * SC_LANES, SC_LANES)], BF16)
                          for i in range(VS_GROUP)]
                    for lv in range(nlv):
                        pr = [cb[i] * plsc.bitcast(rows_v[slot, g * VS_GROUP + i, c,
                                                          pl.ds(lv * SC_LANES, SC_LANES)], BF16)
                              for i in range(VS_GROUP)]
                        lo, hi = _sc_unpack(plsc.bitcast(pr[0] + pr[1], I32))
                        acc[lv] = acc[lv] + lo
                        acc[nlv + lv] = acc[nlv + lv] + hi
                    return tuple(acc)

                acc = lax.fori_loop(0, SC_ROWS // VS_GROUP, body, init)
                for p in range(2):
                    for lv in range(nlv):
                        y_v[tk, pl.ds((2 * c + p) * LANES + lv * SC_LANES, SC_LANES)] = acc[p * nlv + lv]

        @pl.loop(0, nchunk)
        def _(ch):
            t0 = tbase + ch * SC_TOK
            pltpu.sync_copy(idx_hbm.at[pl.ds(t0 * NSEL, SC_TOK * NSEL)], idx_v)
            pltpu.sync_copy(coef_hbm.at[pl.ds(t0, SC_TOK)], c_v)
            for s in range(SC_BUFS - 1):
                gather(s, s).start()

            @pl.loop(0, SC_TOK)
            def _(tk):
                for q in range(qsteps):
                    st = tk * qsteps + q
                    slot = q % SC_BUFS

                    @pl.when(st + SC_BUFS - 1 < nstep)
                    def _():
                        gather(st + SC_BUFS - 1, (q + SC_BUFS - 1) % SC_BUFS).start()

                    gather(st, slot).wait()
                    compute(tk, q, slot)

            pltpu.sync_copy(y_v, out_hbm.at[pl.ds(t0, SC_TOK)])

    return k(table3, idx, coefx)


def _coef_kernel(p_ref, w_ref, cx_ref, *, tt):
    row = lax.broadcasted_iota(I32, (LANES, NSEL), 0)
    col = lax.broadcasted_iota(I32, (LANES, NSEL), 1)
    per_row = LANES // SC_LANES
    act = jnp.zeros((tt, NSEL), F32)
    for s in range(PART_ROWS):
        fold = (col == per_row * s + row // SC_LANES).astype(BF16)
        p1, p2, p3 = _split3(p_ref[pl.ds(s, tt, stride=PART_ROWS), :])
        act = act + (jnp.dot(p1, fold, preferred_element_type=F32) + jnp.dot(p2, fold, preferred_element_type=F32)
                     + jnp.dot(p3, fold, preferred_element_type=F32))
    coef = w_ref[...] * (0.5 * act * (1.0 + lax.erf(act * (2.0 ** -0.5))))
    cb = coef.astype(BF16)
    for s in range(PART_ROWS):
        spread = (row == per_row * s + col // SC_LANES).astype(BF16)
        bits = pltpu.bitcast(jnp.dot(cb, spread, preferred_element_type=F32), I32)
        cx_ref[pl.ds(s, tt, stride=PART_ROWS), :] = bits | lax.shift_right_logical(bits, 16)


def _coef(p2d, w):
    T = w.shape[0]
    tt = 256
    return pl.pallas_call(
        functools.partial(_coef_kernel, tt=tt),
        grid=(T // tt,),
        in_specs=[pl.BlockSpec((tt * PART_ROWS, LANES), lambda i: (i, 0)),
                  pl.BlockSpec((tt, NSEL), lambda i: (i, 0))],
        out_specs=pl.BlockSpec((tt * PART_ROWS, LANES), lambda i: (i, 0)),
        out_shape=jax.ShapeDtypeStruct((T * PART_ROWS, LANES), I32),
        compiler_params=_cparams(("arbitrary",)),
        name="peer_coef",
    )(p2d, w)


def _experts(e, u_rows, v_rows, hn2, w):
    T = hn2.shape[0]
    idx = e.reshape(T * NSEL)
    p = _sc_udot(u_rows, idx, hn2)
    cx = _coef(p.reshape(T * PART_ROWS, LANES), w)
    return _sc_vsum(v_rows, idx, cx.reshape(T, PART_ROWS, LANES))


def _final_kernel(h1_ref, y_ref, g2_ref, fg_ref, fsh_ref, fsc_ref, o_ref):
    h = h1_ref[0] + g2_ref[0] * y_ref[0]
    o_ref[0] = _rms_mod(h, fg_ref[...], fsh_ref[0], fsc_ref[0])


def _final(h1, y, mod3, fmod3, final_g):
    nb, L, d = h1.shape
    tt = 512
    blk = pl.BlockSpec((1, tt, d), lambda b, j: (b, j, 0))
    return pl.pallas_call(
        _final_kernel,
        grid=(nb, L // tt),
        in_specs=[blk, blk, pl.BlockSpec((1, 1, d), lambda b, j: (b, 0, 5)),
                  pl.BlockSpec((1, d), lambda b, j: (0, 0)),
                  pl.BlockSpec((1, 1, d), lambda b, j: (b, 0, 0)),
                  pl.BlockSpec((1, 1, d), lambda b, j: (b, 0, 1))],
        out_specs=blk,
        out_shape=jax.ShapeDtypeStruct((nb, L, d), F32),
        compiler_params=_cparams(("arbitrary", "arbitrary")),
        name="final_norm",
    )(h1, y, mod3, final_g.reshape(1, d), fmod3, fmod3)


def kernel(x, c, ada_w, ada_b, norm_mix_g, w_in, lb_gamma_fwd, lb_gamma_bwd, hgrn_norm_g, conv_w,
           conv_b, conv_ln_g, conv_ln_b, w_out, norm_ffn_g, peer_wq, peer_keys1, peer_keys2, peer_u,
           peer_v, final_ada_w, final_ada_b, final_norm_g):
    nb, L, d = x.shape
    assert ada_w.shape[0] == 1, "single-layer trunk"
    d_hgrn = lb_gamma_fwd.shape[1]
    nh = d_hgrn // HEAD_DIM

    c_pad = jnp.pad(c, ((0, SUBLANES - nb), (0, 0)))
    mod3 = _ada(c_pad, ada_w[0], ada_b[0])[:nb].reshape(nb, 1, 6 * d)
    fmod3 = _ada(c_pad, final_ada_w, final_ada_b)[:nb].reshape(nb, 1, 2 * d)

    d_conv = conv_w.shape[2]
    w_pad = jnp.pad(conv_w[0], ((0, 1), (0, 0)))
    w_in_bf, w_out_bf, wq_bf = w_in[0].astype(BF16), w_out[0].astype(BF16), peer_wq[0].astype(BF16)
    u_rows, v_rows = _pack_table(peer_u[0]), _pack_table(peer_v[0])

    outs = []
    xb, u_rows, v_rows = lax.optimization_barrier((x[0:1], u_rows, v_rows))
    for b in range(nb):
        mb, fb = mod3[b:b + 1], fmod3[b:b + 1]
        proj = _inproj(xb, norm_mix_g[0], mb, w_in_bf)
        o_f, o_b = _hgrn(proj, lb_gamma_fwd, lb_gamma_bwd)
        o_c = _conv(proj, w_pad, conv_b[0], conv_ln_g[0], conv_ln_b[0], a_col=5 * d_hgrn // d_conv)
        h1, hn2, scores = _mix(o_f, o_b, proj, o_c, xb, hgrn_norm_g[0], mb, norm_ffn_g[0], w_out_bf,
                               wq_bf, peer_keys1[0], peer_keys2[0], g_col=4)
        e, w = _topk(scores)
        if b + 1 < nb:
            e, w, xb = lax.optimization_barrier((e, w, x[b + 1:b + 2]))
        y = _experts(e, u_rows, v_rows, hn2.reshape(L, d // 2), w)
        outs.append(_final(h1, y.reshape(1, L, d), mb, fb, final_norm_g))
    return jnp.concatenate(outs, axis=0)
```

```python
import functools

import jax
import jax.numpy as jnp
from jax import lax
from jax.experimental import pallas as pl
from jax.experimental.pallas import tpu as pltpu
from jax.experimental.pallas import tpu_sc as plsc

F32 = jnp.float32
BF16 = jnp.bfloat16
I32 = jnp.int32

EPS = 1e-6
HEAD_DIM = 128
CHUNK = 64
CONV_WIDTH = 31
CONV_PAD = CONV_WIDTH // 2
PEER_HEADS = 8
PEER_KEYS = 128
PEER_TOPK = 16
PEER_HALF = 128
LANES = 128
SUBLANES = 8
VMEM_LIMIT = 48 * 1024 * 1024


def _cparams(sem):
    return pltpu.CompilerParams(dimension_semantics=sem, vmem_limit_bytes=VMEM_LIMIT)


def _silu(x):
    return x * jax.nn.sigmoid(x)


def _split3(x):
    x1 = x.astype(BF16)
    r1 = x - x1.astype(F32)
    x2 = r1.astype(BF16)
    x3 = (r1 - x2.astype(F32)).astype(BF16)
    return x1, x2, x3


def _dot_nt(a, b):
    return lax.dot_general(a, b, (((1,), (1,)), ((), ())), preferred_element_type=F32)


def _dot_tn(a, b):
    return lax.dot_general(a, b, (((0,), (0,)), ((), ())), preferred_element_type=F32)


def _ada_kernel(c_ref, w_ref, b_ref, o_ref):
    ca = _silu(c_ref[...])
    o_ref[...] = jnp.dot(ca, w_ref[...], preferred_element_type=F32,
                         precision=lax.Precision.HIGHEST) + b_ref[...]


def _ada(c_pad, w, b):
    d, n = w.shape
    tn = 1024
    return pl.pallas_call(
        _ada_kernel,
        grid=(n // tn,),
        in_specs=[pl.BlockSpec((SUBLANES, d), lambda j: (0, 0)),
                  pl.BlockSpec((d, tn), lambda j: (0, j)),
                  pl.BlockSpec((1, tn), lambda j: (0, j))],
        out_specs=pl.BlockSpec((SUBLANES, tn), lambda j: (0, j)),
        out_shape=jax.ShapeDtypeStruct((SUBLANES, n), F32),
        compiler_params=_cparams(("arbitrary",)),
        name="ada_proj",
    )(c_pad, w, b.reshape(1, n))


def _rms_mod(x, g, sh, sc):
    y = x * lax.rsqrt(jnp.mean(x * x, axis=-1, keepdims=True) + EPS) * g
    return y * (1.0 + sc) + sh


def _inproj_kernel(x_ref, g_ref, sh_ref, sc_ref, w_ref, o_ref):
    hn = _rms_mod(x_ref[0], g_ref[...], sh_ref[0], sc_ref[0])
    o_ref[0] = jnp.dot(hn.astype(BF16), w_ref[...], preferred_element_type=F32)


def _inproj(x, b0, g, mod3, w_bf16):
    _, L, d = x.shape
    n = w_bf16.shape[1]
    tt = 512
    return pl.pallas_call(
        _inproj_kernel,
        grid=(1, L // tt),
        in_specs=[pl.BlockSpec((1, tt, d), lambda b, j: (b0, j, 0)),
                  pl.BlockSpec((1, d), lambda b, j: (0, 0)),
                  pl.BlockSpec((1, 1, d), lambda b, j: (0, 0, 0)),
                  pl.BlockSpec((1, 1, d), lambda b, j: (0, 0, 1)),
                  pl.BlockSpec((d, n), lambda b, j: (0, 0))],
        out_specs=pl.BlockSpec((1, tt, n), lambda b, j: (0, j, 0)),
        out_shape=jax.ShapeDtypeStruct((1, L, n), F32),
        compiler_params=_cparams(("arbitrary", "arbitrary")),
        name="in_proj",
    )(x, g.reshape(1, d), mod3, mod3, w_bf16)


def _lower_bound(lbg):
    ex = jnp.exp(lbg - jnp.max(lbg, axis=0, keepdims=True))
    return ex[0:1] / jnp.sum(ex, axis=0, keepdims=True)


def _hgrn_chunk(q_ref, f_ref, i_ref, o_ref, lb, st, c, reverse):
    row = lax.broadcasted_iota(I32, (CHUNK, CHUNK), 0)
    col = lax.broadcasted_iota(I32, (CHUNK, CHUNK), 1)
    keep = (col >= row) if reverse else (col <= row)
    tri = keep.astype(BF16)
    i_mid = CHUNK // 2 if reverse else CHUNK // 2 - 1
    i_end = 0 if reverse else CHUNK - 1
    r0 = pl.multiple_of(c * CHUNK, CHUNK)
    z = f_ref[0, pl.ds(r0, CHUNK), :]
    q = _silu(q_ref[0, pl.ds(r0, CHUNK), :])
    v = i_ref[0, pl.ds(r0, CHUNK), :]
    logf = jnp.log(lb + (1.0 - lb) * jax.nn.sigmoid(z))
    k = (1.0 - lb) * jax.nn.sigmoid(-z)
    l1, l2, l3 = _split3(logf)
    cum = (jnp.dot(tri, l1, preferred_element_type=F32)
           + jnp.dot(tri, l2, preferred_element_type=F32)
           + jnp.dot(tri, l3, preferred_element_type=F32))
    c_mid = cum[i_mid:i_mid + 1]
    c_end = cum[i_end:i_end + 1]
    qr = (q * jnp.exp(cum - c_mid)).astype(BF16)
    kr = (k * jnp.exp(c_mid - cum)).astype(BF16)
    scores = jnp.where(keep, _dot_nt(qr, kr), 0.0)
    vb = v.astype(BF16)
    o_intra = jnp.dot(scores.astype(BF16), vb, preferred_element_type=F32)
    o_inter = _dot_nt((q * jnp.exp(cum)).astype(BF16), st.astype(BF16))
    o_ref[0, pl.ds(r0, CHUNK), :] = o_intra + o_inter
    ku = (k * jnp.exp(c_end - cum)).astype(BF16)
    return st * jnp.exp(c_end) + _dot_tn(vb, ku)


def _hgrn_kernel(qf_ref, ff_ref, if_ref, qb_ref, fb_ref, ib_ref, lbf_ref, lbb_ref, of_ref, ob_ref,
                 stf_ref, stb_ref, *, lblk):
    @pl.when(pl.program_id(2) == 0)
    def _():
        stf_ref[...] = jnp.zeros_like(stf_ref)
        stb_ref[...] = jnp.zeros_like(stb_ref)

    lb_f = _lower_bound(lbf_ref[...])
    lb_b = _lower_bound(lbb_ref[...])
    nch = lblk // CHUNK

    def body(ci, carry):
        st_f, st_b = carry
        st_f = _hgrn_chunk(qf_ref, ff_ref, if_ref, of_ref, lb_f, st_f, ci, False)
        st_b = _hgrn_chunk(qb_ref, fb_ref, ib_ref, ob_ref, lb_b, st_b, nch - 1 - ci, True)
        return st_f, st_b

    st_f, st_b = lax.fori_loop(0, nch, body, (stf_ref[...], stb_ref[...]))
    stf_ref[...] = st_f
    stb_ref[...] = st_b


def _hgrn(proj, lb_gamma_fwd, lb_gamma_bwd):
    nb, L, _ = proj.shape
    nh = lb_gamma_fwd.shape[1] // HEAD_DIM
    lblk = min(1024, L)
    nblk = L // lblk

    def fwd(c0):
        return pl.BlockSpec((1, lblk, HEAD_DIM), lambda b, h, j: (b, j, c0 + h))

    def bwd(c0):
        return pl.BlockSpec((1, lblk, HEAD_DIM), lambda b, h, j: (b, nblk - 1 - j, c0 + h))

    lbs = pl.BlockSpec((lb_gamma_fwd.shape[0], HEAD_DIM), lambda b, h, j: (0, h))
    out = jax.ShapeDtypeStruct((nb, L, nh * HEAD_DIM), F32)
    return pl.pallas_call(
        functools.partial(_hgrn_kernel, lblk=lblk),
        grid=(nb, nh, nblk),
        in_specs=[fwd(0), fwd(nh), fwd(3 * nh), bwd(0), bwd(2 * nh), bwd(3 * nh), lbs, lbs],
        out_specs=[fwd(0), bwd(0)],
        out_shape=[out, out],
        scratch_shapes=[pltpu.VMEM((HEAD_DIM, HEAD_DIM), F32), pltpu.VMEM((HEAD_DIM, HEAD_DIM), F32)],
        compiler_params=_cparams(("arbitrary", "arbitrary", "arbitrary")),
        name="hgrn_bidir",
    )(proj, proj, proj, proj, proj, proj, lb_gamma_fwd, lb_gamma_bwd)


CONV_HALO = 16
CONV_ROWS = 64


def _conv_kernel(a_ref, g_ref, ap_ref, gp_ref, an_ref, gn_ref, w_ref, b_ref, lg_ref, lb_ref,
                 o_ref, hbuf, *, tl):
    j = pl.program_id(1)
    last = pl.num_programs(1) - 1
    hbuf[CONV_HALO:CONV_HALO + tl, :] = a_ref[0] * jax.nn.sigmoid(g_ref[0])
    hp = ap_ref[0] * jax.nn.sigmoid(gp_ref[0])
    hbuf[0:CONV_HALO, :] = jnp.where(j > 0, hp, 0.0)
    hn = an_ref[0] * jax.nn.sigmoid(gn_ref[0])
    hbuf[CONV_HALO + tl:2 * CONV_HALO + tl, :] = jnp.where(j < last, hn, 0.0)
    off = CONV_HALO - CONV_PAD
    for r in range(tl // CONV_ROWS):
        acc = jnp.zeros((CONV_ROWS, a_ref.shape[2]), F32)
        for k in range(CONV_WIDTH):
            s = r * CONV_ROWS + k + off
            acc = acc + w_ref[k:k + 1, :] * hbuf[s:s + CONV_ROWS, :]
        acc = acc + b_ref[...]
        mu = jnp.mean(acc, axis=-1, keepdims=True)
        cen = acc - mu
        var = jnp.mean(cen * cen, axis=-1, keepdims=True)
        y = cen * lax.rsqrt(var + EPS) * lg_ref[...] + lb_ref[...]
        o_ref[0, r * CONV_ROWS:(r + 1) * CONV_ROWS, :] = _silu(y)


def _conv(proj, w_pad, bias, ln_g, ln_b, a_col):
    nb, L, _ = proj.shape
    dc = w_pad.shape[1]
    tl = 256
    hb = tl // CONV_HALO
    nhalo = L // CONV_HALO

    def cur(c):
        return pl.BlockSpec((1, tl, dc), lambda b, j: (b, j, c))

    def prev(c):
        return pl.BlockSpec((1, CONV_HALO, dc), lambda b, j: (b, jnp.maximum(j * hb - 1, 0), c))

    def nxt(c):
        return pl.BlockSpec((1, CONV_HALO, dc),
                            lambda b, j: (b, jnp.minimum((j + 1) * hb, nhalo - 1), c))

    vec = pl.BlockSpec((1, dc), lambda b, j: (0, 0))
    return pl.pallas_call(
        functools.partial(_conv_kernel, tl=tl),
        grid=(nb, L // tl),
        in_specs=[cur(a_col), cur(a_col + 1), prev(a_col), prev(a_col + 1), nxt(a_col),
                  nxt(a_col + 1), pl.BlockSpec(w_pad.shape, lambda b, j: (0, 0)), vec, vec, vec],
        out_specs=pl.BlockSpec((1, tl, dc), lambda b, j: (b, j, 0)),
        out_shape=jax.ShapeDtypeStruct((nb, L, dc), F32),
        scratch_shapes=[pltpu.VMEM((tl + 2 * CONV_HALO, dc), F32)],
        compiler_params=_cparams(("arbitrary", "arbitrary")),
        name="conv_group",
    )(proj, proj, proj, proj, proj, proj, w_pad, bias.reshape(1, dc), ln_g.reshape(1, dc),
      ln_b.reshape(1, dc))


def _mix_kernel(of_ref, ob_ref, gr_ref, oc_ref, x_ref, hg_ref, g1_ref, sh2_ref, sc2_ref, ng_ref,
                wo_ref, wq_ref, k1_ref, k2_ref, h1_ref, hp_ref, s_ref):
    o = of_ref[0] + ob_ref[0]
    parts = []
    for hh in range(o.shape[1] // HEAD_DIM):
        oh = o[:, hh * HEAD_DIM:(hh + 1) * HEAD_DIM]
        parts.append(oh * lax.rsqrt(jnp.mean(oh * oh, axis=-1, keepdims=True) + EPS))
    on = jnp.concatenate(parts, axis=-1) * hg_ref[...] * _silu(gr_ref[0])
    cat = jnp.concatenate([on, oc_ref[0]], axis=-1).astype(BF16)
    mix = jnp.dot(cat, wo_ref[...], preferred_element_type=F32)
    h1 = x_ref[0] + g1_ref[0] * mix
    h1_ref[0] = h1
    hn2 = _rms_mod(h1, ng_ref[...], sh2_ref[0], sc2_ref[0])
    hp_ref[0] = _pack_words(hn2)
    q = jnp.dot(hn2.astype(BF16), wq_ref[...], preferred_element_type=F32)
    k1h, k1l, _ = _split3(k1_ref[...])
    k2h, k2l, _ = _split3(k2_ref[...])
    for hd in range(PEER_HEADS):
        for half, (kh, kl) in enumerate(((k1h, k1l), (k2h, k2l))):
            c0 = hd * 2 * PEER_HALF + half * PEER_HALF
            qh, ql, _ = _split3(q[:, c0:c0 + PEER_HALF])
            s_ref[2 * hd + half] = _dot_nt(kh, qh) + _dot_nt(kh, ql) + _dot_nt(kl, qh)


def _mix(o_f, o_b, proj, o_c, x, b0, hgrn_g, mod3, norm_g, w_out_bf16, wq_bf16, keys1, keys2, g_col):
    _, L, d = x.shape
    dh = o_f.shape[2]
    dq = wq_bf16.shape[1]
    tt = 256
    nj = L // tt

    def half(c=0):
        return pl.BlockSpec((1, tt, dh), lambda b, j: (0, j, c))

    def full(bb=0):
        return pl.BlockSpec((1, tt, d), lambda b, j: (bb, j, 0))

    def modc(c):
        return pl.BlockSpec((1, 1, d), lambda b, j: (0, 0, c))

    def const(shape):
        return pl.BlockSpec(shape, lambda b, j: (0,) * len(shape))

    return pl.pallas_call(
        _mix_kernel,
        grid=(1, nj),
        in_specs=[half(), half(), half(g_col), half(), full(b0), const((1, dh)),
                  modc(2), modc(3), modc(4), const((1, d)), const((d, d)), const((d, dq)),
                  const(keys1.shape), const(keys2.shape)],
        out_specs=[full(), pl.BlockSpec((1, tt, d // 2), lambda b, j: (0, j, 0)),
                   pl.BlockSpec((2 * PEER_HEADS, PEER_KEYS, tt), lambda b, j: (0, 0, j))],
        out_shape=[jax.ShapeDtypeStruct((1, L, d), F32), jax.ShapeDtypeStruct((1, L, d // 2), I32),
                   jax.ShapeDtypeStruct((2 * PEER_HEADS, PEER_KEYS, L), F32)],
        compiler_params=_cparams(("arbitrary", "arbitrary")),
        name="mix_scores",
    )(o_f, o_b, proj, o_c, x, hgrn_g.reshape(1, dh), mod3, mod3, mod3, norm_g.reshape(1, d),
      w_out_bf16, wq_bf16, keys1, keys2)


def _top16(s, payload=None):
    n = s.shape[0]
    iota = lax.broadcasted_iota(I32, s.shape, 0).astype(F32)
    vals, tags = [], []
    for _ in range(PEER_TOPK):
        m = jnp.max(s, axis=0, keepdims=True)
        idx = jnp.min(jnp.where(s == m, iota, float(n)), axis=0, keepdims=True)
        hit = iota == idx
        vals.append(m)
        tags.append(idx if payload is None else jnp.max(jnp.where(hit, payload, -1.0), axis=0, keepdims=True))
        s = jnp.where(hit, -jnp.inf, s)
    return jnp.concatenate(vals, axis=0), jnp.concatenate(tags, axis=0)


def _pruned_pairs(x1, x2, combine):
    rows = [combine(x1[a:a + 1], x2[0:PEER_TOPK // (a + 1)]) for a in range(PEER_TOPK // 2)]
    rows.append(combine(x1[PEER_TOPK // 2:], x2[0:1]))
    return jnp.concatenate(rows, axis=0)


def _topk_kernel(s_ref, e_ref, w_ref):
    es, ws = [], []
    for hd in range(PEER_HEADS):
        v1, i1 = _top16(s_ref[2 * hd])
        v2, i2 = _top16(s_ref[2 * hd + 1])
        cand = _pruned_pairs(v1, v2, lambda a, b: a + b)
        cand_e = _pruned_pairs(i1, i2, lambda a, b: a * float(PEER_KEYS) + b)
        sc, e = _top16(cand, cand_e)
        p = jnp.exp(sc - sc[0:1])
        es.append(e.astype(I32))
        ws.append(p / jnp.sum(p, axis=0, keepdims=True))
    e_ref[...] = jnp.concatenate(es, axis=0).T
    w_ref[...] = jnp.concatenate(ws, axis=0).T


def _topk(scores):
    npair, nk, T = scores.shape
    nsel = PEER_HEADS * PEER_TOPK
    tt = 256
    return pl.pallas_call(
        _topk_kernel,
        grid=(T // tt,),
        in_specs=[pl.BlockSpec((npair, nk, tt), lambda i: (0, 0, i))],
        out_specs=[pl.BlockSpec((tt, nsel), lambda i: (i, 0)),
                   pl.BlockSpec((tt, nsel), lambda i: (i, 0))],
        out_shape=[jax.ShapeDtypeStruct((T, nsel), I32), jax.ShapeDtypeStruct((T, nsel), F32)],
        compiler_params=_cparams(("arbitrary",)),
        name="peer_topk",
    )(scores)


WORDS = 4


def _pack_words(x):
    bits = pltpu.bitcast(x.astype(BF16).astype(F32), I32)
    out = []
    for c in range(WORDS):
        lo = bits[:, (2 * c) * LANES:(2 * c + 1) * LANES]
        hi = bits[:, (2 * c + 1) * LANES:(2 * c + 2) * LANES]
        out.append(lax.shift_right_logical(lo, 16) | (hi & jnp.int32(-65536)))
    return jnp.concatenate(out, axis=-1)


def _pack_kernel(x_ref, o_ref):
    te = x_ref.shape[0]
    words = _pack_words(x_ref[...])
    for c in range(WORDS):
        o_ref[pl.ds(c, te, stride=WORDS), :] = words[:, c * LANES:(c + 1) * LANES]


def _pack_table(x):
    ne, d = x.shape
    te = 512
    packed = pl.pallas_call(
        _pack_kernel,
        grid=(ne // te,),
        in_specs=[pl.BlockSpec((te, d), lambda i: (i, 0))],
        out_specs=pl.BlockSpec((te * WORDS, LANES), lambda i: (i, 0)),
        out_shape=jax.ShapeDtypeStruct((ne * WORDS, LANES), I32),
        compiler_params=_cparams(("arbitrary",)),
        name="pack_table",
    )(x)
    return packed.reshape(ne, WORDS, LANES)


NSEL = PEER_HEADS * PEER_TOPK
ROW_TILE = 2 * WORDS


SC_LANES = 16
SC_ROWS = 32
SC_BUFS = 4
SC_TOK = 16
SC_GROUP = 8
VS_GROUP = 2
PART_ROWS = NSEL * SC_LANES // LANES


def _sc_unpack(w):
    lo = plsc.bitcast(lax.shift_left(w, jnp.int32(16)), F32)
    hi = plsc.bitcast(w & jnp.int32(-65536), F32)
    return lo, hi


def _sc_mesh():
    return plsc.VectorSubcoreMesh(core_axis_name="c", subcore_axis_name="s")


def _sc_params():
    return pltpu.CompilerParams(use_tc_tiling_on_sc=True, needs_layout_passes=False)


def _sc_udot(table3, idx, hn3, n_tok):
    T = n_tok
    info = plsc.get_sparse_core_info()
    nw = info.num_cores * info.num_subcores
    tpw = T // nw
    assert T % nw == 0 and tpw % SC_TOK == 0 and NSEL % SC_ROWS == 0
    nchunk = tpw // SC_TOK
    qsteps = NSEL // SC_ROWS
    nstep = SC_TOK * qsteps

    @functools.partial(
        pl.kernel, mesh=_sc_mesh(),
        out_type=jax.ShapeDtypeStruct((T, PART_ROWS, LANES), F32),
        scratch_types=[pltpu.VMEM((SC_TOK * NSEL,), I32),
                       pltpu.VMEM((SC_BUFS, SC_ROWS, WORDS, LANES), I32),
                       pltpu.VMEM((SC_TOK, WORDS * LANES), I32),
                       pltpu.VMEM((SC_TOK, PART_ROWS, LANES), F32),
                       pltpu.SemaphoreType.DMA((SC_BUFS,))],
        compiler_params=_sc_params(),
        name="sc_udot",
    )
    def k(table_hbm, idx_hbm, hn_hbm, out_hbm, idx_v, rows_v, h_v, p_v, sem_g):
        wid = lax.axis_index("s") * info.num_cores + lax.axis_index("c")
        tbase = wid * tpw

        def gather(st, slot):
            return pltpu.make_async_copy(table_hbm.at[idx_v.at[pl.ds(st * SC_ROWS, SC_ROWS)]],
                                         rows_v.at[slot], sem_g.at[slot])

        def compute(tk, q, slot):
            @pl.loop(0, SC_ROWS // SC_GROUP)
            def _(g):
                acc = [jnp.zeros((SC_LANES,), F32) for _ in range(SC_GROUP)]
                for c in range(WORDS):
                    for lv in range(0, LANES // SC_LANES, 2):
                        ls = [pl.ds((lv + d) * SC_LANES, SC_LANES) for d in range(2)]
                        hb = [plsc.bitcast(h_v[tk, pl.ds(c * LANES + (lv + d) * SC_LANES, SC_LANES)], BF16)
                              for d in range(2)]
                        for i in range(SC_GROUP):
                            pr = [plsc.bitcast(rows_v[slot, g * SC_GROUP + i, c, ls[d]], BF16) * hb[d]
                                  for d in range(2)]
                            lo, hi = _sc_unpack(plsc.bitcast(pr[0] + pr[1], I32))
                            acc[i] = acc[i] + lo + hi
                for i in range(SC_GROUP):
                    p_v[tk, q * (SC_ROWS // SC_GROUP) + g, pl.ds(i * SC_LANES, SC_LANES)] = acc[i]

        @pl.loop(0, nchunk)
        def _(ch):
            t0 = tbase + ch * SC_TOK
            pltpu.sync_copy(idx_hbm.at[pl.ds(t0 * NSEL, SC_TOK * NSEL)], idx_v)
            pltpu.sync_copy(hn_hbm.at[pl.ds(t0, SC_TOK)], h_v)
            for s in range(SC_BUFS - 1):
                gather(s, s).start()

            @pl.loop(0, SC_TOK)
            def _(tk):
                for q in range(qsteps):
                    st = tk * qsteps + q
                    slot = q % SC_BUFS

                    @pl.when(st + SC_BUFS - 1 < nstep)
                    def _():
                        gather(st + SC_BUFS - 1, (q + SC_BUFS - 1) % SC_BUFS).start()

                    gather(st, slot).wait()
                    compute(tk, q, slot)

            pltpu.sync_copy(p_v, out_hbm.at[pl.ds(t0, SC_TOK)])

    return k(table3, idx, hn3)


def _sc_vsum(table3, idx, coefx):
    assert VS_GROUP == 2, "the loop body sums the products of exactly two experts in bf16"
    T = coefx.shape[0]
    info = plsc.get_sparse_core_info()
    nw = info.num_cores * info.num_subcores
    tpw = T // nw
    assert T % nw == 0 and tpw % SC_TOK == 0 and NSEL % SC_ROWS == 0 and SC_GROUP % VS_GROUP == 0
    nchunk = tpw // SC_TOK
    qsteps = NSEL // SC_ROWS
    nstep = SC_TOK * qsteps
    nlv = LANES // SC_LANES
    sub = SC_GROUP // VS_GROUP

    @functools.partial(
        pl.kernel, mesh=_sc_mesh(),
        out_type=jax.ShapeDtypeStruct((T, ROW_TILE * LANES), F32),
        scratch_types=[pltpu.VMEM((SC_TOK * NSEL,), I32),
                       pltpu.VMEM((SC_BUFS, SC_ROWS, WORDS, LANES), I32),
                       pltpu.VMEM((SC_TOK, PART_ROWS, LANES), I32),
                       pltpu.VMEM((SC_TOK, ROW_TILE * LANES), F32),
                       pltpu.SemaphoreType.DMA((SC_BUFS,))],
        compiler_params=_sc_params(),
        name="sc_vsum",
    )
    def k(table_hbm, idx_hbm, coef_hbm, out_hbm, idx_v, rows_v, c_v, y_v, sem_g):
        wid = lax.axis_index("s") * info.num_cores + lax.axis_index("c")
        tbase = wid * tpw

        def gather(st, slot):
            return pltpu.make_async_copy(table_hbm.at[idx_v.at[pl.ds(st * SC_ROWS, SC_ROWS)]],
                                         rows_v.at[slot], sem_g.at[slot])

        def compute(tk, q, slot):
            for c in range(WORDS):
                if q == 0:
                    init = tuple(jnp.zeros((SC_LANES,), F32) for _ in range(2 * nlv))
                else:
                    init = tuple(y_v[tk, pl.ds((2 * c + p) * LANES + lv * SC_LANES, SC_LANES)]
                                 for p in range(2) for lv in range(nlv))

                def body(g, acc):
                    acc = list(acc)
                    cb = [plsc.bitcast(c_v[tk, q * (SC_ROWS // SC_GROUP) + g // sub,
                                           pl.ds(((g % sub) * VS_GROUP + i) * SC_LANES, SC_LANES)], BF16)
                          for i in range(VS_GROUP)]
                    for lv in range(nlv):
                        pr = [cb[i] * plsc.bitcast(rows_v[slot, g * VS_GROUP + i, c,
                                                          pl.ds(lv * SC_LANES, SC_LANES)], BF16)
                              for i in range(VS_GROUP)]
                        lo, hi = _sc_unpack(plsc.bitcast(pr[0] + pr[1], I32))
                        acc[lv] = acc[lv] + lo
                        acc[nlv + lv] = acc[nlv + lv] + hi
                    return tuple(acc)

                acc = lax.fori_loop(0, SC_ROWS // VS_GROUP, body, init)
                for p in range(2):
                    for lv in range(nlv):
                        y_v[tk, pl.ds((2 * c + p) * LANES + lv * SC_LANES, SC_LANES)] = acc[p * nlv + lv]

        @pl.loop(0, nchunk)
        def _(ch):
            t0 = tbase + ch * SC_TOK
            pltpu.sync_copy(idx_hbm.at[pl.ds(t0 * NSEL, SC_TOK * NSEL)], idx_v)
            pltpu.sync_copy(coef_hbm.at[pl.ds(t0, SC_TOK)], c_v)
            for s in range(SC_BUFS - 1):
                gather(s, s).start()

            @pl.loop(0, SC_TOK)
            def _(tk):
                for q in range(qsteps):
                    st = tk * qsteps + q
                    slot = q % SC_BUFS

                    @pl.when(st + SC_BUFS - 1 < nstep)
                    def _():
                        gather(st + SC_BUFS - 1, (q + SC_BUFS - 1) % SC_BUFS).start()

                    gather(st, slot).wait()
                    compute(tk, q, slot)

            pltpu.sync_copy(y_v, out_hbm.at[pl.ds(t0, SC_TOK)])

    return k(table3, idx, coefx)


def _coef_kernel(p_ref, w_ref, cx_ref, *, tt):
    row = lax.broadcasted_iota(I32, (LANES, NSEL), 0)
    col = lax.broadcasted_iota(I32, (LANES, NSEL), 1)
    per_row = LANES // SC_LANES
    act = jnp.zeros((tt, NSEL), F32)
    for s in range(PART_ROWS):
        fold = (col == per_row * s + row // SC_LANES).astype(BF16)
        p1, p2, p3 = _split3(p_ref[pl.ds(s, tt, stride=PART_ROWS), :])
        act = act + (jnp.dot(p1, fold, preferred_element_type=F32) + jnp.dot(p2, fold, preferred_element_type=F32)
                     + jnp.dot(p3, fold, preferred_element_type=F32))
    coef = w_ref[...] * (0.5 * act * (1.0 + lax.erf(act * (2.0 ** -0.5))))
    cb = coef.astype(BF16)
    for s in range(PART_ROWS):
        spread = (row == per_row * s + col // SC_LANES).astype(BF16)
        bits = pltpu.bitcast(jnp.dot(cb, spread, preferred_element_type=F32), I32)
        cx_ref[pl.ds(s, tt, stride=PART_ROWS), :] = bits | lax.shift_right_logical(bits, 16)


def _coef(p2d, w):
    T = p2d.shape[0] // PART_ROWS
    tt = 256
    return pl.pallas_call(
        functools.partial(_coef_kernel, tt=tt),
        grid=(T // tt,),
        in_specs=[pl.BlockSpec((tt * PART_ROWS, LANES), lambda i: (i, 0)),
                  pl.BlockSpec((tt, NSEL), lambda i: (i, 0))],
        out_specs=pl.BlockSpec((tt * PART_ROWS, LANES), lambda i: (i, 0)),
        out_shape=jax.ShapeDtypeStruct((T * PART_ROWS, LANES), I32),
        compiler_params=_cparams(("arbitrary",)),
        name="peer_coef",
    )(p2d, w)


def _final_kernel(h1_ref, y_ref, g2_ref, fg_ref, fsh_ref, fsc_ref, *refs):
    o_ref = refs[-1]
    h = h1_ref[0] + g2_ref[0] * y_ref[...]
    o_ref[0] = _rms_mod(h, fg_ref[...], fsh_ref[0], fsc_ref[0])


def _final(h1, y, mod3, fmod3, final_g, out_prev, b0, nb):
    _, L, d = h1.shape
    tt = 512
    in_specs = [pl.BlockSpec((1, tt, d), lambda b, j: (0, j, 0)),
                pl.BlockSpec((tt, d), lambda b, j: (j, 0)),
                pl.BlockSpec((1, 1, d), lambda b, j: (0, 0, 5)),
                pl.BlockSpec((1, d), lambda b, j: (0, 0)),
                pl.BlockSpec((1, 1, d), lambda b, j: (0, 0, 0)),
                pl.BlockSpec((1, 1, d), lambda b, j: (0, 0, 1))]
    args = [h1, y, mod3, final_g.reshape(1, d), fmod3, fmod3]
    aliases = {}
    if out_prev is not None:
        in_specs.append(pl.BlockSpec(memory_space=pl.ANY))
        aliases = {len(args): 0}
        args.append(out_prev)
    return pl.pallas_call(
        _final_kernel,
        grid=(1, L // tt),
        in_specs=in_specs,
        out_specs=pl.BlockSpec((1, tt, d), lambda b, j: (b0, j, 0)),
        out_shape=jax.ShapeDtypeStruct((nb, L, d), F32),
        input_output_aliases=aliases,
        compiler_params=_cparams(("arbitrary", "arbitrary")),
        name="final_norm",
    )(*args)


def kernel(x, c, ada_w, ada_b, norm_mix_g, w_in, lb_gamma_fwd, lb_gamma_bwd, hgrn_norm_g, conv_w,
           conv_b, conv_ln_g, conv_ln_b, w_out, norm_ffn_g, peer_wq, peer_keys1, peer_keys2, peer_u,
           peer_v, final_ada_w, final_ada_b, final_norm_g):
    nb, L, d = x.shape
    assert ada_w.shape[0] == 1, "single-layer trunk"
    d_hgrn = lb_gamma_fwd.shape[1]

    c_pad = jnp.pad(c, ((0, SUBLANES - nb), (0, 0)))
    mod3 = _ada(c_pad, ada_w[0], ada_b[0])[:nb].reshape(nb, 1, 6 * d)
    fmod3 = _ada(c_pad, final_ada_w, final_ada_b)[:nb].reshape(nb, 1, 2 * d)

    d_conv = conv_w.shape[2]
    w_pad = jnp.pad(conv_w[0], ((0, 1), (0, 0)))
    w_in_bf, w_out_bf, wq_bf = w_in[0].astype(BF16), w_out[0].astype(BF16), peer_wq[0].astype(BF16)
    mb, u_rows = lax.optimization_barrier((mod3[0:1], _pack_table(peer_u[0])))
    pending = []
    for b in range(nb):
        proj = _inproj(x, b, norm_mix_g[0], mb, w_in_bf)
        o_f, o_b = _hgrn(proj, lb_gamma_fwd, lb_gamma_bwd)
        o_c = _conv(proj, w_pad, conv_b[0], conv_ln_g[0], conv_ln_b[0], a_col=5 * d_hgrn // d_conv)
        h1, hp, scores = _mix(o_f, o_b, proj, o_c, x, b, hgrn_norm_g[0], mb, norm_ffn_g[0], w_out_bf,
                              wq_bf, peer_keys1[0], peer_keys2[0], g_col=4)
        e, w = _topk(scores)
        mb_this = mb
        if b + 1 < nb:
            e, w, mb = lax.optimization_barrier((e, w, mod3[b + 1:b + 2]))
        idx = e.reshape(L * NSEL)
        p = _sc_udot(u_rows, idx, hp.reshape(L, d // 2), L)
        pending.append((h1, idx, w, p, mb_this))

    idx_last, v_table = lax.optimization_barrier((pending[-1][1], peer_v[0]))
    v_rows = _pack_table(v_table)
    out = None
    for b, (h1, idx, w, p, mb_b) in enumerate(pending):
        cx = _coef(p.reshape(L * PART_ROWS, LANES), w)
        y = _sc_vsum(v_rows, idx if b + 1 < nb else idx_last, cx.reshape(L, PART_ROWS, LANES))
        out = _final(h1, y, mb_b, fmod3[b:b + 1], final_norm_g, out, b, nb)
    return out
```

```python
import functools

import jax
import jax.numpy as jnp
from jax import lax
from jax.experimental import pallas as pl
from jax.experimental.pallas import tpu as pltpu
from jax.experimental.pallas import tpu_sc as plsc

F32 = jnp.float32
BF16 = jnp.bfloat16
I32 = jnp.int32

EPS = 1e-6
HEAD_DIM = 128
CHUNK = 64
CONV_WIDTH = 31
CONV_PAD = CONV_WIDTH // 2
PEER_HEADS = 8
PEER_KEYS = 128
PEER_TOPK = 16
PEER_HALF = 128
LANES = 128
SUBLANES = 8
VMEM_LIMIT = 48 * 1024 * 1024


def _cparams(sem):
    return pltpu.CompilerParams(dimension_semantics=sem, vmem_limit_bytes=VMEM_LIMIT)


def _silu(x):
    return x * jax.nn.sigmoid(x)


def _split3(x):
    x1 = x.astype(BF16)
    r1 = x - x1.astype(F32)
    x2 = r1.astype(BF16)
    x3 = (r1 - x2.astype(F32)).astype(BF16)
    return x1, x2, x3


def _dot_nt(a, b):
    return lax.dot_general(a, b, (((1,), (1,)), ((), ())), preferred_element_type=F32)


def _dot_tn(a, b):
    return lax.dot_general(a, b, (((0,), (0,)), ((), ())), preferred_element_type=F32)


def _ada_kernel(c_ref, w_ref, b_ref, o_ref):
    ca = _silu(c_ref[...])
    o_ref[...] = jnp.dot(ca, w_ref[...], preferred_element_type=F32,
                         precision=lax.Precision.HIGHEST) + b_ref[...]


def _ada(c_pad, w, b):
    d, n = w.shape
    tn = 1024
    return pl.pallas_call(
        _ada_kernel,
        grid=(n // tn,),
        in_specs=[pl.BlockSpec((SUBLANES, d), lambda j: (0, 0)),
                  pl.BlockSpec((d, tn), lambda j: (0, j)),
                  pl.BlockSpec((1, tn), lambda j: (0, j))],
        out_specs=pl.BlockSpec((SUBLANES, tn), lambda j: (0, j)),
        out_shape=jax.ShapeDtypeStruct((SUBLANES, n), F32),
        compiler_params=_cparams(("arbitrary",)),
        name="ada_proj",
    )(c_pad, w, b.reshape(1, n))


def _rms_mod(x, g, sh, sc):
    y = x * lax.rsqrt(jnp.mean(x * x, axis=-1, keepdims=True) + EPS) * g
    return y * (1.0 + sc) + sh


def _inproj_kernel(x_ref, g_ref, sh_ref, sc_ref, w_ref, o_ref):
    hn = _rms_mod(x_ref[0], g_ref[...], sh_ref[0], sc_ref[0])
    o_ref[0] = jnp.dot(hn.astype(BF16), w_ref[...], preferred_element_type=F32)


def _inproj(x, b0, g, mod3, w_bf16):
    _, L, d = x.shape
    n = w_bf16.shape[1]
    tt = 512
    return pl.pallas_call(
        _inproj_kernel,
        grid=(1, L // tt),
        in_specs=[pl.BlockSpec((1, tt, d), lambda b, j: (b0, j, 0)),
                  pl.BlockSpec((1, d), lambda b, j: (0, 0)),
                  pl.BlockSpec((1, 1, d), lambda b, j: (0, 0, 0)),
                  pl.BlockSpec((1, 1, d), lambda b, j: (0, 0, 1)),
                  pl.BlockSpec((d, n), lambda b, j: (0, 0))],
        out_specs=pl.BlockSpec((1, tt, n), lambda b, j: (0, j, 0)),
        out_shape=jax.ShapeDtypeStruct((1, L, n), F32),
        compiler_params=_cparams(("arbitrary", "arbitrary")),
        name="in_proj",
    )(x, g.reshape(1, d), mod3, mod3, w_bf16)


def _lower_bound(lbg):
    ex = jnp.exp(lbg - jnp.max(lbg, axis=0, keepdims=True))
    return ex[0:1] / jnp.sum(ex, axis=0, keepdims=True)


def _hgrn_chunk(q_ref, f_ref, i_ref, o_ref, lb, st, c, reverse):
    row = lax.broadcasted_iota(I32, (CHUNK, CHUNK), 0)
    col = lax.broadcasted_iota(I32, (CHUNK, CHUNK), 1)
    keep = (col >= row) if reverse else (col <= row)
    tri = keep.astype(BF16)
    i_mid = CHUNK // 2 if reverse else CHUNK // 2 - 1
    i_end = 0 if reverse else CHUNK - 1
    r0 = pl.multiple_of(c * CHUNK, CHUNK)
    z = f_ref[0, pl.ds(r0, CHUNK), :]
    q = _silu(q_ref[0, pl.ds(r0, CHUNK), :])
    v = i_ref[0, pl.ds(r0, CHUNK), :]
    logf = jnp.log(lb + (1.0 - lb) * jax.nn.sigmoid(z))
    k = (1.0 - lb) * jax.nn.sigmoid(-z)
    l1, l2, l3 = _split3(logf)
    cum = (jnp.dot(tri, l1, preferred_element_type=F32)
           + jnp.dot(tri, l2, preferred_element_type=F32)
           + jnp.dot(tri, l3, preferred_element_type=F32))
    c_mid = cum[i_mid:i_mid + 1]
    c_end = cum[i_end:i_end + 1]
    qr = (q * jnp.exp(cum - c_mid)).astype(BF16)
    kr = (k * jnp.exp(c_mid - cum)).astype(BF16)
    scores = jnp.where(keep, _dot_nt(qr, kr), 0.0)
    vb = v.astype(BF16)
    o_intra = jnp.dot(scores.astype(BF16), vb, preferred_element_type=F32)
    o_inter = _dot_nt((q * jnp.exp(cum)).astype(BF16), st.astype(BF16))
    o_ref[0, pl.ds(r0, CHUNK), :] = o_intra + o_inter
    ku = (k * jnp.exp(c_end - cum)).astype(BF16)
    return st * jnp.exp(c_end) + _dot_tn(vb, ku)


def _hgrn_kernel(qf_ref, ff_ref, if_ref, qb_ref, fb_ref, ib_ref, lbf_ref, lbb_ref, of_ref, ob_ref,
                 stf_ref, stb_ref, *, lblk):
    @pl.when(pl.program_id(2) == 0)
    def _():
        stf_ref[...] = jnp.zeros_like(stf_ref)
        stb_ref[...] = jnp.zeros_like(stb_ref)

    lb_f = _lower_bound(lbf_ref[...])
    lb_b = _lower_bound(lbb_ref[...])
    nch = lblk // CHUNK

    def body(ci, carry):
        st_f, st_b = carry
        st_f = _hgrn_chunk(qf_ref, ff_ref, if_ref, of_ref, lb_f, st_f, ci, False)
        st_b = _hgrn_chunk(qb_ref, fb_ref, ib_ref, ob_ref, lb_b, st_b, nch - 1 - ci, True)
        return st_f, st_b

    st_f, st_b = lax.fori_loop(0, nch, body, (stf_ref[...], stb_ref[...]))
    stf_ref[...] = st_f
    stb_ref[...] = st_b


def _hgrn(proj, lb_gamma_fwd, lb_gamma_bwd):
    nb, L, _ = proj.shape
    nh = lb_gamma_fwd.shape[1] // HEAD_DIM
    lblk = min(1024, L)
    nblk = L // lblk

    def fwd(c0):
        return pl.BlockSpec((1, lblk, HEAD_DIM), lambda b, h, j: (b, j, c0 + h))

    def bwd(c0):
        return pl.BlockSpec((1, lblk, HEAD_DIM), lambda b, h, j: (b, nblk - 1 - j, c0 + h))

    lbs = pl.BlockSpec((lb_gamma_fwd.shape[0], HEAD_DIM), lambda b, h, j: (0, h))
    out = jax.ShapeDtypeStruct((nb, L, nh * HEAD_DIM), F32)
    return pl.pallas_call(
        functools.partial(_hgrn_kernel, lblk=lblk),
        grid=(nb, nh, nblk),
        in_specs=[fwd(0), fwd(nh), fwd(3 * nh), bwd(0), bwd(2 * nh), bwd(3 * nh), lbs, lbs],
        out_specs=[fwd(0), bwd(0)],
        out_shape=[out, out],
        scratch_shapes=[pltpu.VMEM((HEAD_DIM, HEAD_DIM), F32), pltpu.VMEM((HEAD_DIM, HEAD_DIM), F32)],
        compiler_params=_cparams(("arbitrary", "arbitrary", "arbitrary")),
        name="hgrn_bidir",
    )(proj, proj, proj, proj, proj, proj, lb_gamma_fwd, lb_gamma_bwd)


CONV_HALO = 16
CONV_ROWS = 64


def _conv_kernel(a_ref, g_ref, ap_ref, gp_ref, an_ref, gn_ref, w_ref, b_ref, lg_ref, lb_ref,
                 o_ref, hbuf, *, tl):
    j = pl.program_id(1)
    last = pl.num_programs(1) - 1
    hbuf[CONV_HALO:CONV_HALO + tl, :] = a_ref[0] * jax.nn.sigmoid(g_ref[0])
    hp = ap_ref[0] * jax.nn.sigmoid(gp_ref[0])
    hbuf[0:CONV_HALO, :] = jnp.where(j > 0, hp, 0.0)
    hn = an_ref[0] * jax.nn.sigmoid(gn_ref[0])
    hbuf[CONV_HALO + tl:2 * CONV_HALO + tl, :] = jnp.where(j < last, hn, 0.0)
    off = CONV_HALO - CONV_PAD
    for r in range(tl // CONV_ROWS):
        acc = jnp.zeros((CONV_ROWS, a_ref.shape[2]), F32)
        for k in range(CONV_WIDTH):
            s = r * CONV_ROWS + k + off
            acc = acc + w_ref[k:k + 1, :] * hbuf[s:s + CONV_ROWS, :]
        acc = acc + b_ref[...]
        mu = jnp.mean(acc, axis=-1, keepdims=True)
        cen = acc - mu
        var = jnp.mean(cen * cen, axis=-1, keepdims=True)
        y = cen * lax.rsqrt(var + EPS) * lg_ref[...] + lb_ref[...]
        o_ref[0, r * CONV_ROWS:(r + 1) * CONV_ROWS, :] = _silu(y)


def _conv(proj, w_pad, bias, ln_g, ln_b, a_col):
    nb, L, _ = proj.shape
    dc = w_pad.shape[1]
    tl = 256
    hb = tl // CONV_HALO
    nhalo = L // CONV_HALO

    def cur(c):
        return pl.BlockSpec((1, tl, dc), lambda b, j: (b, j, c))

    def prev(c):
        return pl.BlockSpec((1, CONV_HALO, dc), lambda b, j: (b, jnp.maximum(j * hb - 1, 0), c))

    def nxt(c):
        return pl.BlockSpec((1, CONV_HALO, dc),
                            lambda b, j: (b, jnp.minimum((j + 1) * hb, nhalo - 1), c))

    vec = pl.BlockSpec((1, dc), lambda b, j: (0, 0))
    return pl.pallas_call(
        functools.partial(_conv_kernel, tl=tl),
        grid=(nb, L // tl),
        in_specs=[cur(a_col), cur(a_col + 1), prev(a_col), prev(a_col + 1), nxt(a_col),
                  nxt(a_col + 1), pl.BlockSpec(w_pad.shape, lambda b, j: (0, 0)), vec, vec, vec],
        out_specs=pl.BlockSpec((1, tl, dc), lambda b, j: (b, j, 0)),
        out_shape=jax.ShapeDtypeStruct((nb, L, dc), F32),
        scratch_shapes=[pltpu.VMEM((tl + 2 * CONV_HALO, dc), F32)],
        compiler_params=_cparams(("arbitrary", "arbitrary")),
        name="conv_group",
    )(proj, proj, proj, proj, proj, proj, w_pad, bias.reshape(1, dc), ln_g.reshape(1, dc),
      ln_b.reshape(1, dc))


def _mix_kernel(of_ref, ob_ref, gr_ref, oc_ref, x_ref, hg_ref, g1_ref, sh2_ref, sc2_ref, ng_ref,
                wo_ref, wq_ref, k1_ref, k2_ref, h1_ref, hp_ref, s_ref):
    o = of_ref[0] + ob_ref[0]
    parts = []
    for hh in range(o.shape[1] // HEAD_DIM):
        oh = o[:, hh * HEAD_DIM:(hh + 1) * HEAD_DIM]
        parts.append(oh * lax.rsqrt(jnp.mean(oh * oh, axis=-1, keepdims=True) + EPS))
    on = jnp.concatenate(parts, axis=-1) * hg_ref[...] * _silu(gr_ref[0])
    cat = jnp.concatenate([on, oc_ref[0]], axis=-1).astype(BF16)
    mix = jnp.dot(cat, wo_ref[...], preferred_element_type=F32)
    h1 = x_ref[0] + g1_ref[0] * mix
    h1_ref[0] = h1
    hn2 = _rms_mod(h1, ng_ref[...], sh2_ref[0], sc2_ref[0])
    hp_ref[0] = _pack_words(hn2)
    q = jnp.dot(hn2.astype(BF16), wq_ref[...], preferred_element_type=F32)
    k1h, k1l, _ = _split3(k1_ref[...])
    k2h, k2l, _ = _split3(k2_ref[...])
    for hd in range(PEER_HEADS):
        for half, (kh, kl) in enumerate(((k1h, k1l), (k2h, k2l))):
            c0 = hd * 2 * PEER_HALF + half * PEER_HALF
            qh, ql, _ = _split3(q[:, c0:c0 + PEER_HALF])
            s_ref[2 * hd + half] = _dot_nt(kh, qh) + _dot_nt(kh, ql) + _dot_nt(kl, qh)


def _mix(o_f, o_b, proj, o_c, x, b0, hgrn_g, mod3, norm_g, w_out_bf16, wq_bf16, keys1, keys2, g_col):
    _, L, d = x.shape
    dh = o_f.shape[2]
    dq = wq_bf16.shape[1]
    tt = 256
    nj = L // tt

    def half(c=0):
        return pl.BlockSpec((1, tt, dh), lambda b, j: (0, j, c))

    def full(bb=0):
        return pl.BlockSpec((1, tt, d), lambda b, j: (bb, j, 0))

    def modc(c):
        return pl.BlockSpec((1, 1, d), lambda b, j: (0, 0, c))

    def const(shape):
        return pl.BlockSpec(shape, lambda b, j: (0,) * len(shape))

    return pl.pallas_call(
        _mix_kernel,
        grid=(1, nj),
        in_specs=[half(), half(), half(g_col), half(), full(b0), const((1, dh)),
                  modc(2), modc(3), modc(4), const((1, d)), const((d, d)), const((d, dq)),
                  const(keys1.shape), const(keys2.shape)],
        out_specs=[full(), pl.BlockSpec((1, tt, d // 2), lambda b, j: (0, j, 0)),
                   pl.BlockSpec((2 * PEER_HEADS, PEER_KEYS, tt), lambda b, j: (0, 0, j))],
        out_shape=[jax.ShapeDtypeStruct((1, L, d), F32), jax.ShapeDtypeStruct((1, L, d // 2), I32),
                   jax.ShapeDtypeStruct((2 * PEER_HEADS, PEER_KEYS, L), F32)],
        compiler_params=_cparams(("arbitrary", "arbitrary")),
        name="mix_scores",
    )(o_f, o_b, proj, o_c, x, hgrn_g.reshape(1, dh), mod3, mod3, mod3, norm_g.reshape(1, d),
      w_out_bf16, wq_bf16, keys1, keys2)


def _top16(s, payload=None):
    n = s.shape[0]
    iota = lax.broadcasted_iota(I32, s.shape, 0).astype(F32)
    vals, tags = [], []
    for _ in range(PEER_TOPK):
        m = jnp.max(s, axis=0, keepdims=True)
        idx = jnp.min(jnp.where(s == m, iota, float(n)), axis=0, keepdims=True)
        hit = iota == idx
        vals.append(m)
        tags.append(idx if payload is None else jnp.max(jnp.where(hit, payload, -1.0), axis=0, keepdims=True))
        s = jnp.where(hit, -jnp.inf, s)
    return jnp.concatenate(vals, axis=0), jnp.concatenate(tags, axis=0)


def _pruned_pairs(x1, x2, combine):
    rows = [combine(x1[a:a + 1], x2[0:PEER_TOPK // (a + 1)]) for a in range(PEER_TOPK // 2)]
    rows.append(combine(x1[PEER_TOPK // 2:], x2[0:1]))
    return jnp.concatenate(rows, axis=0)


def _topk_kernel(s_ref, e_ref, w_ref):
    es, ws = [], []
    for hd in range(PEER_HEADS):
        v1, i1 = _top16(s_ref[2 * hd])
        v2, i2 = _top16(s_ref[2 * hd + 1])
        cand = _pruned_pairs(v1, v2, lambda a, b: a + b)
        cand_e = _pruned_pairs(i1, i2, lambda a, b: a * float(PEER_KEYS) + b)
        sc, e = _top16(cand, cand_e)
        p = jnp.exp(sc - sc[0:1])
        es.append(e.astype(I32))
        ws.append(p / jnp.sum(p, axis=0, keepdims=True))
    e_ref[...] = jnp.concatenate(es, axis=0).T
    w_ref[...] = jnp.concatenate(ws, axis=0).T


def _topk(scores):
    npair, nk, T = scores.shape
    nsel = PEER_HEADS * PEER_TOPK
    tt = 256
    return pl.pallas_call(
        _topk_kernel,
        grid=(T // tt,),
        in_specs=[pl.BlockSpec((npair, nk, tt), lambda i: (0, 0, i))],
        out_specs=[pl.BlockSpec((tt, nsel), lambda i: (i, 0)),
                   pl.BlockSpec((tt, nsel), lambda i: (i, 0))],
        out_shape=[jax.ShapeDtypeStruct((T, nsel), I32), jax.ShapeDtypeStruct((T, nsel), F32)],
        compiler_params=_cparams(("arbitrary",)),
        name="peer_topk",
    )(scores)


WORDS = 4


def _pack_words(x):
    bits = pltpu.bitcast(x.astype(BF16).astype(F32), I32)
    out = []
    for c in range(WORDS):
        lo = bits[:, (2 * c) * LANES:(2 * c + 1) * LANES]
        hi = bits[:, (2 * c + 1) * LANES:(2 * c + 2) * LANES]
        out.append(lax.shift_right_logical(lo, 16) | (hi & jnp.int32(-65536)))
    return jnp.concatenate(out, axis=-1)


def _pack_kernel(x_ref, o_ref):
    te = x_ref.shape[0]
    words = _pack_words(x_ref[...])
    for c in range(WORDS):
        o_ref[pl.ds(c, te, stride=WORDS), :] = words[:, c * LANES:(c + 1) * LANES]


def _pack_table(x):
    ne, d = x.shape
    te = 512
    packed = pl.pallas_call(
        _pack_kernel,
        grid=(ne // te,),
        in_specs=[pl.BlockSpec((te, d), lambda i: (i, 0))],
        out_specs=pl.BlockSpec((te * WORDS, LANES), lambda i: (i, 0)),
        out_shape=jax.ShapeDtypeStruct((ne * WORDS, LANES), I32),
        compiler_params=_cparams(("arbitrary",)),
        name="pack_table",
    )(x)
    return packed.reshape(ne, WORDS, LANES)


NSEL = PEER_HEADS * PEER_TOPK
ROW_TILE = 2 * WORDS


SC_LANES = 16
SC_ROWS = 32
SC_BUFS = 4
SC_TOK = 16
SC_GROUP = 8
VS_GROUP = 2
PART_ROWS = NSEL * SC_LANES // LANES


def _sc_unpack(w):
    lo = plsc.bitcast(lax.shift_left(w, jnp.int32(16)), F32)
    hi = plsc.bitcast(w & jnp.int32(-65536), F32)
    return lo, hi


def _sc_mesh():
    return plsc.VectorSubcoreMesh(core_axis_name="c", subcore_axis_name="s")


def _sc_params():
    return pltpu.CompilerParams(use_tc_tiling_on_sc=True, needs_layout_passes=False)


def _sc_udot(table3, idx, hn3, n_tok):
    T = n_tok
    info = plsc.get_sparse_core_info()
    nw = info.num_cores * info.num_subcores
    tpw = T // nw
    assert T % nw == 0 and tpw % SC_TOK == 0 and NSEL % SC_ROWS == 0
    nchunk = tpw // SC_TOK
    qsteps = NSEL // SC_ROWS
    nstep = SC_TOK * qsteps

    @functools.partial(
        pl.kernel, mesh=_sc_mesh(),
        out_type=jax.ShapeDtypeStruct((T, PART_ROWS, LANES), F32),
        scratch_types=[pltpu.VMEM((SC_TOK * NSEL,), I32),
                       pltpu.VMEM((SC_BUFS, SC_ROWS, WORDS, LANES), I32),
                       pltpu.VMEM((SC_TOK, WORDS * LANES), I32),
                       pltpu.VMEM((SC_TOK, PART_ROWS, LANES), F32),
                       pltpu.SemaphoreType.DMA((SC_BUFS,))],
        compiler_params=_sc_params(),
        name="sc_udot",
    )
    def k(table_hbm, idx_hbm, hn_hbm, out_hbm, idx_v, rows_v, h_v, p_v, sem_g):
        wid = lax.axis_index("s") * info.num_cores + lax.axis_index("c")
        tbase = wid * tpw

        def gather(st, slot):
            return pltpu.make_async_copy(table_hbm.at[idx_v.at[pl.ds(st * SC_ROWS, SC_ROWS)]],
                                         rows_v.at[slot], sem_g.at[slot])

        def compute(tk, q, slot):
            @pl.loop(0, SC_ROWS // SC_GROUP)
            def _(g):
                acc = [jnp.zeros((SC_LANES,), F32) for _ in range(SC_GROUP)]
                for c in range(WORDS):
                    for lv in range(0, LANES // SC_LANES, 2):
                        ls = [pl.ds((lv + d) * SC_LANES, SC_LANES) for d in range(2)]
                        hb = [plsc.bitcast(h_v[tk, pl.ds(c * LANES + (lv + d) * SC_LANES, SC_LANES)], BF16)
                              for d in range(2)]
                        for i in range(SC_GROUP):
                            pr = [plsc.bitcast(rows_v[slot, g * SC_GROUP + i, c, ls[d]], BF16) * hb[d]
                                  for d in range(2)]
                            lo, hi = _sc_unpack(plsc.bitcast(pr[0] + pr[1], I32))
                            acc[i] = acc[i] + lo + hi
                for i in range(SC_GROUP):
                    p_v[tk, q * (SC_ROWS // SC_GROUP) + g, pl.ds(i * SC_LANES, SC_LANES)] = acc[i]

        @pl.loop(0, nchunk)
        def _(ch):
            t0 = tbase + ch * SC_TOK
            pltpu.sync_copy(idx_hbm.at[pl.ds(t0 * NSEL, SC_TOK * NSEL)], idx_v)
            pltpu.sync_copy(hn_hbm.at[pl.ds(t0, SC_TOK)], h_v)
            for s in range(SC_BUFS - 1):
                gather(s, s).start()

            @pl.loop(0, SC_TOK)
            def _(tk):
                for q in range(qsteps):
                    st = tk * qsteps + q
                    slot = q % SC_BUFS

                    @pl.when(st + SC_BUFS - 1 < nstep)
                    def _():
                        gather(st + SC_BUFS - 1, (q + SC_BUFS - 1) % SC_BUFS).start()

                    gather(st, slot).wait()
                    compute(tk, q, slot)

            pltpu.sync_copy(p_v, out_hbm.at[pl.ds(t0, SC_TOK)])

    return k(table3, idx, hn3)


def _sc_vsum(table3, idx, coefx):
    assert VS_GROUP == 2, "the loop body sums the products of exactly two experts in bf16"
    T = coefx.shape[0]
    info = plsc.get_sparse_core_info()
    nw = info.num_cores * info.num_subcores
    tpw = T // nw
    assert T % nw == 0 and tpw % SC_TOK == 0 and NSEL % SC_ROWS == 0 and SC_GROUP % VS_GROUP == 0
    nchunk = tpw // SC_TOK
    qsteps = NSEL // SC_ROWS
    nstep = SC_TOK * qsteps
    nlv = LANES // SC_LANES
    sub = SC_GROUP // VS_GROUP

    @functools.partial(
        pl.kernel, mesh=_sc_mesh(),
        out_type=jax.ShapeDtypeStruct((T, ROW_TILE * LANES), F32),
        scratch_types=[pltpu.VMEM((SC_TOK * NSEL,), I32),
                       pltpu.VMEM((SC_BUFS, SC_ROWS, WORDS, LANES), I32),
                       pltpu.VMEM((SC_TOK, PART_ROWS, LANES), I32),
                       pltpu.VMEM((SC_TOK, ROW_TILE * LANES), F32),
                       pltpu.SemaphoreType.DMA((SC_BUFS,))],
        compiler_params=_sc_params(),
        name="sc_vsum",
    )
    def k(table_hbm, idx_hbm, coef_hbm, out_hbm, idx_v, rows_v, c_v, y_v, sem_g):
        wid = lax.axis_index("s") * info.num_cores + lax.axis_index("c")
        tbase = wid * tpw

        def gather(st, slot):
            return pltpu.make_async_copy(table_hbm.at[idx_v.at[pl.ds(st * SC_ROWS, SC_ROWS)]],
                                         rows_v.at[slot], sem_g.at[slot])

        def compute(tk, q, slot):
            for c in range(WORDS):
                if q == 0:
                    init = tuple(jnp.zeros((SC_LANES,), F32) for _ in range(2 * nlv))
                else:
                    init = tuple(y_v[tk, pl.ds((2 * c + p) * LANES + lv * SC_LANES, SC_LANES)]
                                 for p in range(2) for lv in range(nlv))

                def body(g, acc):
                    acc = list(acc)
                    cb = [plsc.bitcast(c_v[tk, q * (SC_ROWS // SC_GROUP) + g // sub,
                                           pl.ds(((g % sub) * VS_GROUP + i) * SC_LANES, SC_LANES)], BF16)
                          for i in range(VS_GROUP)]
                    for lv in range(nlv):
                        pr = [cb[i] * plsc.bitcast(rows_v[slot, g * VS_GROUP + i, c,
                                                          pl.ds(lv * SC_LANES, SC_LANES)], BF16)
                              for i in range(VS_GROUP)]
                        lo, hi = _sc_unpack(plsc.bitcast(pr[0] + pr[1], I32))
                        acc[lv] = acc[lv] + lo
                        acc[nlv + lv] = acc[nlv + lv] + hi
                    return tuple(acc)

                acc = lax.fori_loop(0, SC_ROWS // VS_GROUP, body, init)
                for p in range(2):
                    for lv in range(nlv):
                        y_v[tk, pl.ds((2 * c + p) * LANES + lv * SC_LANES, SC_LANES)] = acc[p * nlv + lv]

        @pl.loop(0, nchunk)
        def _(ch):
            t0 = tbase + ch * SC_TOK
            pltpu.sync_copy(idx_hbm.at[pl.ds(t0 * NSEL, SC_TOK * NSEL)], idx_v)
            pltpu.sync_copy(coef_hbm.at[pl.ds(t0, SC_TOK)], c_v)
            for s in range(SC_BUFS - 1):
                gather(s, s).start()

            @pl.loop(0, SC_TOK)
            def _(tk):
                for q in range(qsteps):
                    st = tk * qsteps + q
                    slot = q % SC_BUFS

                    @pl.when(st + SC_BUFS - 1 < nstep)
                    def _():
                        gather(st + SC_BUFS - 1, (q + SC_BUFS - 1) % SC_BUFS).start()

                    gather(st, slot).wait()
                    compute(tk, q, slot)

            pltpu.sync_copy(y_v, out_hbm.at[pl.ds(t0, SC_TOK)])

    return k(table3, idx, coefx)


def _coef_kernel(p_ref, w_ref, cx_ref, *, tt):
    row = lax.broadcasted_iota(I32, (LANES, NSEL), 0)
    col = lax.broadcasted_iota(I32, (LANES, NSEL), 1)
    per_row = LANES // SC_LANES
    act = jnp.zeros((tt, NSEL), F32)
    for s in range(PART_ROWS):
        fold = (col == per_row * s + row // SC_LANES).astype(BF16)
        p1, p2, p3 = _split3(p_ref[pl.ds(s, tt, stride=PART_ROWS), :])
        act = act + (jnp.dot(p1, fold, preferred_element_type=F32) + jnp.dot(p2, fold, preferred_element_type=F32)
                     + jnp.dot(p3, fold, preferred_element_type=F32))
    coef = w_ref[...] * (0.5 * act * (1.0 + lax.erf(act * (2.0 ** -0.5))))
    cb = coef.astype(BF16)
    for s in range(PART_ROWS):
        spread = (row == per_row * s + col // SC_LANES).astype(BF16)
        bits = pltpu.bitcast(jnp.dot(cb, spread, preferred_element_type=F32), I32)
        cx_ref[pl.ds(s, tt, stride=PART_ROWS), :] = bits | lax.shift_right_logical(bits, 16)


def _coef(p2d, w):
    T = p2d.shape[0] // PART_ROWS
    tt = 256
    return pl.pallas_call(
        functools.partial(_coef_kernel, tt=tt),
        grid=(T // tt,),
        in_specs=[pl.BlockSpec((tt * PART_ROWS, LANES), lambda i: (i, 0)),
                  pl.BlockSpec((tt, NSEL), lambda i: (i, 0))],
        out_specs=pl.BlockSpec((tt * PART_ROWS, LANES), lambda i: (i, 0)),
        out_shape=jax.ShapeDtypeStruct((T * PART_ROWS, LANES), I32),
        compiler_params=_cparams(("arbitrary",)),
        name="peer_coef",
    )(p2d, w)


SC_FRACTION = 0.375
SPLIT_UNIT = 1024
WD_BUFS = 4
WD_TOK = 16


def _sc_wdense(e_flat, w_flat, n_skip, n_tok, nexp):
    info = plsc.get_sparse_core_info()
    nw = info.num_cores * info.num_subcores
    tpw = n_tok // nw
    assert n_tok % nw == 0 and tpw % WD_TOK == 0 and WD_TOK % WD_BUFS == 0 and nexp % SC_LANES == 0
    nchunk = tpw // WD_TOK
    heads = NSEL // SC_LANES

    @functools.partial(
        pl.kernel, mesh=_sc_mesh(),
        out_type=jax.ShapeDtypeStruct((n_tok, nexp), F32),
        scratch_types=[pltpu.VMEM((WD_TOK * NSEL,), I32), pltpu.VMEM((WD_TOK * NSEL,), F32)]
        + [pltpu.VMEM((nexp,), F32) for _ in range(WD_BUFS)]
        + [pltpu.SemaphoreType.DMA((WD_BUFS,))],
        compiler_params=_sc_params(),
        name="sc_wdense",
    )
    def k(e_hbm, w_hbm, out_hbm, idx_v, w_v, *rest):
        rows, sem = rest[:WD_BUFS], rest[WD_BUFS]
        wid = lax.axis_index("s") * info.num_cores + lax.axis_index("c")
        tbase = wid * tpw
        zeros = jnp.zeros((SC_LANES,), F32)

        for s in range(WD_BUFS):
            @pl.loop(0, nexp // SC_LANES)
            def _(i):
                rows[s][pl.ds(i * SC_LANES, SC_LANES)] = zeros

        def put(tok, s):
            return pltpu.make_async_copy(rows[s], out_hbm.at[tok], sem.at[s])

        def scatter(tk, s, clear):
            for h in range(heads):
                sl = pl.ds(tk * NSEL + h * SC_LANES, SC_LANES)
                if clear:
                    plsc.store_scatter(rows[s], [idx_v[sl]], zeros)
                else:
                    plsc.addupdate_scatter(rows[s], [idx_v[sl]], w_v[sl])

        @pl.loop(0, nchunk)
        def _(ch):
            t0 = tbase + ch * WD_TOK
            pltpu.sync_copy(e_hbm.at[pl.ds((n_skip + t0) * NSEL, WD_TOK * NSEL)], idx_v)
            pltpu.sync_copy(w_hbm.at[pl.ds((n_skip + t0) * NSEL, WD_TOK * NSEL)], w_v)

            @pl.loop(0, WD_TOK, step=WD_BUFS)
            def _(tk0):
                for s in range(WD_BUFS):
                    tk = tk0 + s

                    @pl.when(tk0 > 0)
                    def _():
                        put(0, s).wait()
                        scatter(tk - WD_BUFS, s, True)

                    scatter(tk, s, False)
                    put(t0 + tk, s).start()

            for s in range(WD_BUFS):
                put(0, s).wait()
                scatter(WD_TOK - WD_BUFS + s, s, True)

    return k(e_flat, w_flat)


def _dense_kernel(hp_ref, wd_ref, u_ref, v_ref, y_ref, acc_ref):
    k = pl.program_id(1)

    @pl.when(k == 0)
    def _():
        acc_ref[...] = jnp.zeros_like(acc_ref)

    words = hp_ref[...]
    lo = pltpu.bitcast(lax.shift_left(words, 16), F32)
    hi = pltpu.bitcast(words & jnp.int32(-65536), F32)
    parts = []
    for c in range(WORDS):
        parts += [lo[:, c * LANES:(c + 1) * LANES], hi[:, c * LANES:(c + 1) * LANES]]
    hn = jnp.concatenate(parts, axis=-1).astype(BF16)
    act = _dot_nt(hn, u_ref[...])
    wd = wd_ref[...]
    coef = jnp.where(wd != 0.0, wd * (0.5 * act * (1.0 + lax.erf(act * (2.0 ** -0.5)))), 0.0)
    acc_ref[...] += jnp.dot(coef.astype(BF16), v_ref[...], preferred_element_type=F32)

    @pl.when(k == pl.num_programs(1) - 1)
    def _():
        y_ref[...] = acc_ref[...]


def _dense_experts(hp, wd, u_bf16, v_bf16, n_skip):
    n_tok, nexp = wd.shape
    dw = hp.shape[1]
    d = u_bf16.shape[1]
    tt = min(SPLIT_UNIT, n_tok)
    eb = 1024
    skip = n_skip // tt
    return pl.pallas_call(
        _dense_kernel,
        grid=(n_tok // tt, nexp // eb),
        in_specs=[pl.BlockSpec((tt, dw), lambda i, k: (skip + i, 0)),
                  pl.BlockSpec((tt, eb), lambda i, k: (i, k)),
                  pl.BlockSpec((eb, d), lambda i, k: (k, 0)),
                  pl.BlockSpec((eb, d), lambda i, k: (k, 0))],
        out_specs=pl.BlockSpec((tt, d), lambda i, k: (i, 0)),
        out_shape=jax.ShapeDtypeStruct((n_tok, d), F32),
        scratch_shapes=[pltpu.VMEM((tt, d), F32)],
        compiler_params=_cparams(("arbitrary", "arbitrary")),
        name="peer_dense",
    )(hp, wd, u_bf16, v_bf16)


def _final_kernel(h1_ref, ya_ref, yb_ref, g2_ref, fg_ref, fsh_ref, fsc_ref, *refs, na):
    o_ref = refs[-1]
    y = jnp.where(pl.program_id(1) < na, ya_ref[...], yb_ref[...])
    h = h1_ref[0] + g2_ref[0] * y
    o_ref[0] = _rms_mod(h, fg_ref[...], fsh_ref[0], fsc_ref[0])


def _final(h1, ya, yb, mod3, fmod3, final_g, out_prev, b0, nb):
    _, L, d = h1.shape
    tt = 512
    na, nbk = ya.shape[0] // tt, yb.shape[0] // tt
    in_specs = [pl.BlockSpec((1, tt, d), lambda b, j: (0, j, 0)),
                pl.BlockSpec((tt, d), lambda b, j: (jnp.minimum(j, na - 1), 0)),
                pl.BlockSpec((tt, d), lambda b, j: (jnp.clip(j - na, 0, nbk - 1), 0)),
                pl.BlockSpec((1, 1, d), lambda b, j: (0, 0, 5)),
                pl.BlockSpec((1, d), lambda b, j: (0, 0)),
                pl.BlockSpec((1, 1, d), lambda b, j: (0, 0, 0)),
                pl.BlockSpec((1, 1, d), lambda b, j: (0, 0, 1))]
    args = [h1, ya, yb, mod3, final_g.reshape(1, d), fmod3, fmod3]
    aliases = {}
    if out_prev is not None:
        in_specs.append(pl.BlockSpec(memory_space=pl.ANY))
        aliases = {len(args): 0}
        args.append(out_prev)
    return pl.pallas_call(
        functools.partial(_final_kernel, na=na),
        grid=(1, L // tt),
        in_specs=in_specs,
        out_specs=pl.BlockSpec((1, tt, d), lambda b, j: (b0, j, 0)),
        out_shape=jax.ShapeDtypeStruct((nb, L, d), F32),
        input_output_aliases=aliases,
        compiler_params=_cparams(("arbitrary", "arbitrary")),
        name="final_norm",
    )(*args)


def kernel(x, c, ada_w, ada_b, norm_mix_g, w_in, lb_gamma_fwd, lb_gamma_bwd, hgrn_norm_g, conv_w,
           conv_b, conv_ln_g, conv_ln_b, w_out, norm_ffn_g, peer_wq, peer_keys1, peer_keys2, peer_u,
           peer_v, final_ada_w, final_ada_b, final_norm_g):
    nb, L, d = x.shape
    assert ada_w.shape[0] == 1, "single-layer trunk"
    d_hgrn = lb_gamma_fwd.shape[1]

    c_pad = jnp.pad(c, ((0, SUBLANES - nb), (0, 0)))
    mod3 = _ada(c_pad, ada_w[0], ada_b[0])[:nb].reshape(nb, 1, 6 * d)
    fmod3 = _ada(c_pad, final_ada_w, final_ada_b)[:nb].reshape(nb, 1, 2 * d)

    d_conv = conv_w.shape[2]
    w_pad = jnp.pad(conv_w[0], ((0, 1), (0, 0)))
    w_in_bf, w_out_bf, wq_bf = w_in[0].astype(BF16), w_out[0].astype(BF16), peer_wq[0].astype(BF16)
    mb, u_rows = lax.optimization_barrier((mod3[0:1], _pack_table(peer_u[0])))
    nexp = peer_u.shape[1]
    n_sc = int(L * SC_FRACTION) // SPLIT_UNIT * SPLIT_UNIT
    u_bf, v_bf = peer_u[0].astype(BF16), peer_v[0].astype(BF16)
    v_src, v_rows = peer_v[0], None
    work = []

    def weighted_sum(item):
        return _sc_vsum(v_rows, item["idx"], item["cx"].reshape(n_sc, PART_ROWS, LANES))

    for b in range(nb):
        proj = _inproj(x, b, norm_mix_g[0], mb, w_in_bf)
        o_f, o_b = _hgrn(proj, lb_gamma_fwd, lb_gamma_bwd)
        o_c = _conv(proj, w_pad, conv_b[0], conv_ln_g[0], conv_ln_b[0], a_col=5 * d_hgrn // d_conv)
        h1, hp, scores = _mix(o_f, o_b, proj, o_c, x, b, hgrn_norm_g[0], mb, norm_ffn_g[0], w_out_bf,
                              wq_bf, peer_keys1[0], peer_keys2[0], g_col=4)
        e, w = _topk(scores)
        item = dict(h1=h1, mb=mb, hp=hp.reshape(L, d // 2))
        tied = dict(e=e, w=w)
        if b + 1 < nb:
            tied["mb"] = mod3[b + 1:b + 2]
        if work:
            prev = work[-1]
            tied["wd"] = prev["wd"]
            tied["cx"] = _coef(prev["p"].reshape(n_sc * PART_ROWS, LANES), prev["w"])
            if len(work) > 1:
                tied["y"] = work[-2]["y_sc"]
        elif nb > 1:
            tied["v"] = v_src
        tied = lax.optimization_barrier(tied)
        e, w, mb = tied["e"], tied["w"], tied.get("mb")
        if "v" in tied:
            v_rows = _pack_table(tied["v"])
        if work:
            prev["wd"], prev["cx"] = tied["wd"], tied["cx"]
            if "y" in tied:
                work[-2]["y_sc"] = tied["y"]
            prev["y_sc"] = weighted_sum(prev)
        item["idx"], item["w"] = e.reshape(L * NSEL), w
        item["p"] = _sc_udot(u_rows, item["idx"], item["hp"], n_sc)
        item["wd"] = _sc_wdense(item["idx"], w.reshape(L * NSEL), n_sc, L - n_sc, nexp)
        work.append(item)

    if v_rows is None:
        v_rows = _pack_table(v_src)
    y_dn = [_dense_experts(item["hp"], item["wd"], u_bf, v_bf, n_sc) for item in work[:-1]]
    last = work[-1]
    last["p"], y_dn = lax.optimization_barrier((last["p"], y_dn))
    last["cx"] = _coef(last["p"].reshape(n_sc * PART_ROWS, LANES), last["w"])
    last["y_sc"] = weighted_sum(last)
    y_dn.append(_dense_experts(last["hp"], last["wd"], u_bf, v_bf, n_sc))
    out = None
    for b, item in enumerate(work):
        out = _final(item["h1"], item["y_sc"], y_dn[b], item["mb"], fmod3[b:b + 1], final_norm_g, out, b, nb)
    return out
```

```python
import functools

import jax
import jax.numpy as jnp
from jax import lax
from jax.experimental import pallas as pl
from jax.experimental.pallas import tpu as pltpu
from jax.experimental.pallas import tpu_sc as plsc

F32 = jnp.float32
BF16 = jnp.bfloat16
I32 = jnp.int32

EPS = 1e-6
HEAD_DIM = 128
CHUNK = 64
CONV_WIDTH = 31
CONV_PAD = CONV_WIDTH // 2
PEER_HEADS = 8
PEER_KEYS = 128
PEER_TOPK = 16
PEER_HALF = 128
LANES = 128
SUBLANES = 8
VMEM_LIMIT = 48 * 1024 * 1024


def _cparams(sem):
    return pltpu.CompilerParams(dimension_semantics=sem, vmem_limit_bytes=VMEM_LIMIT)


def _silu(x):
    return x * jax.nn.sigmoid(x)


def _split3(x):
    x1 = x.astype(BF16)
    r1 = x - x1.astype(F32)
    x2 = r1.astype(BF16)
    x3 = (r1 - x2.astype(F32)).astype(BF16)
    return x1, x2, x3


def _dot_nt(a, b):
    return lax.dot_general(a, b, (((1,), (1,)), ((), ())), preferred_element_type=F32)


def _dot_tn(a, b):
    return lax.dot_general(a, b, (((0,), (0,)), ((), ())), preferred_element_type=F32)


def _ada_kernel(c_ref, w_ref, b_ref, o_ref):
    ca = _silu(c_ref[...])
    o_ref[...] = jnp.dot(ca, w_ref[...], preferred_element_type=F32,
                         precision=lax.Precision.HIGHEST) + b_ref[...]


def _ada(c_pad, w, b):
    d, n = w.shape
    tn = 1024
    return pl.pallas_call(
        _ada_kernel,
        grid=(n // tn,),
        in_specs=[pl.BlockSpec((SUBLANES, d), lambda j: (0, 0)),
                  pl.BlockSpec((d, tn), lambda j: (0, j)),
                  pl.BlockSpec((1, tn), lambda j: (0, j))],
        out_specs=pl.BlockSpec((SUBLANES, tn), lambda j: (0, j)),
        out_shape=jax.ShapeDtypeStruct((SUBLANES, n), F32),
        compiler_params=_cparams(("arbitrary",)),
        name="ada_proj",
    )(c_pad, w, b.reshape(1, n))


def _rms_mod(x, g, sh, sc):
    y = x * lax.rsqrt(jnp.mean(x * x, axis=-1, keepdims=True) + EPS) * g
    return y * (1.0 + sc) + sh


def _inproj_kernel(x_ref, g_ref, sh_ref, sc_ref, w_ref, o_ref):
    hn = _rms_mod(x_ref[0], g_ref[...], sh_ref[0], sc_ref[0])
    o_ref[0] = jnp.dot(hn.astype(BF16), w_ref[...], preferred_element_type=F32)


def _inproj(x, b0, g, mod3, w_bf16):
    _, L, d = x.shape
    n = w_bf16.shape[1]
    tt = 512
    return pl.pallas_call(
        _inproj_kernel,
        grid=(1, L // tt),
        in_specs=[pl.BlockSpec((1, tt, d), lambda b, j: (b0, j, 0)),
                  pl.BlockSpec((1, d), lambda b, j: (0, 0)),
                  pl.BlockSpec((1, 1, d), lambda b, j: (0, 0, 0)),
                  pl.BlockSpec((1, 1, d), lambda b, j: (0, 0, 1)),
                  pl.BlockSpec((d, n), lambda b, j: (0, 0))],
        out_specs=pl.BlockSpec((1, tt, n), lambda b, j: (0, j, 0)),
        out_shape=jax.ShapeDtypeStruct((1, L, n), F32),
        compiler_params=_cparams(("arbitrary", "arbitrary")),
        name="in_proj",
    )(x, g.reshape(1, d), mod3, mod3, w_bf16)


def _lower_bound(lbg):
    ex = jnp.exp(lbg - jnp.max(lbg, axis=0, keepdims=True))
    return ex[0:1] / jnp.sum(ex, axis=0, keepdims=True)


def _hgrn_chunk(q_ref, f_ref, i_ref, o_ref, ls, lb, st, c, reverse):
    row = lax.broadcasted_iota(I32, (CHUNK, CHUNK), 0)
    col = lax.broadcasted_iota(I32, (CHUNK, CHUNK), 1)
    keep = (col >= row) if reverse else (col <= row)
    tri = keep.astype(BF16)
    i_mid = CHUNK // 2 if reverse else CHUNK // 2 - 1
    i_end = 0 if reverse else CHUNK - 1
    r0 = pl.multiple_of(c * CHUNK, CHUNK)
    z = f_ref[0, pl.ds(r0, CHUNK), ls]
    q = _silu(q_ref[0, pl.ds(r0, CHUNK), ls])
    v = i_ref[0, pl.ds(r0, CHUNK), ls]
    logf = jnp.log(lb + (1.0 - lb) * jax.nn.sigmoid(z))
    k = (1.0 - lb) * jax.nn.sigmoid(-z)
    l1, l2, l3 = _split3(logf)
    cum = (jnp.dot(tri, l1, preferred_element_type=F32)
           + jnp.dot(tri, l2, preferred_element_type=F32)
           + jnp.dot(tri, l3, preferred_element_type=F32))
    c_mid = cum[i_mid:i_mid + 1]
    c_end = cum[i_end:i_end + 1]
    qr = (q * jnp.exp(cum - c_mid)).astype(BF16)
    kr = (k * jnp.exp(c_mid - cum)).astype(BF16)
    scores = jnp.where(keep, _dot_nt(qr, kr), 0.0)
    vb = v.astype(BF16)
    o_intra = jnp.dot(scores.astype(BF16), vb, preferred_element_type=F32)
    o_inter = _dot_nt((q * jnp.exp(cum)).astype(BF16), st.astype(BF16))
    o_ref[0, pl.ds(r0, CHUNK), ls] = o_intra + o_inter
    ku = (k * jnp.exp(c_end - cum)).astype(BF16)
    return st * jnp.exp(c_end) + _dot_tn(vb, ku)


HGRN_GROUP = 4


def _hgrn_kernel(qf_ref, ff_ref, if_ref, qb_ref, fb_ref, ib_ref, lbf_ref, lbb_ref, of_ref, ob_ref,
                 stf_ref, stb_ref, *, lblk):
    @pl.when(pl.program_id(2) == 0)
    def _():
        stf_ref[...] = jnp.zeros_like(stf_ref)
        stb_ref[...] = jnp.zeros_like(stb_ref)

    lanes = [slice(g * HEAD_DIM, (g + 1) * HEAD_DIM) for g in range(HGRN_GROUP)]
    lb_f = _lower_bound(lbf_ref[...])
    lb_b = _lower_bound(lbb_ref[...])
    nch = lblk // CHUNK

    def body(ci, carry):
        new = []
        for g, ls in enumerate(lanes):
            new.append(_hgrn_chunk(qf_ref, ff_ref, if_ref, of_ref, ls, lb_f[:, ls], carry[2 * g], ci, False))
            new.append(_hgrn_chunk(qb_ref, fb_ref, ib_ref, ob_ref, ls, lb_b[:, ls], carry[2 * g + 1],
                                   nch - 1 - ci, True))
        return tuple(new)

    init = tuple(ref[g] for g in range(HGRN_GROUP) for ref in (stf_ref, stb_ref))
    final = lax.fori_loop(0, nch, body, init)
    for g in range(HGRN_GROUP):
        stf_ref[g] = final[2 * g]
        stb_ref[g] = final[2 * g + 1]


def _hgrn(proj, lb_gamma_fwd, lb_gamma_bwd):
    nb, L, _ = proj.shape
    nh = lb_gamma_fwd.shape[1] // HEAD_DIM
    assert nh % HGRN_GROUP == 0
    ng = nh // HGRN_GROUP
    gw = HGRN_GROUP * HEAD_DIM
    lblk = min(1024, L)
    nblk = L // lblk

    def fwd(c0):
        return pl.BlockSpec((1, lblk, gw), lambda b, h, j: (b, j, c0 + h))

    def bwd(c0):
        return pl.BlockSpec((1, lblk, gw), lambda b, h, j: (b, nblk - 1 - j, c0 + h))

    lbs = pl.BlockSpec((lb_gamma_fwd.shape[0], gw), lambda b, h, j: (0, h))
    out = jax.ShapeDtypeStruct((nb, L, nh * HEAD_DIM), F32)
    state = pltpu.VMEM((HGRN_GROUP, HEAD_DIM, HEAD_DIM), F32)
    return pl.pallas_call(
        functools.partial(_hgrn_kernel, lblk=lblk),
        grid=(nb, ng, nblk),
        in_specs=[fwd(0), fwd(ng), fwd(3 * ng), bwd(0), bwd(2 * ng), bwd(3 * ng), lbs, lbs],
        out_specs=[fwd(0), bwd(0)],
        out_shape=[out, out],
        scratch_shapes=[state, state],
        compiler_params=_cparams(("arbitrary", "arbitrary", "arbitrary")),
        name="hgrn_bidir",
    )(proj, proj, proj, proj, proj, proj, lb_gamma_fwd, lb_gamma_bwd)


CONV_HALO = 16
CONV_ROWS = 64


def _conv_kernel(a_ref, g_ref, ap_ref, gp_ref, an_ref, gn_ref, w_ref, b_ref, lg_ref, lb_ref,
                 o_ref, hbuf, *, tl):
    j = pl.program_id(1)
    last = pl.num_programs(1) - 1
    hbuf[CONV_HALO:CONV_HALO + tl, :] = a_ref[0] * jax.nn.sigmoid(g_ref[0])
    hp = ap_ref[0] * jax.nn.sigmoid(gp_ref[0])
    hbuf[0:CONV_HALO, :] = jnp.where(j > 0, hp, 0.0)
    hn = an_ref[0] * jax.nn.sigmoid(gn_ref[0])
    hbuf[CONV_HALO + tl:2 * CONV_HALO + tl, :] = jnp.where(j < last, hn, 0.0)
    off = CONV_HALO - CONV_PAD
    for r in range(tl // CONV_ROWS):
        acc = jnp.zeros((CONV_ROWS, a_ref.shape[2]), F32)
        for k in range(CONV_WIDTH):
            s = r * CONV_ROWS + k + off
            acc = acc + w_ref[k:k + 1, :] * hbuf[s:s + CONV_ROWS, :]
        acc = acc + b_ref[...]
        mu = jnp.mean(acc, axis=-1, keepdims=True)
        cen = acc - mu
        var = jnp.mean(cen * cen, axis=-1, keepdims=True)
        y = cen * lax.rsqrt(var + EPS) * lg_ref[...] + lb_ref[...]
        o_ref[0, r * CONV_ROWS:(r + 1) * CONV_ROWS, :] = _silu(y)


def _conv(proj, w_pad, bias, ln_g, ln_b, a_col):
    nb, L, _ = proj.shape
    dc = w_pad.shape[1]
    tl = 256
    hb = tl // CONV_HALO
    nhalo = L // CONV_HALO

    def cur(c):
        return pl.BlockSpec((1, tl, dc), lambda b, j: (b, j, c))

    def prev(c):
        return pl.BlockSpec((1, CONV_HALO, dc), lambda b, j: (b, jnp.maximum(j * hb - 1, 0), c))

    def nxt(c):
        return pl.BlockSpec((1, CONV_HALO, dc),
                            lambda b, j: (b, jnp.minimum((j + 1) * hb, nhalo - 1), c))

    vec = pl.BlockSpec((1, dc), lambda b, j: (0, 0))
    return pl.pallas_call(
        functools.partial(_conv_kernel, tl=tl),
        grid=(nb, L // tl),
        in_specs=[cur(a_col), cur(a_col + 1), prev(a_col), prev(a_col + 1), nxt(a_col),
                  nxt(a_col + 1), pl.BlockSpec(w_pad.shape, lambda b, j: (0, 0)), vec, vec, vec],
        out_specs=pl.BlockSpec((1, tl, dc), lambda b, j: (b, j, 0)),
        out_shape=jax.ShapeDtypeStruct((nb, L, dc), F32),
        scratch_shapes=[pltpu.VMEM((tl + 2 * CONV_HALO, dc), F32)],
        compiler_params=_cparams(("arbitrary", "arbitrary")),
        name="conv_group",
    )(proj, proj, proj, proj, proj, proj, w_pad, bias.reshape(1, dc), ln_g.reshape(1, dc),
      ln_b.reshape(1, dc))


def _mix_kernel(of_ref, ob_ref, gr_ref, oc_ref, x_ref, hg_ref, g1_ref, sh2_ref, sc2_ref, ng_ref,
                wo_ref, wq_ref, k1_ref, k2_ref, h1_ref, hp_ref, s_ref):
    o = of_ref[0] + ob_ref[0]
    parts = []
    for hh in range(o.shape[1] // HEAD_DIM):
        oh = o[:, hh * HEAD_DIM:(hh + 1) * HEAD_DIM]
        parts.append(oh * lax.rsqrt(jnp.mean(oh * oh, axis=-1, keepdims=True) + EPS))
    on = jnp.concatenate(parts, axis=-1) * hg_ref[...] * _silu(gr_ref[0])
    cat = jnp.concatenate([on, oc_ref[0]], axis=-1).astype(BF16)
    mix = jnp.dot(cat, wo_ref[...], preferred_element_type=F32)
    h1 = x_ref[0] + g1_ref[0] * mix
    h1_ref[0] = h1
    hn2 = _rms_mod(h1, ng_ref[...], sh2_ref[0], sc2_ref[0])
    hp_ref[0] = _pack_words(hn2)
    q = jnp.dot(hn2.astype(BF16), wq_ref[...], preferred_element_type=F32)
    k1h, k1l, _ = _split3(k1_ref[...])
    k2h, k2l, _ = _split3(k2_ref[...])
    for hd in range(PEER_HEADS):
        for half, (kh, kl) in enumerate(((k1h, k1l), (k2h, k2l))):
            c0 = hd * 2 * PEER_HALF + half * PEER_HALF
            qh, ql, _ = _split3(q[:, c0:c0 + PEER_HALF])
            s_ref[2 * hd + half] = _dot_nt(kh, qh) + _dot_nt(kh, ql) + _dot_nt(kl, qh)


def _mix(o_f, o_b, proj, o_c, x, b0, hgrn_g, mod3, norm_g, w_out_bf16, wq_bf16, keys1, keys2, g_col):
    _, L, d = x.shape
    dh = o_f.shape[2]
    dq = wq_bf16.shape[1]
    tt = 256
    nj = L // tt

    def half(c=0):
        return pl.BlockSpec((1, tt, dh), lambda b, j: (0, j, c))

    def full(bb=0):
        return pl.BlockSpec((1, tt, d), lambda b, j: (bb, j, 0))

    def modc(c):
        return pl.BlockSpec((1, 1, d), lambda b, j: (0, 0, c))

    def const(shape):
        return pl.BlockSpec(shape, lambda b, j: (0,) * len(shape))

    return pl.pallas_call(
        _mix_kernel,
        grid=(1, nj),
        in_specs=[half(), half(), half(g_col), half(), full(b0), const((1, dh)),
                  modc(2), modc(3), modc(4), const((1, d)), const((d, d)), const((d, dq)),
                  const(keys1.shape), const(keys2.shape)],
        out_specs=[full(), pl.BlockSpec((1, tt, d // 2), lambda b, j: (0, j, 0)),
                   pl.BlockSpec((2 * PEER_HEADS, PEER_KEYS, tt), lambda b, j: (0, 0, j))],
        out_shape=[jax.ShapeDtypeStruct((1, L, d), F32), jax.ShapeDtypeStruct((1, L, d // 2), I32),
                   jax.ShapeDtypeStruct((2 * PEER_HEADS, PEER_KEYS, L), F32)],
        compiler_params=_cparams(("arbitrary", "arbitrary")),
        name="mix_scores",
    )(o_f, o_b, proj, o_c, x, hgrn_g.reshape(1, dh), mod3, mod3, mod3, norm_g.reshape(1, d),
      w_out_bf16, wq_bf16, keys1, keys2)


def _top16(s, payload=None):
    n = s.shape[0]
    iota = lax.broadcasted_iota(I32, s.shape, 0).astype(F32)
    vals, tags = [], []
    for _ in range(PEER_TOPK):
        m = jnp.max(s, axis=0, keepdims=True)
        idx = jnp.min(jnp.where(s == m, iota, float(n)), axis=0, keepdims=True)
        hit = iota == idx
        vals.append(m)
        tags.append(idx if payload is None else jnp.max(jnp.where(hit, payload, -1.0), axis=0, keepdims=True))
        s = jnp.where(hit, -jnp.inf, s)
    return jnp.concatenate(vals, axis=0), jnp.concatenate(tags, axis=0)


def _pruned_pairs(x1, x2, combine):
    rows = [combine(x1[a:a + 1], x2[0:PEER_TOPK // (a + 1)]) for a in range(PEER_TOPK // 2)]
    rows.append(combine(x1[PEER_TOPK // 2:], x2[0:1]))
    return jnp.concatenate(rows, axis=0)


def _topk_kernel(s_ref, e_ref, w_ref):
    es, ws = [], []
    for hd in range(PEER_HEADS):
        v1, i1 = _top16(s_ref[2 * hd])
        v2, i2 = _top16(s_ref[2 * hd + 1])
        cand = _pruned_pairs(v1, v2, lambda a, b: a + b)
        cand_e = _pruned_pairs(i1, i2, lambda a, b: a * float(PEER_KEYS) + b)
        sc, e = _top16(cand, cand_e)
        p = jnp.exp(sc - sc[0:1])
        es.append(e.astype(I32))
        ws.append(p / jnp.sum(p, axis=0, keepdims=True))
    e_ref[...] = jnp.concatenate(es, axis=0).T
    w_ref[...] = jnp.concatenate(ws, axis=0).T


def _topk(scores):
    npair, nk, T = scores.shape
    nsel = PEER_HEADS * PEER_TOPK
    tt = 256
    return pl.pallas_call(
        _topk_kernel,
        grid=(T // tt,),
        in_specs=[pl.BlockSpec((npair, nk, tt), lambda i: (0, 0, i))],
        out_specs=[pl.BlockSpec((tt, nsel), lambda i: (i, 0)),
                   pl.BlockSpec((tt, nsel), lambda i: (i, 0))],
        out_shape=[jax.ShapeDtypeStruct((T, nsel), I32), jax.ShapeDtypeStruct((T, nsel), F32)],
        compiler_params=_cparams(("arbitrary",)),
        name="peer_topk",
    )(scores)


WORDS = 4


def _pack_words(x):
    bits = pltpu.bitcast(x.astype(BF16).astype(F32), I32)
    out = []
    for c in range(WORDS):
        lo = bits[:, (2 * c) * LANES:(2 * c + 1) * LANES]
        hi = bits[:, (2 * c + 1) * LANES:(2 * c + 2) * LANES]
        out.append(lax.shift_right_logical(lo, 16) | (hi & jnp.int32(-65536)))
    return jnp.concatenate(out, axis=-1)


def _pack_kernel(x_ref, o_ref):
    te = x_ref.shape[0]
    words = _pack_words(x_ref[...])
    for c in range(WORDS):
        o_ref[pl.ds(c, te, stride=WORDS), :] = words[:, c * LANES:(c + 1) * LANES]


def _pack_table(x):
    ne, d = x.shape
    te = 512
    packed = pl.pallas_call(
        _pack_kernel,
        grid=(ne // te,),
        in_specs=[pl.BlockSpec((te, d), lambda i: (i, 0))],
        out_specs=pl.BlockSpec((te * WORDS, LANES), lambda i: (i, 0)),
        out_shape=jax.ShapeDtypeStruct((ne * WORDS, LANES), I32),
        compiler_params=_cparams(("arbitrary",)),
        name="pack_table",
    )(x)
    return packed.reshape(ne, WORDS, LANES)


NSEL = PEER_HEADS * PEER_TOPK
ROW_TILE = 2 * WORDS


SC_LANES = 16
SC_ROWS = 32
SC_BUFS = 4
SC_TOK = 16
SC_GROUP = 8
VS_GROUP = 2
PART_ROWS = NSEL * SC_LANES // LANES


def _sc_unpack(w):
    lo = plsc.bitcast(lax.shift_left(w, jnp.int32(16)), F32)
    hi = plsc.bitcast(w & jnp.int32(-65536), F32)
    return lo, hi


def _sc_mesh():
    return plsc.VectorSubcoreMesh(core_axis_name="c", subcore_axis_name="s")


def _sc_params():
    return pltpu.CompilerParams(use_tc_tiling_on_sc=True, needs_layout_passes=False)


def _sc_udot(table3, idx, hn3, n_tok):
    T = n_tok
    info = plsc.get_sparse_core_info()
    nw = info.num_cores * info.num_subcores
    tpw = T // nw
    assert T % nw == 0 and tpw % SC_TOK == 0 and NSEL % SC_ROWS == 0
    nchunk = tpw // SC_TOK
    qsteps = NSEL // SC_ROWS
    nstep = SC_TOK * qsteps

    @functools.partial(
        pl.kernel, mesh=_sc_mesh(),
        out_type=jax.ShapeDtypeStruct((T, PART_ROWS, LANES), F32),
        scratch_types=[pltpu.VMEM((SC_TOK * NSEL,), I32),
                       pltpu.VMEM((SC_BUFS, SC_ROWS, WORDS, LANES), I32),
                       pltpu.VMEM((SC_TOK, WORDS * LANES), I32),
                       pltpu.VMEM((SC_TOK, PART_ROWS, LANES), F32),
                       pltpu.SemaphoreType.DMA((SC_BUFS,))],
        compiler_params=_sc_params(),
        name="sc_udot",
    )
    def k(table_hbm, idx_hbm, hn_hbm, out_hbm, idx_v, rows_v, h_v, p_v, sem_g):
        wid = lax.axis_index("s") * info.num_cores + lax.axis_index("c")
        tbase = wid * tpw

        def gather(st, slot):
            return pltpu.make_async_copy(table_hbm.at[idx_v.at[pl.ds(st * SC_ROWS, SC_ROWS)]],
                                         rows_v.at[slot], sem_g.at[slot])

        def compute(tk, q, slot):
            @pl.loop(0, SC_ROWS // SC_GROUP)
            def _(g):
                acc = [jnp.zeros((SC_LANES,), F32) for _ in range(SC_GROUP)]
                for c in range(WORDS):
                    for lv in range(0, LANES // SC_LANES, 2):
                        ls = [pl.ds((lv + d) * SC_LANES, SC_LANES) for d in range(2)]
                        hb = [plsc.bitcast(h_v[tk, pl.ds(c * LANES + (lv + d) * SC_LANES, SC_LANES)], BF16)
                              for d in range(2)]
                        for i in range(SC_GROUP):
                            pr = [plsc.bitcast(rows_v[slot, g * SC_GROUP + i, c, ls[d]], BF16) * hb[d]
                                  for d in range(2)]
                            lo, hi = _sc_unpack(plsc.bitcast(pr[0] + pr[1], I32))
                            acc[i] = acc[i] + lo + hi
                for i in range(SC_GROUP):
                    p_v[tk, q * (SC_ROWS // SC_GROUP) + g, pl.ds(i * SC_LANES, SC_LANES)] = acc[i]

        @pl.loop(0, nchunk)
        def _(ch):
            t0 = tbase + ch * SC_TOK
            pltpu.sync_copy(idx_hbm.at[pl.ds(t0 * NSEL, SC_TOK * NSEL)], idx_v)
            pltpu.sync_copy(hn_hbm.at[pl.ds(t0, SC_TOK)], h_v)
            for s in range(SC_BUFS - 1):
                gather(s, s).start()

            @pl.loop(0, SC_TOK)
            def _(tk):
                for q in range(qsteps):
                    st = tk * qsteps + q
                    slot = q % SC_BUFS

                    @pl.when(st + SC_BUFS - 1 < nstep)
                    def _():
                        gather(st + SC_BUFS - 1, (q + SC_BUFS - 1) % SC_BUFS).start()

                    gather(st, slot).wait()
                    compute(tk, q, slot)

            pltpu.sync_copy(p_v, out_hbm.at[pl.ds(t0, SC_TOK)])

    return k(table3, idx, hn3)


def _sc_vsum(table3, idx, coefx):
    assert VS_GROUP == 2, "the loop body sums the products of exactly two experts in bf16"
    T = coefx.shape[0]
    info = plsc.get_sparse_core_info()
    nw = info.num_cores * info.num_subcores
    tpw = T // nw
    assert T % nw == 0 and tpw % SC_TOK == 0 and NSEL % SC_ROWS == 0 and SC_GROUP % VS_GROUP == 0
    nchunk = tpw // SC_TOK
    qsteps = NSEL // SC_ROWS
    nstep = SC_TOK * qsteps
    nlv = LANES // SC_LANES
    sub = SC_GROUP // VS_GROUP

    @functools.partial(
        pl.kernel, mesh=_sc_mesh(),
        out_type=jax.ShapeDtypeStruct((T, ROW_TILE * LANES), F32),
        scratch_types=[pltpu.VMEM((SC_TOK * NSEL,), I32),
                       pltpu.VMEM((SC_BUFS, SC_ROWS, WORDS, LANES), I32),
                       pltpu.VMEM((SC_TOK, PART_ROWS, LANES), I32),
                       pltpu.VMEM((SC_TOK, ROW_TILE * LANES), F32),
                       pltpu.SemaphoreType.DMA((SC_BUFS,))],
        compiler_params=_sc_params(),
        name="sc_vsum",
    )
    def k(table_hbm, idx_hbm, coef_hbm, out_hbm, idx_v, rows_v, c_v, y_v, sem_g):
        wid = lax.axis_index("s") * info.num_cores + lax.axis_index("c")
        tbase = wid * tpw

        def gather(st, slot):
            return pltpu.make_async_copy(table_hbm.at[idx_v.at[pl.ds(st * SC_ROWS, SC_ROWS)]],
                                         rows_v.at[slot], sem_g.at[slot])

        def compute(tk, q, slot):
            for c in range(WORDS):
                if q == 0:
                    init = tuple(jnp.zeros((SC_LANES,), F32) for _ in range(2 * nlv))
                else:
                    init = tuple(y_v[tk, pl.ds((2 * c + p) * LANES + lv * SC_LANES, SC_LANES)]
                                 for p in range(2) for lv in range(nlv))

                def body(g, acc):
                    acc = list(acc)
                    cb = [plsc.bitcast(c_v[tk, q * (SC_ROWS // SC_GROUP) + g // sub,
                                           pl.ds(((g % sub) * VS_GROUP + i) * SC_LANES, SC_LANES)], BF16)
                          for i in range(VS_GROUP)]
                    for lv in range(nlv):
                        pr = [cb[i] * plsc.bitcast(rows_v[slot, g * VS_GROUP + i, c,
                                                          pl.ds(lv * SC_LANES, SC_LANES)], BF16)
                              for i in range(VS_GROUP)]
                        lo, hi = _sc_unpack(plsc.bitcast(pr[0] + pr[1], I32))
                        acc[lv] = acc[lv] + lo
                        acc[nlv + lv] = acc[nlv + lv] + hi
                    return tuple(acc)

                acc = lax.fori_loop(0, SC_ROWS // VS_GROUP, body, init)
                for p in range(2):
                    for lv in range(nlv):
                        y_v[tk, pl.ds((2 * c + p) * LANES + lv * SC_LANES, SC_LANES)] = acc[p * nlv + lv]

        @pl.loop(0, nchunk)
        def _(ch):
            t0 = tbase + ch * SC_TOK
            pltpu.sync_copy(idx_hbm.at[pl.ds(t0 * NSEL, SC_TOK * NSEL)], idx_v)
            pltpu.sync_copy(coef_hbm.at[pl.ds(t0, SC_TOK)], c_v)
            for s in range(SC_BUFS - 1):
                gather(s, s).start()

            @pl.loop(0, SC_TOK)
            def _(tk):
                for q in range(qsteps):
                    st = tk * qsteps + q
                    slot = q % SC_BUFS

                    @pl.when(st + SC_BUFS - 1 < nstep)
                    def _():
                        gather(st + SC_BUFS - 1, (q + SC_BUFS - 1) % SC_BUFS).start()

                    gather(st, slot).wait()
                    compute(tk, q, slot)

            pltpu.sync_copy(y_v, out_hbm.at[pl.ds(t0, SC_TOK)])

    return k(table3, idx, coefx)


def _coef_kernel(p_ref, w_ref, cx_ref, *, tt):
    row = lax.broadcasted_iota(I32, (LANES, NSEL), 0)
    col = lax.broadcasted_iota(I32, (LANES, NSEL), 1)
    per_row = LANES // SC_LANES
    act = jnp.zeros((tt, NSEL), F32)
    for s in range(PART_ROWS):
        fold = (col == per_row * s + row // SC_LANES).astype(BF16)
        p1, p2, p3 = _split3(p_ref[pl.ds(s, tt, stride=PART_ROWS), :])
        act = act + (jnp.dot(p1, fold, preferred_element_type=F32) + jnp.dot(p2, fold, preferred_element_type=F32)
                     + jnp.dot(p3, fold, preferred_element_type=F32))
    coef = w_ref[...] * (0.5 * act * (1.0 + lax.erf(act * (2.0 ** -0.5))))
    cb = coef.astype(BF16)
    for s in range(PART_ROWS):
        spread = (row == per_row * s + col // SC_LANES).astype(BF16)
        bits = pltpu.bitcast(jnp.dot(cb, spread, preferred_element_type=F32), I32)
        cx_ref[pl.ds(s, tt, stride=PART_ROWS), :] = bits | lax.shift_right_logical(bits, 16)


def _coef(p2d, w):
    T = p2d.shape[0] // PART_ROWS
    tt = 256
    return pl.pallas_call(
        functools.partial(_coef_kernel, tt=tt),
        grid=(T // tt,),
        in_specs=[pl.BlockSpec((tt * PART_ROWS, LANES), lambda i: (i, 0)),
                  pl.BlockSpec((tt, NSEL), lambda i: (i, 0))],
        out_specs=pl.BlockSpec((tt * PART_ROWS, LANES), lambda i: (i, 0)),
        out_shape=jax.ShapeDtypeStruct((T * PART_ROWS, LANES), I32),
        compiler_params=_cparams(("arbitrary",)),
        name="peer_coef",
    )(p2d, w)


SC_FRACTION = 0.375
SPLIT_UNIT = 1024
WD_BUFS = 4
WD_TOK = 16


def _sc_wdense(e_flat, w_flat, n_skip, n_tok, nexp):
    info = plsc.get_sparse_core_info()
    nw = info.num_cores * info.num_subcores
    tpw = n_tok // nw
    assert n_tok % nw == 0 and tpw % WD_TOK == 0 and WD_TOK % WD_BUFS == 0 and nexp % SC_LANES == 0
    nchunk = tpw // WD_TOK
    heads = NSEL // SC_LANES

    @functools.partial(
        pl.kernel, mesh=_sc_mesh(),
        out_type=jax.ShapeDtypeStruct((n_tok, nexp), F32),
        scratch_types=[pltpu.VMEM((WD_TOK * NSEL,), I32), pltpu.VMEM((WD_TOK * NSEL,), F32)]
        + [pltpu.VMEM((nexp,), F32) for _ in range(WD_BUFS)]
        + [pltpu.SemaphoreType.DMA((WD_BUFS,))],
        compiler_params=_sc_params(),
        name="sc_wdense",
    )
    def k(e_hbm, w_hbm, out_hbm, idx_v, w_v, *rest):
        rows, sem = rest[:WD_BUFS], rest[WD_BUFS]
        wid = lax.axis_index("s") * info.num_cores + lax.axis_index("c")
        tbase = wid * tpw
        zeros = jnp.zeros((SC_LANES,), F32)

        for s in range(WD_BUFS):
            @pl.loop(0, nexp // SC_LANES)
            def _(i):
                rows[s][pl.ds(i * SC_LANES, SC_LANES)] = zeros

        def put(tok, s):
            return pltpu.make_async_copy(rows[s], out_hbm.at[tok], sem.at[s])

        def scatter(tk, s, clear):
            for h in range(heads):
                sl = pl.ds(tk * NSEL + h * SC_LANES, SC_LANES)
                if clear:
                    plsc.store_scatter(rows[s], [idx_v[sl]], zeros)
                else:
                    plsc.addupdate_scatter(rows[s], [idx_v[sl]], w_v[sl])

        @pl.loop(0, nchunk)
        def _(ch):
            t0 = tbase + ch * WD_TOK
            pltpu.sync_copy(e_hbm.at[pl.ds((n_skip + t0) * NSEL, WD_TOK * NSEL)], idx_v)
            pltpu.sync_copy(w_hbm.at[pl.ds((n_skip + t0) * NSEL, WD_TOK * NSEL)], w_v)

            @pl.loop(0, WD_TOK, step=WD_BUFS)
            def _(tk0):
                for s in range(WD_BUFS):
                    tk = tk0 + s

                    @pl.when(tk0 > 0)
                    def _():
                        put(0, s).wait()
                        scatter(tk - WD_BUFS, s, True)

                    scatter(tk, s, False)
                    put(t0 + tk, s).start()

            for s in range(WD_BUFS):
                put(0, s).wait()
                scatter(WD_TOK - WD_BUFS + s, s, True)

    return k(e_flat, w_flat)


def _dense_kernel(hp_ref, wd_ref, u_ref, v_ref, y_ref, acc_ref):
    k = pl.program_id(1)

    @pl.when(k == 0)
    def _():
        acc_ref[...] = jnp.zeros_like(acc_ref)

    words = hp_ref[...]
    lo = pltpu.bitcast(lax.shift_left(words, 16), F32)
    hi = pltpu.bitcast(words & jnp.int32(-65536), F32)
    parts = []
    for c in range(WORDS):
        parts += [lo[:, c * LANES:(c + 1) * LANES], hi[:, c * LANES:(c + 1) * LANES]]
    hn = jnp.concatenate(parts, axis=-1).astype(BF16)
    act = _dot_nt(hn, u_ref[...])
    wd = wd_ref[...]
    coef = jnp.where(wd != 0.0, wd * (0.5 * act * (1.0 + lax.erf(act * (2.0 ** -0.5)))), 0.0)
    acc_ref[...] += jnp.dot(coef.astype(BF16), v_ref[...], preferred_element_type=F32)

    @pl.when(k == pl.num_programs(1) - 1)
    def _():
        y_ref[...] = acc_ref[...]


def _dense_experts(hp, wd, u_bf16, v_bf16, n_skip):
    n_tok, nexp = wd.shape
    dw = hp.shape[1]
    d = u_bf16.shape[1]
    tt = min(SPLIT_UNIT, n_tok)
    eb = 1024
    skip = n_skip // tt
    return pl.pallas_call(
        _dense_kernel,
        grid=(n_tok // tt, nexp // eb),
        in_specs=[pl.BlockSpec((tt, dw), lambda i, k: (skip + i, 0)),
                  pl.BlockSpec((tt, eb), lambda i, k: (i, k)),
                  pl.BlockSpec((eb, d), lambda i, k: (k, 0)),
                  pl.BlockSpec((eb, d), lambda i, k: (k, 0))],
        out_specs=pl.BlockSpec((tt, d), lambda i, k: (i, 0)),
        out_shape=jax.ShapeDtypeStruct((n_tok, d), F32),
        scratch_shapes=[pltpu.VMEM((tt, d), F32)],
        compiler_params=_cparams(("arbitrary", "arbitrary")),
        name="peer_dense",
    )(hp, wd, u_bf16, v_bf16)


def _final_kernel(h1_ref, ya_ref, yb_ref, g2_ref, fg_ref, fsh_ref, fsc_ref, *refs, na):
    o_ref = refs[-1]
    y = jnp.where(pl.program_id(1) < na, ya_ref[...], yb_ref[...])
    h = h1_ref[0] + g2_ref[0] * y
    o_ref[0] = _rms_mod(h, fg_ref[...], fsh_ref[0], fsc_ref[0])


def _final(h1, ya, yb, mod3, fmod3, final_g, out_prev, b0, nb):
    _, L, d = h1.shape
    tt = 512
    na, nbk = ya.shape[0] // tt, yb.shape[0] // tt
    in_specs = [pl.BlockSpec((1, tt, d), lambda b, j: (0, j, 0)),
                pl.BlockSpec((tt, d), lambda b, j: (jnp.minimum(j, na - 1), 0)),
                pl.BlockSpec((tt, d), lambda b, j: (jnp.clip(j - na, 0, nbk - 1), 0)),
                pl.BlockSpec((1, 1, d), lambda b, j: (0, 0, 5)),
                pl.BlockSpec((1, d), lambda b, j: (0, 0)),
                pl.BlockSpec((1, 1, d), lambda b, j: (0, 0, 0)),
                pl.BlockSpec((1, 1, d), lambda b, j: (0, 0, 1))]
    args = [h1, ya, yb, mod3, final_g.reshape(1, d), fmod3, fmod3]
    aliases = {}
    if out_prev is not None:
        in_specs.append(pl.BlockSpec(memory_space=pl.ANY))
        aliases = {len(args): 0}
        args.append(out_prev)
    return pl.pallas_call(
        functools.partial(_final_kernel, na=na),
        grid=(1, L // tt),
        in_specs=in_specs,
        out_specs=pl.BlockSpec((1, tt, d), lambda b, j: (b0, j, 0)),
        out_shape=jax.ShapeDtypeStruct((nb, L, d), F32),
        input_output_aliases=aliases,
        compiler_params=_cparams(("arbitrary", "arbitrary")),
        name="final_norm",
    )(*args)


def kernel(x, c, ada_w, ada_b, norm_mix_g, w_in, lb_gamma_fwd, lb_gamma_bwd, hgrn_norm_g, conv_w,
           conv_b, conv_ln_g, conv_ln_b, w_out, norm_ffn_g, peer_wq, peer_keys1, peer_keys2, peer_u,
           peer_v, final_ada_w, final_ada_b, final_norm_g):
    nb, L, d = x.shape
    assert ada_w.shape[0] == 1, "single-layer trunk"
    d_hgrn = lb_gamma_fwd.shape[1]

    c_pad = jnp.pad(c, ((0, SUBLANES - nb), (0, 0)))
    mod3 = _ada(c_pad, ada_w[0], ada_b[0])[:nb].reshape(nb, 1, 6 * d)
    fmod3 = _ada(c_pad, final_ada_w, final_ada_b)[:nb].reshape(nb, 1, 2 * d)

    d_conv = conv_w.shape[2]
    w_pad = jnp.pad(conv_w[0], ((0, 1), (0, 0)))
    w_in_bf, w_out_bf, wq_bf = w_in[0].astype(BF16), w_out[0].astype(BF16), peer_wq[0].astype(BF16)
    mb, u_rows = lax.optimization_barrier((mod3[0:1], _pack_table(peer_u[0])))
    nexp = peer_u.shape[1]
    n_sc = int(L * SC_FRACTION) // SPLIT_UNIT * SPLIT_UNIT
    u_bf, v_bf = peer_u[0].astype(BF16), peer_v[0].astype(BF16)
    v_src, v_rows = peer_v[0], None
    work = []

    def weighted_sum(item):
        return _sc_vsum(v_rows, item["idx"], item["cx"].reshape(n_sc, PART_ROWS, LANES))

    for b in range(nb):
        proj = _inproj(x, b, norm_mix_g[0], mb, w_in_bf)
        o_f, o_b = _hgrn(proj, lb_gamma_fwd, lb_gamma_bwd)
        o_c = _conv(proj, w_pad, conv_b[0], conv_ln_g[0], conv_ln_b[0], a_col=5 * d_hgrn // d_conv)
        h1, hp, scores = _mix(o_f, o_b, proj, o_c, x, b, hgrn_norm_g[0], mb, norm_ffn_g[0], w_out_bf,
                              wq_bf, peer_keys1[0], peer_keys2[0], g_col=4)
        e, w = _topk(scores)
        item = dict(h1=h1, mb=mb, hp=hp.reshape(L, d // 2))
        tied = dict(e=e, w=w)
        if b + 1 < nb:
            tied["mb"] = mod3[b + 1:b + 2]
        if work:
            prev = work[-1]
            tied["wd"] = prev["wd"]
            tied["cx"] = _coef(prev["p"].reshape(n_sc * PART_ROWS, LANES), prev["w"])
            if len(work) > 1:
                tied["y"] = work[-2]["y_sc"]
        elif nb > 1:
            tied["v"] = v_src
        tied = lax.optimization_barrier(tied)
        e, w, mb = tied["e"], tied["w"], tied.get("mb")
        if "v" in tied:
            v_rows = _pack_table(tied["v"])
        if work:
            prev["wd"], prev["cx"] = tied["wd"], tied["cx"]
            if "y" in tied:
                work[-2]["y_sc"] = tied["y"]
            prev["y_sc"] = weighted_sum(prev)
        item["idx"], item["w"] = e.reshape(L * NSEL), w
        item["p"] = _sc_udot(u_rows, item["idx"], item["hp"], n_sc)
        item["wd"] = _sc_wdense(item["idx"], w.reshape(L * NSEL), n_sc, L - n_sc, nexp)
        work.append(item)

    if v_rows is None:
        v_rows = _pack_table(v_src)
    y_dn = [_dense_experts(item["hp"], item["wd"], u_bf, v_bf, n_sc) for item in work[:-1]]
    last = work[-1]
    last["p"], y_dn = lax.optimization_barrier((last["p"], y_dn))
    last["cx"] = _coef(last["p"].reshape(n_sc * PART_ROWS, LANES), last["w"])
    last["y_sc"] = weighted_sum(last)
    y_dn.append(_dense_experts(last["hp"], last["wd"], u_bf, v_bf, n_sc))
    out = None
    for b, item in enumerate(work):
        out = _final(item["h1"], item["y_sc"], y_dn[b], item["mb"], fmod3[b:b + 1], final_norm_g, out, b, nb)
    return out
```

```python
import functools

import jax
import jax.numpy as jnp
from jax import lax
from jax.experimental import pallas as pl
from jax.experimental.pallas import tpu as pltpu
from jax.experimental.pallas import tpu_sc as plsc

F32 = jnp.float32
BF16 = jnp.bfloat16
I32 = jnp.int32

EPS = 1e-6
HEAD_DIM = 128
CHUNK = 64
CONV_WIDTH = 31
CONV_PAD = CONV_WIDTH // 2
PEER_HEADS = 8
PEER_KEYS = 128
PEER_TOPK = 16
PEER_HALF = 128
LANES = 128
SUBLANES = 8
VMEM_LIMIT = 48 * 1024 * 1024


def _cparams(sem):
    return pltpu.CompilerParams(dimension_semantics=sem, vmem_limit_bytes=VMEM_LIMIT)


def _silu(x):
    return x * jax.nn.sigmoid(x)


def _split3(x):
    x1 = x.astype(BF16)
    r1 = x - x1.astype(F32)
    x2 = r1.astype(BF16)
    x3 = (r1 - x2.astype(F32)).astype(BF16)
    return x1, x2, x3


def _dot_nt(a, b):
    return lax.dot_general(a, b, (((1,), (1,)), ((), ())), preferred_element_type=F32)


def _dot_tn(a, b):
    return lax.dot_general(a, b, (((0,), (0,)), ((), ())), preferred_element_type=F32)


def _ada_kernel(c_ref, w_ref, b_ref, o_ref):
    ca = _silu(c_ref[...])
    o_ref[...] = jnp.dot(ca, w_ref[...], preferred_element_type=F32,
                         precision=lax.Precision.HIGHEST) + b_ref[...]


def _ada(c_pad, w, b):
    d, n = w.shape
    tn = 1024
    return pl.pallas_call(
        _ada_kernel,
        grid=(n // tn,),
        in_specs=[pl.BlockSpec((SUBLANES, d), lambda j: (0, 0)),
                  pl.BlockSpec((d, tn), lambda j: (0, j)),
                  pl.BlockSpec((1, tn), lambda j: (0, j))],
        out_specs=pl.BlockSpec((SUBLANES, tn), lambda j: (0, j)),
        out_shape=jax.ShapeDtypeStruct((SUBLANES, n), F32),
        compiler_params=_cparams(("arbitrary",)),
        name="ada_proj",
    )(c_pad, w, b.reshape(1, n))


def _rms_mod(x, g, sh, sc):
    y = x * lax.rsqrt(jnp.mean(x * x, axis=-1, keepdims=True) + EPS) * g
    return y * (1.0 + sc) + sh


def _inproj_kernel(x_ref, g_ref, sh_ref, sc_ref, w_ref, o_ref):
    hn = _rms_mod(x_ref[0], g_ref[...], sh_ref[0], sc_ref[0])
    o_ref[0] = jnp.dot(hn.astype(BF16), w_ref[...], preferred_element_type=F32)


def _inproj(x, b0, g, mod3, w_bf16):
    _, L, d = x.shape
    n = w_bf16.shape[1]
    tt = 512
    return pl.pallas_call(
        _inproj_kernel,
        grid=(1, L // tt),
        in_specs=[pl.BlockSpec((1, tt, d), lambda b, j: (b0, j, 0)),
                  pl.BlockSpec((1, d), lambda b, j: (0, 0)),
                  pl.BlockSpec((1, 1, d), lambda b, j: (0, 0, 0)),
                  pl.BlockSpec((1, 1, d), lambda b, j: (0, 0, 1)),
                  pl.BlockSpec((d, n), lambda b, j: (0, 0))],
        out_specs=pl.BlockSpec((1, tt, n), lambda b, j: (0, j, 0)),
        out_shape=jax.ShapeDtypeStruct((1, L, n), F32),
        compiler_params=_cparams(("arbitrary", "arbitrary")),
        name="in_proj",
    )(x, g.reshape(1, d), mod3, mod3, w_bf16)


def _lower_bound(lbg):
    ex = jnp.exp(lbg - jnp.max(lbg, axis=0, keepdims=True))
    return ex[0:1] / jnp.sum(ex, axis=0, keepdims=True)


def _hgrn_chunk(q_ref, f_ref, i_ref, o_ref, ls, lb, st, c, reverse):
    row = lax.broadcasted_iota(I32, (CHUNK, CHUNK), 0)
    col = lax.broadcasted_iota(I32, (CHUNK, CHUNK), 1)
    keep = (col >= row) if reverse else (col <= row)
    tri = keep.astype(BF16)
    i_mid = CHUNK // 2 if reverse else CHUNK // 2 - 1
    i_end = 0 if reverse else CHUNK - 1
    r0 = pl.multiple_of(c * CHUNK, CHUNK)
    z = f_ref[0, pl.ds(r0, CHUNK), ls]
    q = _silu(q_ref[0, pl.ds(r0, CHUNK), ls])
    v = i_ref[0, pl.ds(r0, CHUNK), ls]
    logf = jnp.log(lb + (1.0 - lb) * jax.nn.sigmoid(z))
    k = (1.0 - lb) * jax.nn.sigmoid(-z)
    l1, l2, l3 = _split3(logf)
    cum = (jnp.dot(tri, l1, preferred_element_type=F32)
           + jnp.dot(tri, l2, preferred_element_type=F32)
           + jnp.dot(tri, l3, preferred_element_type=F32))
    c_mid = cum[i_mid:i_mid + 1]
    c_end = cum[i_end:i_end + 1]
    qr = (q * jnp.exp(cum - c_mid)).astype(BF16)
    kr = (k * jnp.exp(c_mid - cum)).astype(BF16)
    scores = jnp.where(keep, _dot_nt(qr, kr), 0.0)
    vb = v.astype(BF16)
    o_intra = jnp.dot(scores.astype(BF16), vb, preferred_element_type=F32)
    o_inter = _dot_nt((q * jnp.exp(cum)).astype(BF16), st.astype(BF16))
    o_ref[0, pl.ds(r0, CHUNK), ls] = o_intra + o_inter
    ku = (k * jnp.exp(c_end - cum)).astype(BF16)
    return st * jnp.exp(c_end) + _dot_tn(vb, ku)


HGRN_GROUP = 4


def _hgrn_kernel(qf_ref, ff_ref, if_ref, qb_ref, fb_ref, ib_ref, lbf_ref, lbb_ref, of_ref, ob_ref,
                 stf_ref, stb_ref, *, lblk):
    @pl.when(pl.program_id(2) == 0)
    def _():
        stf_ref[...] = jnp.zeros_like(stf_ref)
        stb_ref[...] = jnp.zeros_like(stb_ref)

    lanes = [slice(g * HEAD_DIM, (g + 1) * HEAD_DIM) for g in range(HGRN_GROUP)]
    lb_f = _lower_bound(lbf_ref[...])
    lb_b = _lower_bound(lbb_ref[...])
    nch = lblk // CHUNK

    def body(ci, carry):
        new = []
        for g, ls in enumerate(lanes):
            new.append(_hgrn_chunk(qf_ref, ff_ref, if_ref, of_ref, ls, lb_f[:, ls], carry[2 * g], ci, False))
            new.append(_hgrn_chunk(qb_ref, fb_ref, ib_ref, ob_ref, ls, lb_b[:, ls], carry[2 * g + 1],
                                   nch - 1 - ci, True))
        return tuple(new)

    init = tuple(ref[g] for g in range(HGRN_GROUP) for ref in (stf_ref, stb_ref))
    final = lax.fori_loop(0, nch, body, init)
    for g in range(HGRN_GROUP):
        stf_ref[g] = final[2 * g]
        stb_ref[g] = final[2 * g + 1]


def _hgrn(proj, lb_gamma_fwd, lb_gamma_bwd):
    nb, L, _ = proj.shape
    nh = lb_gamma_fwd.shape[1] // HEAD_DIM
    assert nh % HGRN_GROUP == 0
    ng = nh // HGRN_GROUP
    gw = HGRN_GROUP * HEAD_DIM
    lblk = min(1024, L)
    nblk = L // lblk

    def fwd(c0):
        return pl.BlockSpec((1, lblk, gw), lambda b, h, j: (b, j, c0 + h))

    def bwd(c0):
        return pl.BlockSpec((1, lblk, gw), lambda b, h, j: (b, nblk - 1 - j, c0 + h))

    lbs = pl.BlockSpec((lb_gamma_fwd.shape[0], gw), lambda b, h, j: (0, h))
    out = jax.ShapeDtypeStruct((nb, L, nh * HEAD_DIM), F32)
    state = pltpu.VMEM((HGRN_GROUP, HEAD_DIM, HEAD_DIM), F32)
    return pl.pallas_call(
        functools.partial(_hgrn_kernel, lblk=lblk),
        grid=(nb, ng, nblk),
        in_specs=[fwd(0), fwd(ng), fwd(3 * ng), bwd(0), bwd(2 * ng), bwd(3 * ng), lbs, lbs],
        out_specs=[fwd(0), bwd(0)],
        out_shape=[out, out],
        scratch_shapes=[state, state],
        compiler_params=_cparams(("arbitrary", "arbitrary", "arbitrary")),
        name="hgrn_bidir",
    )(proj, proj, proj, proj, proj, proj, lb_gamma_fwd, lb_gamma_bwd)


CONV_HALO = 16
CONV_ROWS = 64


def _conv_kernel(a_ref, g_ref, ap_ref, gp_ref, an_ref, gn_ref, w_ref, b_ref, lg_ref, lb_ref,
                 o_ref, hbuf, *, tl):
    j = pl.program_id(1)
    last = pl.num_programs(1) - 1
    hbuf[CONV_HALO:CONV_HALO + tl, :] = a_ref[0] * jax.nn.sigmoid(g_ref[0])
    hp = ap_ref[0] * jax.nn.sigmoid(gp_ref[0])
    hbuf[0:CONV_HALO, :] = jnp.where(j > 0, hp, 0.0)
    hn = an_ref[0] * jax.nn.sigmoid(gn_ref[0])
    hbuf[CONV_HALO + tl:2 * CONV_HALO + tl, :] = jnp.where(j < last, hn, 0.0)
    off = CONV_HALO - CONV_PAD
    for r in range(tl // CONV_ROWS):
        acc = jnp.zeros((CONV_ROWS, a_ref.shape[2]), F32)
        for k in range(CONV_WIDTH):
            s = r * CONV_ROWS + k + off
            acc = acc + w_ref[k:k + 1, :] * hbuf[s:s + CONV_ROWS, :]
        acc = acc + b_ref[...]
        mu = jnp.mean(acc, axis=-1, keepdims=True)
        cen = acc - mu
        var = jnp.mean(cen * cen, axis=-1, keepdims=True)
        y = cen * lax.rsqrt(var + EPS) * lg_ref[...] + lb_ref[...]
        o_ref[0, r * CONV_ROWS:(r + 1) * CONV_ROWS, :] = _silu(y)


def _conv(proj, w_pad, bias, ln_g, ln_b, a_col):
    nb, L, _ = proj.shape
    dc = w_pad.shape[1]
    tl = 256
    hb = tl // CONV_HALO
    nhalo = L // CONV_HALO

    def cur(c):
        return pl.BlockSpec((1, tl, dc), lambda b, j: (b, j, c))

    def prev(c):
        return pl.BlockSpec((1, CONV_HALO, dc), lambda b, j: (b, jnp.maximum(j * hb - 1, 0), c))

    def nxt(c):
        return pl.BlockSpec((1, CONV_HALO, dc),
                            lambda b, j: (b, jnp.minimum((j + 1) * hb, nhalo - 1), c))

    vec = pl.BlockSpec((1, dc), lambda b, j: (0, 0))
    return pl.pallas_call(
        functools.partial(_conv_kernel, tl=tl),
        grid=(nb, L // tl),
        in_specs=[cur(a_col), cur(a_col + 1), prev(a_col), prev(a_col + 1), nxt(a_col),
                  nxt(a_col + 1), pl.BlockSpec(w_pad.shape, lambda b, j: (0, 0)), vec, vec, vec],
        out_specs=pl.BlockSpec((1, tl, dc), lambda b, j: (b, j, 0)),
        out_shape=jax.ShapeDtypeStruct((nb, L, dc), F32),
        scratch_shapes=[pltpu.VMEM((tl + 2 * CONV_HALO, dc), F32)],
        compiler_params=_cparams(("arbitrary", "arbitrary")),
        name="conv_group",
    )(proj, proj, proj, proj, proj, proj, w_pad, bias.reshape(1, dc), ln_g.reshape(1, dc),
      ln_b.reshape(1, dc))


def _mix_kernel(of_ref, ob_ref, gr_ref, oc_ref, x_ref, hg_ref, g1_ref, sh2_ref, sc2_ref, ng_ref,
                wo_ref, wq_ref, k1_ref, k2_ref, h1_ref, hp_ref, s_ref):
    o = of_ref[0] + ob_ref[0]
    parts = []
    for hh in range(o.shape[1] // HEAD_DIM):
        oh = o[:, hh * HEAD_DIM:(hh + 1) * HEAD_DIM]
        parts.append(oh * lax.rsqrt(jnp.mean(oh * oh, axis=-1, keepdims=True) + EPS))
    on = jnp.concatenate(parts, axis=-1) * hg_ref[...] * _silu(gr_ref[0])
    cat = jnp.concatenate([on, oc_ref[0]], axis=-1).astype(BF16)
    mix = jnp.dot(cat, wo_ref[...], preferred_element_type=F32)
    h1 = x_ref[0] + g1_ref[0] * mix
    h1_ref[0] = h1
    hn2 = _rms_mod(h1, ng_ref[...], sh2_ref[0], sc2_ref[0])
    hp_ref[0] = _pack_words(hn2)
    q = jnp.dot(hn2.astype(BF16), wq_ref[...], preferred_element_type=F32)
    k1h, k1l, _ = _split3(k1_ref[...])
    k2h, k2l, _ = _split3(k2_ref[...])
    for hd in range(PEER_HEADS):
        for half, (kh, kl) in enumerate(((k1h, k1l), (k2h, k2l))):
            c0 = hd * 2 * PEER_HALF + half * PEER_HALF
            qh, ql, _ = _split3(q[:, c0:c0 + PEER_HALF])
            s_ref[2 * hd + half] = _dot_nt(kh, qh) + _dot_nt(kh, ql) + _dot_nt(kl, qh)


def _mix(o_f, o_b, proj, o_c, x, b0, hgrn_g, mod3, norm_g, w_out_bf16, wq_bf16, keys1, keys2, g_col):
    _, L, d = x.shape
    dh = o_f.shape[2]
    dq = wq_bf16.shape[1]
    tt = 256
    nj = L // tt

    def half(c=0):
        return pl.BlockSpec((1, tt, dh), lambda b, j: (0, j, c))

    def full(bb=0):
        return pl.BlockSpec((1, tt, d), lambda b, j: (bb, j, 0))

    def modc(c):
        return pl.BlockSpec((1, 1, d), lambda b, j: (0, 0, c))

    def const(shape):
        return pl.BlockSpec(shape, lambda b, j: (0,) * len(shape))

    return pl.pallas_call(
        _mix_kernel,
        grid=(1, nj),
        in_specs=[half(), half(), half(g_col), half(), full(b0), const((1, dh)),
                  modc(2), modc(3), modc(4), const((1, d)), const((d, d)), const((d, dq)),
                  const(keys1.shape), const(keys2.shape)],
        out_specs=[full(), pl.BlockSpec((1, tt, d // 2), lambda b, j: (0, j, 0)),
                   pl.BlockSpec((2 * PEER_HEADS, PEER_KEYS, tt), lambda b, j: (0, 0, j))],
        out_shape=[jax.ShapeDtypeStruct((1, L, d), F32), jax.ShapeDtypeStruct((1, L, d // 2), I32),
                   jax.ShapeDtypeStruct((2 * PEER_HEADS, PEER_KEYS, L), F32)],
        compiler_params=_cparams(("arbitrary", "arbitrary")),
        name="mix_scores",
    )(o_f, o_b, proj, o_c, x, hgrn_g.reshape(1, dh), mod3, mod3, mod3, norm_g.reshape(1, d),
      w_out_bf16, wq_bf16, keys1, keys2)


def _top16(s, payload=None):
    n = s.shape[0]
    iota = lax.broadcasted_iota(I32, s.shape, 0).astype(F32)
    vals, tags = [], []
    for _ in range(PEER_TOPK):
        m = jnp.max(s, axis=0, keepdims=True)
        idx = jnp.min(jnp.where(s == m, iota, float(n)), axis=0, keepdims=True)
        hit = iota == idx
        vals.append(m)
        tags.append(idx if payload is None else jnp.max(jnp.where(hit, payload, -1.0), axis=0, keepdims=True))
        s = jnp.where(hit, -jnp.inf, s)
    return jnp.concatenate(vals, axis=0), jnp.concatenate(tags, axis=0)


def _pruned_pairs(x1, x2, combine):
    rows = [combine(x1[a:a + 1], x2[0:PEER_TOPK // (a + 1)]) for a in range(PEER_TOPK // 2)]
    rows.append(combine(x1[PEER_TOPK // 2:], x2[0:1]))
    return jnp.concatenate(rows, axis=0)


def _topk_kernel(s_ref, e_ref, w_ref):
    es, ws = [], []
    for hd in range(PEER_HEADS):
        v1, i1 = _top16(s_ref[2 * hd])
        v2, i2 = _top16(s_ref[2 * hd + 1])
        cand = _pruned_pairs(v1, v2, lambda a, b: a + b)
        cand_e = _pruned_pairs(i1, i2, lambda a, b: a * float(PEER_KEYS) + b)
        sc, e = _top16(cand, cand_e)
        p = jnp.exp(sc - sc[0:1])
        es.append(e.astype(I32))
        ws.append(p / jnp.sum(p, axis=0, keepdims=True))
    e_ref[...] = jnp.concatenate(es, axis=0).T
    w_ref[...] = jnp.concatenate(ws, axis=0).T


def _topk(scores):
    npair, nk, T = scores.shape
    nsel = PEER_HEADS * PEER_TOPK
    tt = 256
    return pl.pallas_call(
        _topk_kernel,
        grid=(T // tt,),
        in_specs=[pl.BlockSpec((npair, nk, tt), lambda i: (0, 0, i))],
        out_specs=[pl.BlockSpec((tt, nsel), lambda i: (i, 0)),
                   pl.BlockSpec((tt, nsel), lambda i: (i, 0))],
        out_shape=[jax.ShapeDtypeStruct((T, nsel), I32), jax.ShapeDtypeStruct((T, nsel), F32)],
        compiler_params=_cparams(("arbitrary",)),
        name="peer_topk",
    )(scores)


WORDS = 4


def _pack_words(x):
    bits = pltpu.bitcast(x.astype(BF16).astype(F32), I32)
    out = []
    for c in range(WORDS):
        lo = bits[:, (2 * c) * LANES:(2 * c + 1) * LANES]
        hi = bits[:, (2 * c + 1) * LANES:(2 * c + 2) * LANES]
        out.append(lax.shift_right_logical(lo, 16) | (hi & jnp.int32(-65536)))
    return jnp.concatenate(out, axis=-1)


def _pack_kernel(x_ref, o_ref, b_ref):
    te = x_ref.shape[0]
    x = x_ref[...]
    b_ref[...] = x.astype(BF16)
    words = _pack_words(x)
    for c in range(WORDS):
        o_ref[pl.ds(c, te, stride=WORDS), :] = words[:, c * LANES:(c + 1) * LANES]


def _pack_table(x):
    ne, d = x.shape
    te = 512
    packed, plain = pl.pallas_call(
        _pack_kernel,
        grid=(ne // te,),
        in_specs=[pl.BlockSpec((te, d), lambda i: (i, 0))],
        out_specs=[pl.BlockSpec((te * WORDS, LANES), lambda i: (i, 0)), pl.BlockSpec((te, d), lambda i: (i, 0))],
        out_shape=[jax.ShapeDtypeStruct((ne * WORDS, LANES), I32), jax.ShapeDtypeStruct((ne, d), BF16)],
        compiler_params=_cparams(("arbitrary",)),
        name="pack_table",
    )(x)
    return packed.reshape(ne, WORDS, LANES), plain


NSEL = PEER_HEADS * PEER_TOPK
ROW_TILE = 2 * WORDS


SC_LANES = 16
SC_ROWS = 32
SC_BUFS = 4
SC_TOK = 16
SC_GROUP = 8
VS_GROUP = 2
PART_ROWS = NSEL * SC_LANES // LANES


def _sc_unpack(w):
    lo = plsc.bitcast(lax.shift_left(w, jnp.int32(16)), F32)
    hi = plsc.bitcast(w & jnp.int32(-65536), F32)
    return lo, hi


def _sc_mesh():
    return plsc.VectorSubcoreMesh(core_axis_name="c", subcore_axis_name="s")


def _sc_params():
    return pltpu.CompilerParams(use_tc_tiling_on_sc=True, needs_layout_passes=False)


def _sc_udot(table3, idx, hn3, n_tok):
    T = n_tok
    info = plsc.get_sparse_core_info()
    nw = info.num_cores * info.num_subcores
    tpw = T // nw
    assert T % nw == 0 and tpw % SC_TOK == 0 and NSEL % SC_ROWS == 0
    nchunk = tpw // SC_TOK
    qsteps = NSEL // SC_ROWS
    nstep = SC_TOK * qsteps

    @functools.partial(
        pl.kernel, mesh=_sc_mesh(),
        out_type=jax.ShapeDtypeStruct((T, PART_ROWS, LANES), F32),
        scratch_types=[pltpu.VMEM((SC_TOK * NSEL,), I32),
                       pltpu.VMEM((SC_BUFS, SC_ROWS, WORDS, LANES), I32),
                       pltpu.VMEM((SC_TOK, WORDS * LANES), I32),
                       pltpu.VMEM((SC_TOK, PART_ROWS, LANES), F32),
                       pltpu.SemaphoreType.DMA((SC_BUFS,))],
        compiler_params=_sc_params(),
        name="sc_udot",
    )
    def k(table_hbm, idx_hbm, hn_hbm, out_hbm, idx_v, rows_v, h_v, p_v, sem_g):
        wid = lax.axis_index("s") * info.num_cores + lax.axis_index("c")
        tbase = wid * tpw

        def gather(st, slot):
            return pltpu.make_async_copy(table_hbm.at[idx_v.at[pl.ds(st * SC_ROWS, SC_ROWS)]],
                                         rows_v.at[slot], sem_g.at[slot])

        def compute(tk, q, slot):
            @pl.loop(0, SC_ROWS // SC_GROUP)
            def _(g):
                acc = [jnp.zeros((SC_LANES,), F32) for _ in range(SC_GROUP)]
                for c in range(WORDS):
                    for lv in range(0, LANES // SC_LANES, 2):
                        ls = [pl.ds((lv + d) * SC_LANES, SC_LANES) for d in range(2)]
                        hb = [plsc.bitcast(h_v[tk, pl.ds(c * LANES + (lv + d) * SC_LANES, SC_LANES)], BF16)
                              for d in range(2)]
                        for i in range(SC_GROUP):
                            pr = [plsc.bitcast(rows_v[slot, g * SC_GROUP + i, c, ls[d]], BF16) * hb[d]
                                  for d in range(2)]
                            lo, hi = _sc_unpack(plsc.bitcast(pr[0] + pr[1], I32))
                            acc[i] = acc[i] + lo + hi
                for i in range(SC_GROUP):
                    p_v[tk, q * (SC_ROWS // SC_GROUP) + g, pl.ds(i * SC_LANES, SC_LANES)] = acc[i]

        @pl.loop(0, nchunk)
        def _(ch):
            t0 = tbase + ch * SC_TOK
            pltpu.sync_copy(idx_hbm.at[pl.ds(t0 * NSEL, SC_TOK * NSEL)], idx_v)
            pltpu.sync_copy(hn_hbm.at[pl.ds(t0, SC_TOK)], h_v)
            for s in range(SC_BUFS - 1):
                gather(s, s).start()

            @pl.loop(0, SC_TOK)
            def _(tk):
                for q in range(qsteps):
                    st = tk * qsteps + q
                    slot = q % SC_BUFS

                    @pl.when(st + SC_BUFS - 1 < nstep)
                    def _():
                        gather(st + SC_BUFS - 1, (q + SC_BUFS - 1) % SC_BUFS).start()

                    gather(st, slot).wait()
                    compute(tk, q, slot)

            pltpu.sync_copy(p_v, out_hbm.at[pl.ds(t0, SC_TOK)])

    return k(table3, idx, hn3)


def _sc_vsum(table3, idx, coefx):
    assert VS_GROUP == 2, "the loop body sums the products of exactly two experts in bf16"
    T = coefx.shape[0]
    info = plsc.get_sparse_core_info()
    nw = info.num_cores * info.num_subcores
    tpw = T // nw
    assert T % nw == 0 and tpw % SC_TOK == 0 and NSEL % SC_ROWS == 0 and SC_GROUP % VS_GROUP == 0
    nchunk = tpw // SC_TOK
    qsteps = NSEL // SC_ROWS
    nstep = SC_TOK * qsteps
    nlv = LANES // SC_LANES
    sub = SC_GROUP // VS_GROUP

    @functools.partial(
        pl.kernel, mesh=_sc_mesh(),
        out_type=jax.ShapeDtypeStruct((T, ROW_TILE * LANES), F32),
        scratch_types=[pltpu.VMEM((SC_TOK * NSEL,), I32),
                       pltpu.VMEM((SC_BUFS, SC_ROWS, WORDS, LANES), I32),
                       pltpu.VMEM((SC_TOK, PART_ROWS, LANES), I32),
                       pltpu.VMEM((SC_TOK, ROW_TILE * LANES), F32),
                       pltpu.SemaphoreType.DMA((SC_BUFS,))],
        compiler_params=_sc_params(),
        name="sc_vsum",
    )
    def k(table_hbm, idx_hbm, coef_hbm, out_hbm, idx_v, rows_v, c_v, y_v, sem_g):
        wid = lax.axis_index("s") * info.num_cores + lax.axis_index("c")
        tbase = wid * tpw

        def gather(st, slot):
            return pltpu.make_async_copy(table_hbm.at[idx_v.at[pl.ds(st * SC_ROWS, SC_ROWS)]],
                                         rows_v.at[slot], sem_g.at[slot])

        def compute(tk, q, slot):
            for c in range(WORDS):
                if q == 0:
                    init = tuple(jnp.zeros((SC_LANES,), F32) for _ in range(2 * nlv))
                else:
                    init = tuple(y_v[tk, pl.ds((2 * c + p) * LANES + lv * SC_LANES, SC_LANES)]
                                 for p in range(2) for lv in range(nlv))

                def body(g, acc):
                    acc = list(acc)
                    cb = [plsc.bitcast(c_v[tk, q * (SC_ROWS // SC_GROUP) + g // sub,
                                           pl.ds(((g % sub) * VS_GROUP + i) * SC_LANES, SC_LANES)], BF16)
                          for i in range(VS_GROUP)]
                    for lv in range(nlv):
                        pr = [cb[i] * plsc.bitcast(rows_v[slot, g * VS_GROUP + i, c,
                                                          pl.ds(lv * SC_LANES, SC_LANES)], BF16)
                              for i in range(VS_GROUP)]
                        lo, hi = _sc_unpack(plsc.bitcast(pr[0] + pr[1], I32))
                        acc[lv] = acc[lv] + lo
                        acc[nlv + lv] = acc[nlv + lv] + hi
                    return tuple(acc)

                acc = lax.fori_loop(0, SC_ROWS // VS_GROUP, body, init)
                for p in range(2):
                    for lv in range(nlv):
                        y_v[tk, pl.ds((2 * c + p) * LANES + lv * SC_LANES, SC_LANES)] = acc[p * nlv + lv]

        @pl.loop(0, nchunk)
        def _(ch):
            t0 = tbase + ch * SC_TOK
            pltpu.sync_copy(idx_hbm.at[pl.ds(t0 * NSEL, SC_TOK * NSEL)], idx_v)
            pltpu.sync_copy(coef_hbm.at[pl.ds(t0, SC_TOK)], c_v)
            for s in range(SC_BUFS - 1):
                gather(s, s).start()

            @pl.loop(0, SC_TOK)
            def _(tk):
                for q in range(qsteps):
                    st = tk * qsteps + q
                    slot = q % SC_BUFS

                    @pl.when(st + SC_BUFS - 1 < nstep)
                    def _():
                        gather(st + SC_BUFS - 1, (q + SC_BUFS - 1) % SC_BUFS).start()

                    gather(st, slot).wait()
                    compute(tk, q, slot)

            pltpu.sync_copy(y_v, out_hbm.at[pl.ds(t0, SC_TOK)])

    return k(table3, idx, coefx)


def _coef_kernel(p_ref, w_ref, cx_ref, *, tt):
    row = lax.broadcasted_iota(I32, (LANES, NSEL), 0)
    col = lax.broadcasted_iota(I32, (LANES, NSEL), 1)
    per_row = LANES // SC_LANES
    act = jnp.zeros((tt, NSEL), F32)
    for s in range(PART_ROWS):
        fold = (col == per_row * s + row // SC_LANES).astype(BF16)
        p1, p2, p3 = _split3(p_ref[pl.ds(s, tt, stride=PART_ROWS), :])
        act = act + (jnp.dot(p1, fold, preferred_element_type=F32) + jnp.dot(p2, fold, preferred_element_type=F32)
                     + jnp.dot(p3, fold, preferred_element_type=F32))
    coef = w_ref[...] * (0.5 * act * (1.0 + lax.erf(act * (2.0 ** -0.5))))
    cb = coef.astype(BF16)
    for s in range(PART_ROWS):
        spread = (row == per_row * s + col // SC_LANES).astype(BF16)
        bits = pltpu.bitcast(jnp.dot(cb, spread, preferred_element_type=F32), I32)
        cx_ref[pl.ds(s, tt, stride=PART_ROWS), :] = bits | lax.shift_right_logical(bits, 16)


def _coef(p2d, w):
    T = p2d.shape[0] // PART_ROWS
    tt = 256
    return pl.pallas_call(
        functools.partial(_coef_kernel, tt=tt),
        grid=(T // tt,),
        in_specs=[pl.BlockSpec((tt * PART_ROWS, LANES), lambda i: (i, 0)),
                  pl.BlockSpec((tt, NSEL), lambda i: (i, 0))],
        out_specs=pl.BlockSpec((tt * PART_ROWS, LANES), lambda i: (i, 0)),
        out_shape=jax.ShapeDtypeStruct((T * PART_ROWS, LANES), I32),
        compiler_params=_cparams(("arbitrary",)),
        name="peer_coef",
    )(p2d, w)


SC_FRACTION = 0.375
SPLIT_UNIT = 1024
WD_BUFS = 4
WD_TOK = 16


def _sc_wdense(e_flat, w_flat, n_skip, n_tok, nexp):
    info = plsc.get_sparse_core_info()
    nw = info.num_cores * info.num_subcores
    tpw = n_tok // nw
    assert n_tok % nw == 0 and tpw % WD_TOK == 0 and WD_TOK % WD_BUFS == 0 and nexp % SC_LANES == 0
    nchunk = tpw // WD_TOK
    heads = NSEL // SC_LANES

    @functools.partial(
        pl.kernel, mesh=_sc_mesh(),
        out_type=jax.ShapeDtypeStruct((n_tok, nexp), F32),
        scratch_types=[pltpu.VMEM((WD_TOK * NSEL,), I32), pltpu.VMEM((WD_TOK * NSEL,), F32)]
        + [pltpu.VMEM((nexp,), F32) for _ in range(WD_BUFS)]
        + [pltpu.SemaphoreType.DMA((WD_BUFS,))],
        compiler_params=_sc_params(),
        name="sc_wdense",
    )
    def k(e_hbm, w_hbm, out_hbm, idx_v, w_v, *rest):
        rows, sem = rest[:WD_BUFS], rest[WD_BUFS]
        wid = lax.axis_index("s") * info.num_cores + lax.axis_index("c")
        tbase = wid * tpw
        zeros = jnp.zeros((SC_LANES,), F32)

        for s in range(WD_BUFS):
            @pl.loop(0, nexp // SC_LANES)
            def _(i):
                rows[s][pl.ds(i * SC_LANES, SC_LANES)] = zeros

        def put(tok, s):
            return pltpu.make_async_copy(rows[s], out_hbm.at[tok], sem.at[s])

        def scatter(tk, s, clear):
            for h in range(heads):
                sl = pl.ds(tk * NSEL + h * SC_LANES, SC_LANES)
                if clear:
                    plsc.store_scatter(rows[s], [idx_v[sl]], zeros)
                else:
                    plsc.addupdate_scatter(rows[s], [idx_v[sl]], w_v[sl])

        @pl.loop(0, nchunk)
        def _(ch):
            t0 = tbase + ch * WD_TOK
            pltpu.sync_copy(e_hbm.at[pl.ds((n_skip + t0) * NSEL, WD_TOK * NSEL)], idx_v)
            pltpu.sync_copy(w_hbm.at[pl.ds((n_skip + t0) * NSEL, WD_TOK * NSEL)], w_v)

            @pl.loop(0, WD_TOK, step=WD_BUFS)
            def _(tk0):
                for s in range(WD_BUFS):
                    tk = tk0 + s

                    @pl.when(tk0 > 0)
                    def _():
                        put(0, s).wait()
                        scatter(tk - WD_BUFS, s, True)

                    scatter(tk, s, False)
                    put(t0 + tk, s).start()

            for s in range(WD_BUFS):
                put(0, s).wait()
                scatter(WD_TOK - WD_BUFS + s, s, True)

    return k(e_flat, w_flat)


def _dense_kernel(hp_ref, wd_ref, u_ref, v_ref, y_ref, acc_ref):
    k = pl.program_id(1)

    @pl.when(k == 0)
    def _():
        acc_ref[...] = jnp.zeros_like(acc_ref)

    words = hp_ref[...]
    lo = pltpu.bitcast(lax.shift_left(words, 16), F32)
    hi = pltpu.bitcast(words & jnp.int32(-65536), F32)
    parts = []
    for c in range(WORDS):
        parts += [lo[:, c * LANES:(c + 1) * LANES], hi[:, c * LANES:(c + 1) * LANES]]
    hn = jnp.concatenate(parts, axis=-1).astype(BF16)
    act = _dot_nt(hn, u_ref[...])
    wd = wd_ref[...]
    coef = jnp.where(wd != 0.0, wd * (0.5 * act * (1.0 + lax.erf(act * (2.0 ** -0.5)))), 0.0)
    acc_ref[...] += jnp.dot(coef.astype(BF16), v_ref[...], preferred_element_type=F32)

    @pl.when(k == pl.num_programs(1) - 1)
    def _():
        y_ref[...] = acc_ref[...]


def _dense_experts(hp, wd, u_bf16, v_bf16, n_skip):
    n_tok, nexp = wd.shape
    dw = hp.shape[1]
    d = u_bf16.shape[1]
    tt = min(SPLIT_UNIT, n_tok)
    eb = 1024
    skip = n_skip // tt
    return pl.pallas_call(
        _dense_kernel,
        grid=(n_tok // tt, nexp // eb),
        in_specs=[pl.BlockSpec((tt, dw), lambda i, k: (skip + i, 0)),
                  pl.BlockSpec((tt, eb), lambda i, k: (i, k)),
                  pl.BlockSpec((eb, d), lambda i, k: (k, 0)),
                  pl.BlockSpec((eb, d), lambda i, k: (k, 0))],
        out_specs=pl.BlockSpec((tt, d), lambda i, k: (i, 0)),
        out_shape=jax.ShapeDtypeStruct((n_tok, d), F32),
        scratch_shapes=[pltpu.VMEM((tt, d), F32)],
        compiler_params=_cparams(("arbitrary", "arbitrary")),
        name="peer_dense",
    )(hp, wd, u_bf16, v_bf16)


def _final_kernel(h1_ref, ya_ref, yb_ref, g2_ref, fg_ref, fsh_ref, fsc_ref, *refs, na):
    o_ref = refs[-1]
    y = jnp.where(pl.program_id(1) < na, ya_ref[...], yb_ref[...])
    h = h1_ref[0] + g2_ref[0] * y
    o_ref[0] = _rms_mod(h, fg_ref[...], fsh_ref[0], fsc_ref[0])


def _final(h1, ya, yb, mod3, fmod3, final_g, out_prev, b0, nb):
    _, L, d = h1.shape
    tt = 512
    na, nbk = ya.shape[0] // tt, yb.shape[0] // tt
    in_specs = [pl.BlockSpec((1, tt, d), lambda b, j: (0, j, 0)),
                pl.BlockSpec((tt, d), lambda b, j: (jnp.minimum(j, na - 1), 0)),
                pl.BlockSpec((tt, d), lambda b, j: (jnp.clip(j - na, 0, nbk - 1), 0)),
                pl.BlockSpec((1, 1, d), lambda b, j: (0, 0, 5)),
                pl.BlockSpec((1, d), lambda b, j: (0, 0)),
                pl.BlockSpec((1, 1, d), lambda b, j: (0, 0, 0)),
                pl.BlockSpec((1, 1, d), lambda b, j: (0, 0, 1))]
    args = [h1, ya, yb, mod3, final_g.reshape(1, d), fmod3, fmod3]
    aliases = {}
    if out_prev is not None:
        in_specs.append(pl.BlockSpec(memory_space=pl.ANY))
        aliases = {len(args): 0}
        args.append(out_prev)
    return pl.pallas_call(
        functools.partial(_final_kernel, na=na),
        grid=(1, L // tt),
        in_specs=in_specs,
        out_specs=pl.BlockSpec((1, tt, d), lambda b, j: (b0, j, 0)),
        out_shape=jax.ShapeDtypeStruct((nb, L, d), F32),
        input_output_aliases=aliases,
        compiler_params=_cparams(("arbitrary", "arbitrary")),
        name="final_norm",
    )(*args)


def kernel(x, c, ada_w, ada_b, norm_mix_g, w_in, lb_gamma_fwd, lb_gamma_bwd, hgrn_norm_g, conv_w,
           conv_b, conv_ln_g, conv_ln_b, w_out, norm_ffn_g, peer_wq, peer_keys1, peer_keys2, peer_u,
           peer_v, final_ada_w, final_ada_b, final_norm_g):
    nb, L, d = x.shape
    assert ada_w.shape[0] == 1, "single-layer trunk"
    d_hgrn = lb_gamma_fwd.shape[1]

    c_pad = jnp.pad(c, ((0, SUBLANES - nb), (0, 0)))
    mod3 = _ada(c_pad, ada_w[0], ada_b[0])[:nb].reshape(nb, 1, 6 * d)
    fmod3 = _ada(c_pad, final_ada_w, final_ada_b)[:nb].reshape(nb, 1, 2 * d)

    d_conv = conv_w.shape[2]
    w_pad = jnp.pad(conv_w[0], ((0, 1), (0, 0)))
    w_in_bf, w_out_bf, wq_bf = w_in[0].astype(BF16), w_out[0].astype(BF16), peer_wq[0].astype(BF16)
    mb, (u_rows, u_bf) = lax.optimization_barrier((mod3[0:1], _pack_table(peer_u[0])))
    nexp = peer_u.shape[1]
    n_sc = int(L * SC_FRACTION) // SPLIT_UNIT * SPLIT_UNIT
    v_src, v_rows, v_bf = peer_v[0], None, None
    work = []

    def weighted_sum(item):
        return _sc_vsum(v_rows, item["idx"], item["cx"].reshape(n_sc, PART_ROWS, LANES))

    for b in range(nb):
        proj = _inproj(x, b, norm_mix_g[0], mb, w_in_bf)
        o_f, o_b = _hgrn(proj, lb_gamma_fwd, lb_gamma_bwd)
        o_c = _conv(proj, w_pad, conv_b[0], conv_ln_g[0], conv_ln_b[0], a_col=5 * d_hgrn // d_conv)
        h1, hp, scores = _mix(o_f, o_b, proj, o_c, x, b, hgrn_norm_g[0], mb, norm_ffn_g[0], w_out_bf,
                              wq_bf, peer_keys1[0], peer_keys2[0], g_col=4)
        e, w = _topk(scores)
        item = dict(h1=h1, mb=mb, hp=hp.reshape(L, d // 2))
        tied = dict(e=e, w=w)
        if b + 1 < nb:
            tied["mb"] = mod3[b + 1:b + 2]
        if work:
            prev = work[-1]
            tied["wd"] = prev["wd"]
            tied["cx"] = _coef(prev["p"].reshape(n_sc * PART_ROWS, LANES), prev["w"])
            if len(work) > 1:
                tied["y"] = work[-2]["y_sc"]
        elif nb > 1:
            tied["v"] = v_src
        tied = lax.optimization_barrier(tied)
        e, w, mb = tied["e"], tied["w"], tied.get("mb")
        if "v" in tied:
            v_rows, v_bf = _pack_table(tied["v"])
        if work:
            prev["wd"], prev["cx"] = tied["wd"], tied["cx"]
            if "y" in tied:
                work[-2]["y_sc"] = tied["y"]
            prev["y_sc"] = weighted_sum(prev)
        item["idx"], item["w"] = e.reshape(L * NSEL), w
        item["p"] = _sc_udot(u_rows, item["idx"], item["hp"], n_sc)
        item["wd"] = _sc_wdense(item["idx"], w.reshape(L * NSEL), n_sc, L - n_sc, nexp)
        work.append(item)

    if v_rows is None:
        v_rows, v_bf = _pack_table(v_src)
    y_dn = [_dense_experts(item["hp"], item["wd"], u_bf, v_bf, n_sc) for item in work[:-1]]
    last = work[-1]
    last["p"], y_dn = lax.optimization_barrier((last["p"], y_dn))
    last["cx"] = _coef(last["p"].reshape(n_sc * PART_ROWS, LANES), last["w"])
    last["y_sc"] = weighted_sum(last)
    y_dn.append(_dense_experts(last["hp"], last["wd"], u_bf, v_bf, n_sc))
    out = None
    for b, item in enumerate(work):
        out = _final(item["h1"], item["y_sc"], y_dn[b], item["mb"], fmod3[b:b + 1], final_norm_g, out, b, nb)
    return out
```

```python
import functools

import jax
import jax.numpy as jnp
from jax import lax
from jax.experimental import pallas as pl
from jax.experimental.pallas import tpu as pltpu
from jax.experimental.pallas import tpu_sc as plsc

F32 = jnp.float32
BF16 = jnp.bfloat16
I32 = jnp.int32

EPS = 1e-6
HEAD_DIM = 128
CHUNK = 64
CONV_WIDTH = 31
CONV_PAD = CONV_WIDTH // 2
PEER_HEADS = 8
PEER_KEYS = 128
PEER_TOPK = 16
PEER_HALF = 128
LANES = 128
SUBLANES = 8
VMEM_LIMIT = 48 * 1024 * 1024


def _cparams(sem):
    return pltpu.CompilerParams(dimension_semantics=sem, vmem_limit_bytes=VMEM_LIMIT)


def _silu(x):
    return x * jax.nn.sigmoid(x)


def _split3(x):
    x1 = x.astype(BF16)
    r1 = x - x1.astype(F32)
    x2 = r1.astype(BF16)
    x3 = (r1 - x2.astype(F32)).astype(BF16)
    return x1, x2, x3


def _dot_nt(a, b):
    return lax.dot_general(a, b, (((1,), (1,)), ((), ())), preferred_element_type=F32)


def _dot_tn(a, b):
    return lax.dot_general(a, b, (((0,), (0,)), ((), ())), preferred_element_type=F32)


def _ada_kernel(c_ref, w_ref, b_ref, o_ref):
    ca = _silu(c_ref[...])
    o_ref[...] = jnp.dot(ca, w_ref[...], preferred_element_type=F32,
                         precision=lax.Precision.HIGHEST) + b_ref[...]


def _ada(c_pad, w, b):
    d, n = w.shape
    tn = 1024
    return pl.pallas_call(
        _ada_kernel,
        grid=(n // tn,),
        in_specs=[pl.BlockSpec((SUBLANES, d), lambda j: (0, 0)),
                  pl.BlockSpec((d, tn), lambda j: (0, j)),
                  pl.BlockSpec((1, tn), lambda j: (0, j))],
        out_specs=pl.BlockSpec((SUBLANES, tn), lambda j: (0, j)),
        out_shape=jax.ShapeDtypeStruct((SUBLANES, n), F32),
        compiler_params=_cparams(("arbitrary",)),
        name="ada_proj",
    )(c_pad, w, b.reshape(1, n))


def _rms_mod(x, g, sh, sc):
    y = x * lax.rsqrt(jnp.mean(x * x, axis=-1, keepdims=True) + EPS) * g
    return y * (1.0 + sc) + sh


def _inproj_kernel(x_ref, g_ref, sh_ref, sc_ref, w_ref, o_ref):
    hn = _rms_mod(x_ref[0], g_ref[...], sh_ref[0], sc_ref[0])
    o_ref[0] = jnp.dot(hn.astype(BF16), w_ref[...], preferred_element_type=F32)


def _inproj(x, b0, g, mod3, w_bf16):
    _, L, d = x.shape
    n = w_bf16.shape[1]
    tt = 512
    return pl.pallas_call(
        _inproj_kernel,
        grid=(1, L // tt),
        in_specs=[pl.BlockSpec((1, tt, d), lambda b, j: (b0, j, 0)),
                  pl.BlockSpec((1, d), lambda b, j: (0, 0)),
                  pl.BlockSpec((1, 1, d), lambda b, j: (0, 0, 0)),
                  pl.BlockSpec((1, 1, d), lambda b, j: (0, 0, 1)),
                  pl.BlockSpec((d, n), lambda b, j: (0, 0))],
        out_specs=pl.BlockSpec((1, tt, n), lambda b, j: (0, j, 0)),
        out_shape=jax.ShapeDtypeStruct((1, L, n), F32),
        compiler_params=_cparams(("arbitrary", "arbitrary")),
        name="in_proj",
    )(x, g.reshape(1, d), mod3, mod3, w_bf16)


def _lower_bound(lbg):
    ex = jnp.exp(lbg - jnp.max(lbg, axis=0, keepdims=True))
    return ex[0:1] / jnp.sum(ex, axis=0, keepdims=True)


def _hgrn_chunk(q_ref, f_ref, i_ref, o_ref, ls, lb, st, c, reverse):
    row = lax.broadcasted_iota(I32, (CHUNK, CHUNK), 0)
    col = lax.broadcasted_iota(I32, (CHUNK, CHUNK), 1)
    keep = (col >= row) if reverse else (col <= row)
    tri = keep.astype(BF16)
    i_mid = CHUNK // 2 if reverse else CHUNK // 2 - 1
    i_end = 0 if reverse else CHUNK - 1
    r0 = pl.multiple_of(c * CHUNK, CHUNK)
    z = f_ref[0, pl.ds(r0, CHUNK), ls]
    q = _silu(q_ref[0, pl.ds(r0, CHUNK), ls])
    v = i_ref[0, pl.ds(r0, CHUNK), ls]
    logf = jnp.log(lb + (1.0 - lb) * jax.nn.sigmoid(z))
    k = (1.0 - lb) * jax.nn.sigmoid(-z)
    l1, l2, l3 = _split3(logf)
    cum = (jnp.dot(tri, l1, preferred_element_type=F32)
           + jnp.dot(tri, l2, preferred_element_type=F32)
           + jnp.dot(tri, l3, preferred_element_type=F32))
    c_mid = cum[i_mid:i_mid + 1]
    c_end = cum[i_end:i_end + 1]
    qr = (q * jnp.exp(cum - c_mid)).astype(BF16)
    kr = (k * jnp.exp(c_mid - cum)).astype(BF16)
    scores = jnp.where(keep, _dot_nt(qr, kr), 0.0)
    vb = v.astype(BF16)
    o_intra = jnp.dot(scores.astype(BF16), vb, preferred_element_type=F32)
    o_inter = _dot_nt((q * jnp.exp(cum)).astype(BF16), st.astype(BF16))
    o_ref[0, pl.ds(r0, CHUNK), ls] = o_intra + o_inter
    ku = (k * jnp.exp(c_end - cum)).astype(BF16)
    return st * jnp.exp(c_end) + _dot_tn(vb, ku)


HGRN_GROUP = 4


def _hgrn_kernel(qf_ref, ff_ref, if_ref, qb_ref, fb_ref, ib_ref, lbf_ref, lbb_ref, of_ref, ob_ref,
                 stf_ref, stb_ref, *, lblk):
    @pl.when(pl.program_id(2) == 0)
    def _():
        stf_ref[...] = jnp.zeros_like(stf_ref)
        stb_ref[...] = jnp.zeros_like(stb_ref)

    lanes = [slice(g * HEAD_DIM, (g + 1) * HEAD_DIM) for g in range(HGRN_GROUP)]
    lb_f = _lower_bound(lbf_ref[...])
    lb_b = _lower_bound(lbb_ref[...])
    nch = lblk // CHUNK

    def body(ci, carry):
        new = []
        for g, ls in enumerate(lanes):
            new.append(_hgrn_chunk(qf_ref, ff_ref, if_ref, of_ref, ls, lb_f[:, ls], carry[2 * g], ci, False))
            new.append(_hgrn_chunk(qb_ref, fb_ref, ib_ref, ob_ref, ls, lb_b[:, ls], carry[2 * g + 1],
                                   nch - 1 - ci, True))
        return tuple(new)

    init = tuple(ref[g] for g in range(HGRN_GROUP) for ref in (stf_ref, stb_ref))
    final = lax.fori_loop(0, nch, body, init)
    for g in range(HGRN_GROUP):
        stf_ref[g] = final[2 * g]
        stb_ref[g] = final[2 * g + 1]


def _hgrn(proj, lb_gamma_fwd, lb_gamma_bwd):
    nb, L, _ = proj.shape
    nh = lb_gamma_fwd.shape[1] // HEAD_DIM
    assert nh % HGRN_GROUP == 0
    ng = nh // HGRN_GROUP
    gw = HGRN_GROUP * HEAD_DIM
    lblk = min(1024, L)
    nblk = L // lblk

    def fwd(c0):
        return pl.BlockSpec((1, lblk, gw), lambda b, h, j: (b, j, c0 + h))

    def bwd(c0):
        return pl.BlockSpec((1, lblk, gw), lambda b, h, j: (b, nblk - 1 - j, c0 + h))

    lbs = pl.BlockSpec((lb_gamma_fwd.shape[0], gw), lambda b, h, j: (0, h))
    out = jax.ShapeDtypeStruct((nb, L, nh * HEAD_DIM), F32)
    state = pltpu.VMEM((HGRN_GROUP, HEAD_DIM, HEAD_DIM), F32)
    return pl.pallas_call(
        functools.partial(_hgrn_kernel, lblk=lblk),
        grid=(nb, ng, nblk),
        in_specs=[fwd(0), fwd(ng), fwd(3 * ng), bwd(0), bwd(2 * ng), bwd(3 * ng), lbs, lbs],
        out_specs=[fwd(0), bwd(0)],
        out_shape=[out, out],
        scratch_shapes=[state, state],
        compiler_params=_cparams(("arbitrary", "arbitrary", "arbitrary")),
        name="hgrn_bidir",
    )(proj, proj, proj, proj, proj, proj, lb_gamma_fwd, lb_gamma_bwd)


CONV_HALO = 16
CONV_ROWS = 64


def _conv_kernel(a_ref, g_ref, ap_ref, gp_ref, an_ref, gn_ref, w_ref, b_ref, lg_ref, lb_ref,
                 o_ref, hbuf, hsh, *, tl):
    j = pl.program_id(1)
    last = pl.num_programs(1) - 1
    hbuf[CONV_HALO:CONV_HALO + tl, :] = a_ref[0] * jax.nn.sigmoid(g_ref[0])
    hp = ap_ref[0] * jax.nn.sigmoid(gp_ref[0])
    hbuf[0:CONV_HALO, :] = jnp.where(j > 0, hp, 0.0)
    hn = an_ref[0] * jax.nn.sigmoid(gn_ref[0])
    hbuf[CONV_HALO + tl:2 * CONV_HALO + tl, :] = jnp.where(j < last, hn, 0.0)
    rows = hsh.shape[1]
    for o in range(SUBLANES):
        hsh[o] = hbuf[o:o + rows, :]
    off = CONV_HALO - CONV_PAD
    for r in range(tl // CONV_ROWS):
        acc = jnp.zeros((CONV_ROWS, a_ref.shape[2]), F32)
        for k in range(CONV_WIDTH):
            s = r * CONV_ROWS + k + off
            a0 = s - s % SUBLANES
            acc = acc + w_ref[k:k + 1, :] * hsh[s % SUBLANES, a0:a0 + CONV_ROWS, :]
        acc = acc + b_ref[...]
        mu = jnp.mean(acc, axis=-1, keepdims=True)
        cen = acc - mu
        var = jnp.mean(cen * cen, axis=-1, keepdims=True)
        y = cen * lax.rsqrt(var + EPS) * lg_ref[...] + lb_ref[...]
        o_ref[0, r * CONV_ROWS:(r + 1) * CONV_ROWS, :] = _silu(y)


def _conv(proj, w_pad, bias, ln_g, ln_b, a_col):
    nb, L, _ = proj.shape
    dc = w_pad.shape[1]
    tl = 256
    hb = tl // CONV_HALO
    nhalo = L // CONV_HALO

    def cur(c):
        return pl.BlockSpec((1, tl, dc), lambda b, j: (b, j, c))

    def prev(c):
        return pl.BlockSpec((1, CONV_HALO, dc), lambda b, j: (b, jnp.maximum(j * hb - 1, 0), c))

    def nxt(c):
        return pl.BlockSpec((1, CONV_HALO, dc),
                            lambda b, j: (b, jnp.minimum((j + 1) * hb, nhalo - 1), c))

    vec = pl.BlockSpec((1, dc), lambda b, j: (0, 0))
    return pl.pallas_call(
        functools.partial(_conv_kernel, tl=tl),
        grid=(nb, L // tl),
        in_specs=[cur(a_col), cur(a_col + 1), prev(a_col), prev(a_col + 1), nxt(a_col),
                  nxt(a_col + 1), pl.BlockSpec(w_pad.shape, lambda b, j: (0, 0)), vec, vec, vec],
        out_specs=pl.BlockSpec((1, tl, dc), lambda b, j: (b, j, 0)),
        out_shape=jax.ShapeDtypeStruct((nb, L, dc), F32),
        scratch_shapes=[pltpu.VMEM((tl + 2 * CONV_HALO, dc), F32),
                        pltpu.VMEM((SUBLANES, tl + 2 * CONV_HALO - SUBLANES, dc), F32)],
        compiler_params=_cparams(("arbitrary", "arbitrary")),
        name="conv_group",
    )(proj, proj, proj, proj, proj, proj, w_pad, bias.reshape(1, dc), ln_g.reshape(1, dc),
      ln_b.reshape(1, dc))


def _mix_kernel(of_ref, ob_ref, gr_ref, oc_ref, x_ref, hg_ref, g1_ref, sh2_ref, sc2_ref, ng_ref,
                wo_ref, wq_ref, k1_ref, k2_ref, h1_ref, hp_ref, s_ref):
    o = of_ref[0] + ob_ref[0]
    parts = []
    for hh in range(o.shape[1] // HEAD_DIM):
        oh = o[:, hh * HEAD_DIM:(hh + 1) * HEAD_DIM]
        parts.append(oh * lax.rsqrt(jnp.mean(oh * oh, axis=-1, keepdims=True) + EPS))
    on = jnp.concatenate(parts, axis=-1) * hg_ref[...] * _silu(gr_ref[0])
    cat = jnp.concatenate([on, oc_ref[0]], axis=-1).astype(BF16)
    mix = jnp.dot(cat, wo_ref[...], preferred_element_type=F32)
    h1 = x_ref[0] + g1_ref[0] * mix
    h1_ref[0] = h1
    hn2 = _rms_mod(h1, ng_ref[...], sh2_ref[0], sc2_ref[0])
    hp_ref[0] = _pack_words(hn2)
    q = jnp.dot(hn2.astype(BF16), wq_ref[...], preferred_element_type=F32)
    k1h, k1l, _ = _split3(k1_ref[...])
    k2h, k2l, _ = _split3(k2_ref[...])
    for hd in range(PEER_HEADS):
        for half, (kh, kl) in enumerate(((k1h, k1l), (k2h, k2l))):
            c0 = hd * 2 * PEER_HALF + half * PEER_HALF
            qh, ql, _ = _split3(q[:, c0:c0 + PEER_HALF])
            s_ref[2 * hd + half] = _dot_nt(kh, qh) + _dot_nt(kh, ql) + _dot_nt(kl, qh)


def _mix(o_f, o_b, proj, o_c, x, b0, hgrn_g, mod3, norm_g, w_out_bf16, wq_bf16, keys1, keys2, g_col):
    _, L, d = x.shape
    dh = o_f.shape[2]
    dq = wq_bf16.shape[1]
    tt = 256
    nj = L // tt

    def half(c=0):
        return pl.BlockSpec((1, tt, dh), lambda b, j: (0, j, c))

    def full(bb=0):
        return pl.BlockSpec((1, tt, d), lambda b, j: (bb, j, 0))

    def modc(c):
        return pl.BlockSpec((1, 1, d), lambda b, j: (0, 0, c))

    def const(shape):
        return pl.BlockSpec(shape, lambda b, j: (0,) * len(shape))

    return pl.pallas_call(
        _mix_kernel,
        grid=(1, nj),
        in_specs=[half(), half(), half(g_col), half(), full(b0), const((1, dh)),
                  modc(2), modc(3), modc(4), const((1, d)), const((d, d)), const((d, dq)),
                  const(keys1.shape), const(keys2.shape)],
        out_specs=[full(), pl.BlockSpec((1, tt, d // 2), lambda b, j: (0, j, 0)),
                   pl.BlockSpec((2 * PEER_HEADS, PEER_KEYS, tt), lambda b, j: (0, 0, j))],
        out_shape=[jax.ShapeDtypeStruct((1, L, d), F32), jax.ShapeDtypeStruct((1, L, d // 2), I32),
                   jax.ShapeDtypeStruct((2 * PEER_HEADS, PEER_KEYS, L), F32)],
        compiler_params=_cparams(("arbitrary", "arbitrary")),
        name="mix_scores",
    )(o_f, o_b, proj, o_c, x, hgrn_g.reshape(1, dh), mod3, mod3, mod3, norm_g.reshape(1, d),
      w_out_bf16, wq_bf16, keys1, keys2)


def _top16(s, payload=None):
    n = s.shape[0]
    iota = lax.broadcasted_iota(I32, s.shape, 0).astype(F32)
    vals, tags = [], []
    for _ in range(PEER_TOPK):
        m = jnp.max(s, axis=0, keepdims=True)
        idx = jnp.min(jnp.where(s == m, iota, float(n)), axis=0, keepdims=True)
        hit = iota == idx
        vals.append(m)
        tags.append(idx if payload is None else jnp.max(jnp.where(hit, payload, -1.0), axis=0, keepdims=True))
        s = jnp.where(hit, -jnp.inf, s)
    return jnp.concatenate(vals, axis=0), jnp.concatenate(tags, axis=0)


def _pruned_pairs(x1, x2, combine):
    rows = [combine(x1[a:a + 1], x2[0:PEER_TOPK // (a + 1)]) for a in range(PEER_TOPK // 2)]
    rows.append(combine(x1[PEER_TOPK // 2:], x2[0:1]))
    return jnp.concatenate(rows, axis=0)


def _topk_kernel(s_ref, e_ref, w_ref):
    es, ws = [], []
    for hd in range(PEER_HEADS):
        v1, i1 = _top16(s_ref[2 * hd])
        v2, i2 = _top16(s_ref[2 * hd + 1])
        cand = _pruned_pairs(v1, v2, lambda a, b: a + b)
        cand_e = _pruned_pairs(i1, i2, lambda a, b: a * float(PEER_KEYS) + b)
        sc, e = _top16(cand, cand_e)
        p = jnp.exp(sc - sc[0:1])
        es.append(e.astype(I32))
        ws.append(p / jnp.sum(p, axis=0, keepdims=True))
    e_ref[...] = jnp.concatenate(es, axis=0).T
    w_ref[...] = jnp.concatenate(ws, axis=0).T


def _topk(scores):
    npair, nk, T = scores.shape
    nsel = PEER_HEADS * PEER_TOPK
    tt = 256
    return pl.pallas_call(
        _topk_kernel,
        grid=(T // tt,),
        in_specs=[pl.BlockSpec((npair, nk, tt), lambda i: (0, 0, i))],
        out_specs=[pl.BlockSpec((tt, nsel), lambda i: (i, 0)),
                   pl.BlockSpec((tt, nsel), lambda i: (i, 0))],
        out_shape=[jax.ShapeDtypeStruct((T, nsel), I32), jax.ShapeDtypeStruct((T, nsel), F32)],
        compiler_params=_cparams(("arbitrary",)),
        name="peer_topk",
    )(scores)


WORDS = 4


def _pack_words(x):
    bits = pltpu.bitcast(x.astype(BF16).astype(F32), I32)
    out = []
    for c in range(WORDS):
        lo = bits[:, (2 * c) * LANES:(2 * c + 1) * LANES]
        hi = bits[:, (2 * c + 1) * LANES:(2 * c + 2) * LANES]
        out.append(lax.shift_right_logical(lo, 16) | (hi & jnp.int32(-65536)))
    return jnp.concatenate(out, axis=-1)


def _pack_kernel(x_ref, o_ref, b_ref):
    te = x_ref.shape[0]
    x = x_ref[...]
    b_ref[...] = x.astype(BF16)
    words = _pack_words(x)
    for c in range(WORDS):
        o_ref[pl.ds(c, te, stride=WORDS), :] = words[:, c * LANES:(c + 1) * LANES]


def _pack_table(x):
    ne, d = x.shape
    te = 512
    packed, plain = pl.pallas_call(
        _pack_kernel,
        grid=(ne // te,),
        in_specs=[pl.BlockSpec((te, d), lambda i: (i, 0))],
        out_specs=[pl.BlockSpec((te * WORDS, LANES), lambda i: (i, 0)), pl.BlockSpec((te, d), lambda i: (i, 0))],
        out_shape=[jax.ShapeDtypeStruct((ne * WORDS, LANES), I32), jax.ShapeDtypeStruct((ne, d), BF16)],
        compiler_params=_cparams(("arbitrary",)),
        name="pack_table",
    )(x)
    return packed.reshape(ne, WORDS, LANES), plain


NSEL = PEER_HEADS * PEER_TOPK
ROW_TILE = 2 * WORDS


SC_LANES = 16
SC_ROWS = 32
SC_BUFS = 4
SC_TOK = 16
SC_GROUP = 8
VS_GROUP = 2
PART_ROWS = NSEL * SC_LANES // LANES


def _sc_unpack(w):
    lo = plsc.bitcast(lax.shift_left(w, jnp.int32(16)), F32)
    hi = plsc.bitcast(w & jnp.int32(-65536), F32)
    return lo, hi


def _sc_mesh():
    return plsc.VectorSubcoreMesh(core_axis_name="c", subcore_axis_name="s")


def _sc_params():
    return pltpu.CompilerParams(use_tc_tiling_on_sc=True, needs_layout_passes=False)


def _sc_udot(table3, idx, hn3, n_tok):
    T = n_tok
    info = plsc.get_sparse_core_info()
    nw = info.num_cores * info.num_subcores
    tpw = T // nw
    assert T % nw == 0 and tpw % SC_TOK == 0 and NSEL % SC_ROWS == 0
    nchunk = tpw // SC_TOK
    qsteps = NSEL // SC_ROWS
    nstep = SC_TOK * qsteps

    @functools.partial(
        pl.kernel, mesh=_sc_mesh(),
        out_type=jax.ShapeDtypeStruct((T, PART_ROWS, LANES), F32),
        scratch_types=[pltpu.VMEM((SC_TOK * NSEL,), I32),
                       pltpu.VMEM((SC_BUFS, SC_ROWS, WORDS, LANES), I32),
                       pltpu.VMEM((SC_TOK, WORDS * LANES), I32),
                       pltpu.VMEM((SC_TOK, PART_ROWS, LANES), F32),
                       pltpu.SemaphoreType.DMA((SC_BUFS,))],
        compiler_params=_sc_params(),
        name="sc_udot",
    )
    def k(table_hbm, idx_hbm, hn_hbm, out_hbm, idx_v, rows_v, h_v, p_v, sem_g):
        wid = lax.axis_index("s") * info.num_cores + lax.axis_index("c")
        tbase = wid * tpw

        def gather(st, slot):
            return pltpu.make_async_copy(table_hbm.at[idx_v.at[pl.ds(st * SC_ROWS, SC_ROWS)]],
                                         rows_v.at[slot], sem_g.at[slot])

        def compute(tk, q, slot):
            @pl.loop(0, SC_ROWS // SC_GROUP)
            def _(g):
                acc = [jnp.zeros((SC_LANES,), F32) for _ in range(SC_GROUP)]
                for c in range(WORDS):
                    for lv in range(0, LANES // SC_LANES, 2):
                        ls = [pl.ds((lv + d) * SC_LANES, SC_LANES) for d in range(2)]
                        hb = [plsc.bitcast(h_v[tk, pl.ds(c * LANES + (lv + d) * SC_LANES, SC_LANES)], BF16)
                              for d in range(2)]
                        for i in range(SC_GROUP):
                            pr = [plsc.bitcast(rows_v[slot, g * SC_GROUP + i, c, ls[d]], BF16) * hb[d]
                                  for d in range(2)]
                            lo, hi = _sc_unpack(plsc.bitcast(pr[0] + pr[1], I32))
                            acc[i] = acc[i] + lo + hi
                for i in range(SC_GROUP):
                    p_v[tk, q * (SC_ROWS // SC_GROUP) + g, pl.ds(i * SC_LANES, SC_LANES)] = acc[i]

        @pl.loop(0, nchunk)
        def _(ch):
            t0 = tbase + ch * SC_TOK
            pltpu.sync_copy(idx_hbm.at[pl.ds(t0 * NSEL, SC_TOK * NSEL)], idx_v)
            pltpu.sync_copy(hn_hbm.at[pl.ds(t0, SC_TOK)], h_v)
            for s in range(SC_BUFS - 1):
                gather(s, s).start()

            @pl.loop(0, SC_TOK)
            def _(tk):
                for q in range(qsteps):
                    st = tk * qsteps + q
                    slot = q % SC_BUFS

                    @pl.when(st + SC_BUFS - 1 < nstep)
                    def _():
                        gather(st + SC_BUFS - 1, (q + SC_BUFS - 1) % SC_BUFS).start()

                    gather(st, slot).wait()
                    compute(tk, q, slot)

            pltpu.sync_copy(p_v, out_hbm.at[pl.ds(t0, SC_TOK)])

    return k(table3, idx, hn3)


def _sc_vsum(table3, idx, coefx):
    assert VS_GROUP == 2, "the loop body sums the products of exactly two experts in bf16"
    T = coefx.shape[0]
    info = plsc.get_sparse_core_info()
    nw = info.num_cores * info.num_subcores
    tpw = T // nw
    assert T % nw == 0 and tpw % SC_TOK == 0 and NSEL % SC_ROWS == 0 and SC_GROUP % VS_GROUP == 0
    nchunk = tpw // SC_TOK
    qsteps = NSEL // SC_ROWS
    nstep = SC_TOK * qsteps
    nlv = LANES // SC_LANES
    sub = SC_GROUP // VS_GROUP

    @functools.partial(
        pl.kernel, mesh=_sc_mesh(),
        out_type=jax.ShapeDtypeStruct((T, ROW_TILE * LANES), F32),
        scratch_types=[pltpu.VMEM((SC_TOK * NSEL,), I32),
                       pltpu.VMEM((SC_BUFS, SC_ROWS, WORDS, LANES), I32),
                       pltpu.VMEM((SC_TOK, PART_ROWS, LANES), I32),
                       pltpu.VMEM((SC_TOK, ROW_TILE * LANES), F32),
                       pltpu.SemaphoreType.DMA((SC_BUFS,))],
        compiler_params=_sc_params(),
        name="sc_vsum",
    )
    def k(table_hbm, idx_hbm, coef_hbm, out_hbm, idx_v, rows_v, c_v, y_v, sem_g):
        wid = lax.axis_index("s") * info.num_cores + lax.axis_index("c")
        tbase = wid * tpw

        def gather(st, slot):
            return pltpu.make_async_copy(table_hbm.at[idx_v.at[pl.ds(st * SC_ROWS, SC_ROWS)]],
                                         rows_v.at[slot], sem_g.at[slot])

        def compute(tk, q, slot):
            for c in range(WORDS):
                if q == 0:
                    init = tuple(jnp.zeros((SC_LANES,), F32) for _ in range(2 * nlv))
                else:
                    init = tuple(y_v[tk, pl.ds((2 * c + p) * LANES + lv * SC_LANES, SC_LANES)]
                                 for p in range(2) for lv in range(nlv))

                def body(g, acc):
                    acc = list(acc)
                    cb = [plsc.bitcast(c_v[tk, q * (SC_ROWS // SC_GROUP) + g // sub,
                                           pl.ds(((g % sub) * VS_GROUP + i) * SC_LANES, SC_LANES)], BF16)
                          for i in range(VS_GROUP)]
                    for lv in range(nlv):
                        pr = [cb[i] * plsc.bitcast(rows_v[slot, g * VS_GROUP + i, c,
                                                          pl.ds(lv * SC_LANES, SC_LANES)], BF16)
                              for i in range(VS_GROUP)]
                        lo, hi = _sc_unpack(plsc.bitcast(pr[0] + pr[1], I32))
                        acc[lv] = acc[lv] + lo
                        acc[nlv + lv] = acc[nlv + lv] + hi
                    return tuple(acc)

                acc = lax.fori_loop(0, SC_ROWS // VS_GROUP, body, init)
                for p in range(2):
                    for lv in range(nlv):
                        y_v[tk, pl.ds((2 * c + p) * LANES + lv * SC_LANES, SC_LANES)] = acc[p * nlv + lv]

        @pl.loop(0, nchunk)
        def _(ch):
            t0 = tbase + ch * SC_TOK
            pltpu.sync_copy(idx_hbm.at[pl.ds(t0 * NSEL, SC_TOK * NSEL)], idx_v)
            pltpu.sync_copy(coef_hbm.at[pl.ds(t0, SC_TOK)], c_v)
            for s in range(SC_BUFS - 1):
                gather(s, s).start()

            @pl.loop(0, SC_TOK)
            def _(tk):
                for q in range(qsteps):
                    st = tk * qsteps + q
                    slot = q % SC_BUFS

                    @pl.when(st + SC_BUFS - 1 < nstep)
                    def _():
                        gather(st + SC_BUFS - 1, (q + SC_BUFS - 1) % SC_BUFS).start()

                    gather(st, slot).wait()
                    compute(tk, q, slot)

            pltpu.sync_copy(y_v, out_hbm.at[pl.ds(t0, SC_TOK)])

    return k(table3, idx, coefx)


def _coef_kernel(p_ref, w_ref, cx_ref, *, tt):
    row = lax.broadcasted_iota(I32, (LANES, NSEL), 0)
    col = lax.broadcasted_iota(I32, (LANES, NSEL), 1)
    per_row = LANES // SC_LANES
    act = jnp.zeros((tt, NSEL), F32)
    for s in range(PART_ROWS):
        fold = (col == per_row * s + row // SC_LANES).astype(BF16)
        p1, p2, p3 = _split3(p_ref[pl.ds(s, tt, stride=PART_ROWS), :])
        act = act + (jnp.dot(p1, fold, preferred_element_type=F32) + jnp.dot(p2, fold, preferred_element_type=F32)
                     + jnp.dot(p3, fold, preferred_element_type=F32))
    coef = w_ref[...] * (0.5 * act * (1.0 + lax.erf(act * (2.0 ** -0.5))))
    cb = coef.astype(BF16)
    for s in range(PART_ROWS):
        spread = (row == per_row * s + col // SC_LANES).astype(BF16)
        bits = pltpu.bitcast(jnp.dot(cb, spread, preferred_element_type=F32), I32)
        cx_ref[pl.ds(s, tt, stride=PART_ROWS), :] = bits | lax.shift_right_logical(bits, 16)


def _coef(p2d, w):
    T = p2d.shape[0] // PART_ROWS
    tt = 256
    return pl.pallas_call(
        functools.partial(_coef_kernel, tt=tt),
        grid=(T // tt,),
        in_specs=[pl.BlockSpec((tt * PART_ROWS, LANES), lambda i: (i, 0)),
                  pl.BlockSpec((tt, NSEL), lambda i: (i, 0))],
        out_specs=pl.BlockSpec((tt * PART_ROWS, LANES), lambda i: (i, 0)),
        out_shape=jax.ShapeDtypeStruct((T * PART_ROWS, LANES), I32),
        compiler_params=_cparams(("arbitrary",)),
        name="peer_coef",
    )(p2d, w)


SC_FRACTION = 0.375
SPLIT_UNIT = 1024
WD_BUFS = 4
WD_TOK = 16


def _sc_wdense(e_flat, w_flat, n_skip, n_tok, nexp):
    info = plsc.get_sparse_core_info()
    nw = info.num_cores * info.num_subcores
    tpw = n_tok // nw
    assert n_tok % nw == 0 and tpw % WD_TOK == 0 and WD_TOK % WD_BUFS == 0 and nexp % SC_LANES == 0
    nchunk = tpw // WD_TOK
    heads = NSEL // SC_LANES

    @functools.partial(
        pl.kernel, mesh=_sc_mesh(),
        out_type=jax.ShapeDtypeStruct((n_tok, nexp), F32),
        scratch_types=[pltpu.VMEM((WD_TOK * NSEL,), I32), pltpu.VMEM((WD_TOK * NSEL,), F32)]
        + [pltpu.VMEM((nexp,), F32) for _ in range(WD_BUFS)]
        + [pltpu.SemaphoreType.DMA((WD_BUFS,))],
        compiler_params=_sc_params(),
        name="sc_wdense",
    )
    def k(e_hbm, w_hbm, out_hbm, idx_v, w_v, *rest):
        rows, sem = rest[:WD_BUFS], rest[WD_BUFS]
        wid = lax.axis_index("s") * info.num_cores + lax.axis_index("c")
        tbase = wid * tpw
        zeros = jnp.zeros((SC_LANES,), F32)

        for s in range(WD_BUFS):
            @pl.loop(0, nexp // SC_LANES)
            def _(i):
                rows[s][pl.ds(i * SC_LANES, SC_LANES)] = zeros

        def put(tok, s):
            return pltpu.make_async_copy(rows[s], out_hbm.at[tok], sem.at[s])

        def scatter(tk, s, clear):
            for h in range(heads):
                sl = pl.ds(tk * NSEL + h * SC_LANES, SC_LANES)
                if clear:
                    plsc.store_scatter(rows[s], [idx_v[sl]], zeros)
                else:
                    plsc.addupdate_scatter(rows[s], [idx_v[sl]], w_v[sl])

        @pl.loop(0, nchunk)
        def _(ch):
            t0 = tbase + ch * WD_TOK
            pltpu.sync_copy(e_hbm.at[pl.ds((n_skip + t0) * NSEL, WD_TOK * NSEL)], idx_v)
            pltpu.sync_copy(w_hbm.at[pl.ds((n_skip + t0) * NSEL, WD_TOK * NSEL)], w_v)

            @pl.loop(0, WD_TOK, step=WD_BUFS)
            def _(tk0):
                for s in range(WD_BUFS):
                    tk = tk0 + s

                    @pl.when(tk0 > 0)
                    def _():
                        put(0, s).wait()
                        scatter(tk - WD_BUFS, s, True)

                    scatter(tk, s, False)
                    put(t0 + tk, s).start()

            for s in range(WD_BUFS):
                put(0, s).wait()
                scatter(WD_TOK - WD_BUFS + s, s, True)

    return k(e_flat, w_flat)


def _dense_kernel(hp_ref, wd_ref, u_ref, v_ref, y_ref, acc_ref):
    k = pl.program_id(1)

    @pl.when(k == 0)
    def _():
        acc_ref[...] = jnp.zeros_like(acc_ref)

    words = hp_ref[...]
    lo = pltpu.bitcast(lax.shift_left(words, 16), F32)
    hi = pltpu.bitcast(words & jnp.int32(-65536), F32)
    parts = []
    for c in range(WORDS):
        parts += [lo[:, c * LANES:(c + 1) * LANES], hi[:, c * LANES:(c + 1) * LANES]]
    hn = jnp.concatenate(parts, axis=-1).astype(BF16)
    act = _dot_nt(hn, u_ref[...])
    wd = wd_ref[...]
    coef = jnp.where(wd != 0.0, wd * (0.5 * act * (1.0 + lax.erf(act * (2.0 ** -0.5)))), 0.0)
    acc_ref[...] += jnp.dot(coef.astype(BF16), v_ref[...], preferred_element_type=F32)

    @pl.when(k == pl.num_programs(1) - 1)
    def _():
        y_ref[...] = acc_ref[...]


def _dense_experts(hp, wd, u_bf16, v_bf16, n_skip):
    n_tok, nexp = wd.shape
    dw = hp.shape[1]
    d = u_bf16.shape[1]
    tt = min(SPLIT_UNIT, n_tok)
    eb = 1024
    skip = n_skip // tt
    return pl.pallas_call(
        _dense_kernel,
        grid=(n_tok // tt, nexp // eb),
        in_specs=[pl.BlockSpec((tt, dw), lambda i, k: (skip + i, 0)),
                  pl.BlockSpec((tt, eb), lambda i, k: (i, k)),
                  pl.BlockSpec((eb, d), lambda i, k: (k, 0)),
                  pl.BlockSpec((eb, d), lambda i, k: (k, 0))],
        out_specs=pl.BlockSpec((tt, d), lambda i, k: (i, 0)),
        out_shape=jax.ShapeDtypeStruct((n_tok, d), F32),
        scratch_shapes=[pltpu.VMEM((tt, d), F32)],
        compiler_params=_cparams(("arbitrary", "arbitrary")),
        name="peer_dense",
    )(hp, wd, u_bf16, v_bf16)


def _final_kernel(h1_ref, ya_ref, yb_ref, g2_ref, fg_ref, fsh_ref, fsc_ref, *refs, na):
    o_ref = refs[-1]
    y = jnp.where(pl.program_id(1) < na, ya_ref[...], yb_ref[...])
    h = h1_ref[0] + g2_ref[0] * y
    o_ref[0] = _rms_mod(h, fg_ref[...], fsh_ref[0], fsc_ref[0])


def _final(h1, ya, yb, mod3, fmod3, final_g, out_prev, b0, nb):
    _, L, d = h1.shape
    tt = 512
    na, nbk = ya.shape[0] // tt, yb.shape[0] // tt
    in_specs = [pl.BlockSpec((1, tt, d), lambda b, j: (0, j, 0)),
                pl.BlockSpec((tt, d), lambda b, j: (jnp.minimum(j, na - 1), 0)),
                pl.BlockSpec((tt, d), lambda b, j: (jnp.clip(j - na, 0, nbk - 1), 0)),
                pl.BlockSpec((1, 1, d), lambda b, j: (0, 0, 5)),
                pl.BlockSpec((1, d), lambda b, j: (0, 0)),
                pl.BlockSpec((1, 1, d), lambda b, j: (0, 0, 0)),
                pl.BlockSpec((1, 1, d), lambda b, j: (0, 0, 1))]
    args = [h1, ya, yb, mod3, final_g.reshape(1, d), fmod3, fmod3]
    aliases = {}
    if out_prev is not None:
        in_specs.append(pl.BlockSpec(memory_space=pl.ANY))
        aliases = {len(args): 0}
        args.append(out_prev)
    return pl.pallas_call(
        functools.partial(_final_kernel, na=na),
        grid=(1, L // tt),
        in_specs=in_specs,
        out_specs=pl.BlockSpec((1, tt, d), lambda b, j: (b0, j, 0)),
        out_shape=jax.ShapeDtypeStruct((nb, L, d), F32),
        input_output_aliases=aliases,
        compiler_params=_cparams(("arbitrary", "arbitrary")),
        name="final_norm",
    )(*args)


def kernel(x, c, ada_w, ada_b, norm_mix_g, w_in, lb_gamma_fwd, lb_gamma_bwd, hgrn_norm_g, conv_w,
           conv_b, conv_ln_g, conv_ln_b, w_out, norm_ffn_g, peer_wq, peer_keys1, peer_keys2, peer_u,
           peer_v, final_ada_w, final_ada_b, final_norm_g):
    nb, L, d = x.shape
    assert ada_w.shape[0] == 1, "single-layer trunk"
    d_hgrn = lb_gamma_fwd.shape[1]

    c_pad = jnp.pad(c, ((0, SUBLANES - nb), (0, 0)))
    mod3 = _ada(c_pad, ada_w[0], ada_b[0])[:nb].reshape(nb, 1, 6 * d)
    fmod3 = _ada(c_pad, final_ada_w, final_ada_b)[:nb].reshape(nb, 1, 2 * d)

    d_conv = conv_w.shape[2]
    w_pad = jnp.pad(conv_w[0], ((0, 1), (0, 0)))
    w_in_bf, w_out_bf, wq_bf = w_in[0].astype(BF16), w_out[0].astype(BF16), peer_wq[0].astype(BF16)
    mb, (u_rows, u_bf) = lax.optimization_barrier((mod3[0:1], _pack_table(peer_u[0])))
    nexp = peer_u.shape[1]
    n_sc = int(L * SC_FRACTION) // SPLIT_UNIT * SPLIT_UNIT
    v_src, v_rows, v_bf = peer_v[0], None, None
    work = []

    def weighted_sum(item):
        return _sc_vsum(v_rows, item["idx"], item["cx"].reshape(n_sc, PART_ROWS, LANES))

    for b in range(nb):
        proj = _inproj(x, b, norm_mix_g[0], mb, w_in_bf)
        o_f, o_b = _hgrn(proj, lb_gamma_fwd, lb_gamma_bwd)
        o_c = _conv(proj, w_pad, conv_b[0], conv_ln_g[0], conv_ln_b[0], a_col=5 * d_hgrn // d_conv)
        h1, hp, scores = _mix(o_f, o_b, proj, o_c, x, b, hgrn_norm_g[0], mb, norm_ffn_g[0], w_out_bf,
                              wq_bf, peer_keys1[0], peer_keys2[0], g_col=4)
        e, w = _topk(scores)
        item = dict(h1=h1, mb=mb, hp=hp.reshape(L, d // 2))
        tied = dict(e=e, w=w)
        if b + 1 < nb:
            tied["mb"] = mod3[b + 1:b + 2]
        if work:
            prev = work[-1]
            tied["wd"] = prev["wd"]
            tied["cx"] = _coef(prev["p"].reshape(n_sc * PART_ROWS, LANES), prev["w"])
            if len(work) > 1:
                tied["y"] = work[-2]["y_sc"]
        elif nb > 1:
            tied["v"] = v_src
        tied = lax.optimization_barrier(tied)
        e, w, mb = tied["e"], tied["w"], tied.get("mb")
        if "v" in tied:
            v_rows, v_bf = _pack_table(tied["v"])
        if work:
            prev["wd"], prev["cx"] = tied["wd"], tied["cx"]
            if "y" in tied:
                work[-2]["y_sc"] = tied["y"]
            prev["y_sc"] = weighted_sum(prev)
        item["idx"], item["w"] = e.reshape(L * NSEL), w
        item["p"] = _sc_udot(u_rows, item["idx"], item["hp"], n_sc)
        item["wd"] = _sc_wdense(item["idx"], w.reshape(L * NSEL), n_sc, L - n_sc, nexp)
        work.append(item)

    if v_rows is None:
        v_rows, v_bf = _pack_table(v_src)
    y_dn = [_dense_experts(item["hp"], item["wd"], u_bf, v_bf, n_sc) for item in work[:-1]]
    last = work[-1]
    last["p"], y_dn = lax.optimization_barrier((last["p"], y_dn))
    last["cx"] = _coef(last["p"].reshape(n_sc * PART_ROWS, LANES), last["w"])
    last["y_sc"] = weighted_sum(last)
    y_dn.append(_dense_experts(last["hp"], last["wd"], u_bf, v_bf, n_sc))
    out = None
    for b, item in enumerate(work):
        out = _final(item["h1"], item["y_sc"], y_dn[b], item["mb"], fmod3[b:b + 1], final_norm_g, out, b, nb)
    return out
```

```python
import functools

import jax
import jax.numpy as jnp
from jax import lax
from jax.experimental import pallas as pl
from jax.experimental.pallas import tpu as pltpu
from jax.experimental.pallas import tpu_sc as plsc

F32 = jnp.float32
BF16 = jnp.bfloat16
I32 = jnp.int32

EPS = 1e-6
HEAD_DIM = 128
CHUNK = 64
CONV_WIDTH = 31
CONV_PAD = CONV_WIDTH // 2
PEER_HEADS = 8
PEER_KEYS = 128
PEER_TOPK = 16
PEER_HALF = 128
LANES = 128
SUBLANES = 8
VMEM_LIMIT = 48 * 1024 * 1024


def _cparams(sem):
    return pltpu.CompilerParams(dimension_semantics=sem, vmem_limit_bytes=VMEM_LIMIT)


def _silu(x):
    return x * jax.nn.sigmoid(x)


def _split3(x):
    x1 = x.astype(BF16)
    r1 = x - x1.astype(F32)
    x2 = r1.astype(BF16)
    x3 = (r1 - x2.astype(F32)).astype(BF16)
    return x1, x2, x3


def _dot_nt(a, b):
    return lax.dot_general(a, b, (((1,), (1,)), ((), ())), preferred_element_type=F32)


def _dot_tn(a, b):
    return lax.dot_general(a, b, (((0,), (0,)), ((), ())), preferred_element_type=F32)


def _ada_kernel(c_ref, w_ref, b_ref, o_ref):
    ca = _silu(c_ref[...])
    o_ref[...] = jnp.dot(ca, w_ref[...], preferred_element_type=F32,
                         precision=lax.Precision.HIGHEST) + b_ref[...]


def _ada(c_pad, w, b):
    d, n = w.shape
    tn = 1024
    return pl.pallas_call(
        _ada_kernel,
        grid=(n // tn,),
        in_specs=[pl.BlockSpec((SUBLANES, d), lambda j: (0, 0)),
                  pl.BlockSpec((d, tn), lambda j: (0, j)),
                  pl.BlockSpec((1, tn), lambda j: (0, j))],
        out_specs=pl.BlockSpec((SUBLANES, tn), lambda j: (0, j)),
        out_shape=jax.ShapeDtypeStruct((SUBLANES, n), F32),
        compiler_params=_cparams(("arbitrary",)),
        name="ada_proj",
    )(c_pad, w, b.reshape(1, n))


def _rms_mod(x, g, sh, sc):
    y = x * lax.rsqrt(jnp.mean(x * x, axis=-1, keepdims=True) + EPS) * g
    return y * (1.0 + sc) + sh


def _inproj_kernel(x_ref, g_ref, sh_ref, sc_ref, w_ref, o_ref):
    hn = _rms_mod(x_ref[0], g_ref[...], sh_ref[0], sc_ref[0])
    o_ref[0] = jnp.dot(hn.astype(BF16), w_ref[...], preferred_element_type=F32)


def _inproj(x, b0, g, mod3, w_bf16):
    _, L, d = x.shape
    n = w_bf16.shape[1]
    tt = 512
    return pl.pallas_call(
        _inproj_kernel,
        grid=(1, L // tt),
        in_specs=[pl.BlockSpec((1, tt, d), lambda b, j: (b0, j, 0)),
                  pl.BlockSpec((1, d), lambda b, j: (0, 0)),
                  pl.BlockSpec((1, 1, d), lambda b, j: (0, 0, 0)),
                  pl.BlockSpec((1, 1, d), lambda b, j: (0, 0, 1)),
                  pl.BlockSpec((d, n), lambda b, j: (0, 0))],
        out_specs=pl.BlockSpec((1, tt, n), lambda b, j: (0, j, 0)),
        out_shape=jax.ShapeDtypeStruct((1, L, n), F32),
        compiler_params=_cparams(("arbitrary", "arbitrary")),
        name="in_proj",
    )(x, g.reshape(1, d), mod3, mod3, w_bf16)


def _lower_bound(lbg):
    ex = jnp.exp(lbg - jnp.max(lbg, axis=0, keepdims=True))
    return ex[0:1] / jnp.sum(ex, axis=0, keepdims=True)


def _hgrn_chunk(q_ref, f_ref, i_ref, o_ref, ls, lb, st, c, reverse):
    row = lax.broadcasted_iota(I32, (CHUNK, CHUNK), 0)
    col = lax.broadcasted_iota(I32, (CHUNK, CHUNK), 1)
    keep = (col >= row) if reverse else (col <= row)
    tri = keep.astype(BF16)
    i_mid = CHUNK // 2 if reverse else CHUNK // 2 - 1
    i_end = 0 if reverse else CHUNK - 1
    r0 = pl.multiple_of(c * CHUNK, CHUNK)
    z = f_ref[0, pl.ds(r0, CHUNK), ls]
    q = _silu(q_ref[0, pl.ds(r0, CHUNK), ls])
    v = i_ref[0, pl.ds(r0, CHUNK), ls]
    logf = jnp.log(lb + (1.0 - lb) * jax.nn.sigmoid(z))
    k = (1.0 - lb) * jax.nn.sigmoid(-z)
    l1, l2, l3 = _split3(logf)
    cum = (jnp.dot(tri, l1, preferred_element_type=F32)
           + jnp.dot(tri, l2, preferred_element_type=F32)
           + jnp.dot(tri, l3, preferred_element_type=F32))
    c_mid = cum[i_mid:i_mid + 1]
    c_end = cum[i_end:i_end + 1]
    qr = (q * jnp.exp(cum - c_mid)).astype(BF16)
    kr = (k * jnp.exp(c_mid - cum)).astype(BF16)
    scores = jnp.where(keep, _dot_nt(qr, kr), 0.0)
    vb = v.astype(BF16)
    o_intra = jnp.dot(scores.astype(BF16), vb, preferred_element_type=F32)
    o_inter = _dot_nt((q * jnp.exp(cum)).astype(BF16), st.astype(BF16))
    o_ref[0, pl.ds(r0, CHUNK), ls] = o_intra + o_inter
    ku = (k * jnp.exp(c_end - cum)).astype(BF16)
    return st * jnp.exp(c_end) + _dot_tn(vb, ku)


HGRN_GROUP = 4


def _hgrn_kernel(qf_ref, ff_ref, if_ref, qb_ref, fb_ref, ib_ref, lbf_ref, lbb_ref, of_ref, ob_ref,
                 stf_ref, stb_ref, *, lblk):
    @pl.when(pl.program_id(2) == 0)
    def _():
        stf_ref[...] = jnp.zeros_like(stf_ref)
        stb_ref[...] = jnp.zeros_like(stb_ref)

    lanes = [slice(g * HEAD_DIM, (g + 1) * HEAD_DIM) for g in range(HGRN_GROUP)]
    lb_f = _lower_bound(lbf_ref[...])
    lb_b = _lower_bound(lbb_ref[...])
    nch = lblk // CHUNK

    def body(ci, carry):
        new = []
        for g, ls in enumerate(lanes):
            new.append(_hgrn_chunk(qf_ref, ff_ref, if_ref, of_ref, ls, lb_f[:, ls], carry[2 * g], ci, False))
            new.append(_hgrn_chunk(qb_ref, fb_ref, ib_ref, ob_ref, ls, lb_b[:, ls], carry[2 * g + 1],
                                   nch - 1 - ci, True))
        return tuple(new)

    init = tuple(ref[g] for g in range(HGRN_GROUP) for ref in (stf_ref, stb_ref))
    final = lax.fori_loop(0, nch, body, init)
    for g in range(HGRN_GROUP):
        stf_ref[g] = final[2 * g]
        stb_ref[g] = final[2 * g + 1]


def _hgrn(proj, lb_gamma_fwd, lb_gamma_bwd):
    nb, L, _ = proj.shape
    nh = lb_gamma_fwd.shape[1] // HEAD_DIM
    assert nh % HGRN_GROUP == 0
    ng = nh // HGRN_GROUP
    gw = HGRN_GROUP * HEAD_DIM
    lblk = min(1024, L)
    nblk = L // lblk

    def fwd(c0):
        return pl.BlockSpec((1, lblk, gw), lambda b, h, j: (b, j, c0 + h))

    def bwd(c0):
        return pl.BlockSpec((1, lblk, gw), lambda b, h, j: (b, nblk - 1 - j, c0 + h))

    lbs = pl.BlockSpec((lb_gamma_fwd.shape[0], gw), lambda b, h, j: (0, h))
    out = jax.ShapeDtypeStruct((nb, L, nh * HEAD_DIM), F32)
    state = pltpu.VMEM((HGRN_GROUP, HEAD_DIM, HEAD_DIM), F32)
    return pl.pallas_call(
        functools.partial(_hgrn_kernel, lblk=lblk),
        grid=(nb, ng, nblk),
        in_specs=[fwd(0), fwd(ng), fwd(3 * ng), bwd(0), bwd(2 * ng), bwd(3 * ng), lbs, lbs],
        out_specs=[fwd(0), bwd(0)],
        out_shape=[out, out],
        scratch_shapes=[state, state],
        compiler_params=_cparams(("arbitrary", "arbitrary", "arbitrary")),
        name="hgrn_bidir",
    )(proj, proj, proj, proj, proj, proj, lb_gamma_fwd, lb_gamma_bwd)


CONV_HALO = 16
CONV_ROWS = 64


def _conv_kernel(a_ref, g_ref, ap_ref, gp_ref, an_ref, gn_ref, w_ref, b_ref, lg_ref, lb_ref,
                 o_ref, hbuf, hsh, *, tl):
    j = pl.program_id(1)
    last = pl.num_programs(1) - 1
    hbuf[CONV_HALO:CONV_HALO + tl, :] = a_ref[0] * jax.nn.sigmoid(g_ref[0])
    hp = ap_ref[0] * jax.nn.sigmoid(gp_ref[0])
    hbuf[0:CONV_HALO, :] = jnp.where(j > 0, hp, 0.0)
    hn = an_ref[0] * jax.nn.sigmoid(gn_ref[0])
    hbuf[CONV_HALO + tl:2 * CONV_HALO + tl, :] = jnp.where(j < last, hn, 0.0)
    rows = hsh.shape[1]
    for o in range(SUBLANES):
        hsh[o] = hbuf[o:o + rows, :]
    off = CONV_HALO - CONV_PAD
    for r in range(tl // CONV_ROWS):
        acc = jnp.zeros((CONV_ROWS, a_ref.shape[2]), F32)
        for k in range(CONV_WIDTH):
            s = r * CONV_ROWS + k + off
            a0 = s - s % SUBLANES
            acc = acc + w_ref[k:k + 1, :] * hsh[s % SUBLANES, a0:a0 + CONV_ROWS, :]
        acc = acc + b_ref[...]
        mu = jnp.mean(acc, axis=-1, keepdims=True)
        cen = acc - mu
        var = jnp.mean(cen * cen, axis=-1, keepdims=True)
        y = cen * lax.rsqrt(var + EPS) * lg_ref[...] + lb_ref[...]
        o_ref[0, r * CONV_ROWS:(r + 1) * CONV_ROWS, :] = _silu(y)


def _conv(proj, w_pad, bias, ln_g, ln_b, a_col):
    nb, L, _ = proj.shape
    dc = w_pad.shape[1]
    tl = 256
    hb = tl // CONV_HALO
    nhalo = L // CONV_HALO

    def cur(c):
        return pl.BlockSpec((1, tl, dc), lambda b, j: (b, j, c))

    def prev(c):
        return pl.BlockSpec((1, CONV_HALO, dc), lambda b, j: (b, jnp.maximum(j * hb - 1, 0), c))

    def nxt(c):
        return pl.BlockSpec((1, CONV_HALO, dc),
                            lambda b, j: (b, jnp.minimum((j + 1) * hb, nhalo - 1), c))

    vec = pl.BlockSpec((1, dc), lambda b, j: (0, 0))
    return pl.pallas_call(
        functools.partial(_conv_kernel, tl=tl),
        grid=(nb, L // tl),
        in_specs=[cur(a_col), cur(a_col + 1), prev(a_col), prev(a_col + 1), nxt(a_col),
                  nxt(a_col + 1), pl.BlockSpec(w_pad.shape, lambda b, j: (0, 0)), vec, vec, vec],
        out_specs=pl.BlockSpec((1, tl, dc), lambda b, j: (b, j, 0)),
        out_shape=jax.ShapeDtypeStruct((nb, L, dc), F32),
        scratch_shapes=[pltpu.VMEM((tl + 2 * CONV_HALO, dc), F32),
                        pltpu.VMEM((SUBLANES, tl + 2 * CONV_HALO - SUBLANES, dc), F32)],
        compiler_params=_cparams(("arbitrary", "arbitrary")),
        name="conv_group",
    )(proj, proj, proj, proj, proj, proj, w_pad, bias.reshape(1, dc), ln_g.reshape(1, dc),
      ln_b.reshape(1, dc))


def _mix_kernel(of_ref, ob_ref, gr_ref, oc_ref, x_ref, hg_ref, g1_ref, sh2_ref, sc2_ref, ng_ref,
                wo_ref, wq_ref, k1_ref, k2_ref, h1_ref, hp_ref, e_ref, w_ref):
    o = of_ref[0] + ob_ref[0]
    parts = []
    for hh in range(o.shape[1] // HEAD_DIM):
        oh = o[:, hh * HEAD_DIM:(hh + 1) * HEAD_DIM]
        parts.append(oh * lax.rsqrt(jnp.mean(oh * oh, axis=-1, keepdims=True) + EPS))
    on = jnp.concatenate(parts, axis=-1) * hg_ref[...] * _silu(gr_ref[0])
    cat = jnp.concatenate([on, oc_ref[0]], axis=-1).astype(BF16)
    mix = jnp.dot(cat, wo_ref[...], preferred_element_type=F32)
    h1 = x_ref[0] + g1_ref[0] * mix
    h1_ref[0] = h1
    hn2 = _rms_mod(h1, ng_ref[...], sh2_ref[0], sc2_ref[0])
    hp_ref[0] = _pack_words(hn2)
    q = jnp.dot(hn2.astype(BF16), wq_ref[...], preferred_element_type=F32)
    k1h, k1l, _ = _split3(k1_ref[...])
    k2h, k2l, _ = _split3(k2_ref[...])
    es, ws = [], []
    for hd in range(PEER_HEADS):
        top = []
        for half, (kh, kl) in enumerate(((k1h, k1l), (k2h, k2l))):
            c0 = hd * 2 * PEER_HALF + half * PEER_HALF
            qh, ql, _ = _split3(q[:, c0:c0 + PEER_HALF])
            top.append(_top16(_dot_nt(kh, qh) + _dot_nt(kh, ql) + _dot_nt(kl, qh)))
        (v1, i1), (v2, i2) = top
        cand = _pruned_pairs(v1, v2, lambda a, b: a + b)
        cand_e = _pruned_pairs(i1, i2, lambda a, b: a * float(PEER_KEYS) + b)
        sc, e = _top16(cand, cand_e)
        p = jnp.exp(sc - sc[0:1])
        es.append(e.astype(I32))
        ws.append(p / jnp.sum(p, axis=0, keepdims=True))
    e_ref[0] = jnp.concatenate(es, axis=0).T
    w_ref[0] = jnp.concatenate(ws, axis=0).T


def _mix(o_f, o_b, proj, o_c, x, b0, hgrn_g, mod3, norm_g, w_out_bf16, wq_bf16, keys1, keys2, g_col):
    _, L, d = x.shape
    dh = o_f.shape[2]
    dq = wq_bf16.shape[1]
    tt = 256
    nj = L // tt
    nsel = PEER_HEADS * PEER_TOPK

    def half(c=0):
        return pl.BlockSpec((1, tt, dh), lambda b, j: (0, j, c))

    def full(bb=0):
        return pl.BlockSpec((1, tt, d), lambda b, j: (bb, j, 0))

    def modc(c):
        return pl.BlockSpec((1, 1, d), lambda b, j: (0, 0, c))

    def const(shape):
        return pl.BlockSpec(shape, lambda b, j: (0,) * len(shape))

    return pl.pallas_call(
        _mix_kernel,
        grid=(1, nj),
        in_specs=[half(), half(), half(g_col), half(), full(b0), const((1, dh)),
                  modc(2), modc(3), modc(4), const((1, d)), const((d, d)), const((d, dq)),
                  const(keys1.shape), const(keys2.shape)],
        out_specs=[full(), pl.BlockSpec((1, tt, d // 2), lambda b, j: (0, j, 0)),
                   pl.BlockSpec((1, tt, nsel), lambda b, j: (0, j, 0)),
                   pl.BlockSpec((1, tt, nsel), lambda b, j: (0, j, 0))],
        out_shape=[jax.ShapeDtypeStruct((1, L, d), F32), jax.ShapeDtypeStruct((1, L, d // 2), I32),
                   jax.ShapeDtypeStruct((1, L, nsel), I32), jax.ShapeDtypeStruct((1, L, nsel), F32)],
        compiler_params=_cparams(("arbitrary", "arbitrary")),
        name="mix_scores",
    )(o_f, o_b, proj, o_c, x, hgrn_g.reshape(1, dh), mod3, mod3, mod3, norm_g.reshape(1, d),
      w_out_bf16, wq_bf16, keys1, keys2)


def _top16(s, payload=None):
    n = s.shape[0]
    iota = lax.broadcasted_iota(I32, s.shape, 0).astype(F32)
    vals, tags = [], []
    for _ in range(PEER_TOPK):
        m = jnp.max(s, axis=0, keepdims=True)
        idx = jnp.min(jnp.where(s == m, iota, float(n)), axis=0, keepdims=True)
        hit = iota == idx
        vals.append(m)
        tags.append(idx if payload is None else jnp.max(jnp.where(hit, payload, -1.0), axis=0, keepdims=True))
        s = jnp.where(hit, -jnp.inf, s)
    return jnp.concatenate(vals, axis=0), jnp.concatenate(tags, axis=0)


def _pruned_pairs(x1, x2, combine):
    rows = [combine(x1[a:a + 1], x2[0:PEER_TOPK // (a + 1)]) for a in range(PEER_TOPK // 2)]
    rows.append(combine(x1[PEER_TOPK // 2:], x2[0:1]))
    return jnp.concatenate(rows, axis=0)


WORDS = 4


def _pack_words(x):
    bits = pltpu.bitcast(x.astype(BF16).astype(F32), I32)
    out = []
    for c in range(WORDS):
        lo = bits[:, (2 * c) * LANES:(2 * c + 1) * LANES]
        hi = bits[:, (2 * c + 1) * LANES:(2 * c + 2) * LANES]
        out.append(lax.shift_right_logical(lo, 16) | (hi & jnp.int32(-65536)))
    return jnp.concatenate(out, axis=-1)


def _pack_kernel(x_ref, o_ref, b_ref):
    te = x_ref.shape[0]
    x = x_ref[...]
    b_ref[...] = x.astype(BF16)
    words = _pack_words(x)
    for c in range(WORDS):
        o_ref[pl.ds(c, te, stride=WORDS), :] = words[:, c * LANES:(c + 1) * LANES]


def _pack_table(x):
    ne, d = x.shape
    te = 512
    packed, plain = pl.pallas_call(
        _pack_kernel,
        grid=(ne // te,),
        in_specs=[pl.BlockSpec((te, d), lambda i: (i, 0))],
        out_specs=[pl.BlockSpec((te * WORDS, LANES), lambda i: (i, 0)), pl.BlockSpec((te, d), lambda i: (i, 0))],
        out_shape=[jax.ShapeDtypeStruct((ne * WORDS, LANES), I32), jax.ShapeDtypeStruct((ne, d), BF16)],
        compiler_params=_cparams(("arbitrary",)),
        name="pack_table",
    )(x)
    return packed.reshape(ne, WORDS, LANES), plain


NSEL = PEER_HEADS * PEER_TOPK
ROW_TILE = 2 * WORDS


SC_LANES = 16
SC_ROWS = 32
SC_BUFS = 4
SC_TOK = 16
SC_GROUP = 8
VS_GROUP = 2
PART_ROWS = NSEL * SC_LANES // LANES


def _sc_unpack(w):
    lo = plsc.bitcast(lax.shift_left(w, jnp.int32(16)), F32)
    hi = plsc.bitcast(w & jnp.int32(-65536), F32)
    return lo, hi


def _sc_mesh():
    return plsc.VectorSubcoreMesh(core_axis_name="c", subcore_axis_name="s")


def _sc_params():
    return pltpu.CompilerParams(use_tc_tiling_on_sc=True, needs_layout_passes=False)


def _sc_udot(table3, idx, hn3, n_tok):
    T = n_tok
    info = plsc.get_sparse_core_info()
    nw = info.num_cores * info.num_subcores
    tpw = T // nw
    assert T % nw == 0 and tpw % SC_TOK == 0 and NSEL % SC_ROWS == 0
    nchunk = tpw // SC_TOK
    qsteps = NSEL // SC_ROWS
    nstep = SC_TOK * qsteps

    @functools.partial(
        pl.kernel, mesh=_sc_mesh(),
        out_type=jax.ShapeDtypeStruct((T, PART_ROWS, LANES), F32),
        scratch_types=[pltpu.VMEM((SC_TOK * NSEL,), I32),
                       pltpu.VMEM((SC_BUFS, SC_ROWS, WORDS, LANES), I32),
                       pltpu.VMEM((SC_TOK, WORDS * LANES), I32),
                       pltpu.VMEM((SC_TOK, PART_ROWS, LANES), F32),
                       pltpu.SemaphoreType.DMA((SC_BUFS,))],
        compiler_params=_sc_params(),
        name="sc_udot",
    )
    def k(table_hbm, idx_hbm, hn_hbm, out_hbm, idx_v, rows_v, h_v, p_v, sem_g):
        wid = lax.axis_index("s") * info.num_cores + lax.axis_index("c")
        tbase = wid * tpw

        def gather(st, slot):
            return pltpu.make_async_copy(table_hbm.at[idx_v.at[pl.ds(st * SC_ROWS, SC_ROWS)]],
                                         rows_v.at[slot], sem_g.at[slot])

        def compute(tk, q, slot):
            @pl.loop(0, SC_ROWS // SC_GROUP)
            def _(g):
                acc = [jnp.zeros((SC_LANES,), F32) for _ in range(SC_GROUP)]
                for c in range(WORDS):
                    for lv in range(0, LANES // SC_LANES, 2):
                        ls = [pl.ds((lv + d) * SC_LANES, SC_LANES) for d in range(2)]
                        hb = [plsc.bitcast(h_v[tk, pl.ds(c * LANES + (lv + d) * SC_LANES, SC_LANES)], BF16)
                              for d in range(2)]
                        for i in range(SC_GROUP):
                            pr = [plsc.bitcast(rows_v[slot, g * SC_GROUP + i, c, ls[d]], BF16) * hb[d]
                                  for d in range(2)]
                            lo, hi = _sc_unpack(plsc.bitcast(pr[0] + pr[1], I32))
                            acc[i] = acc[i] + lo + hi
                for i in range(SC_GROUP):
                    p_v[tk, q * (SC_ROWS // SC_GROUP) + g, pl.ds(i * SC_LANES, SC_LANES)] = acc[i]

        @pl.loop(0, nchunk)
        def _(ch):
            t0 = tbase + ch * SC_TOK
            pltpu.sync_copy(idx_hbm.at[pl.ds(t0 * NSEL, SC_TOK * NSEL)], idx_v)
            pltpu.sync_copy(hn_hbm.at[pl.ds(t0, SC_TOK)], h_v)
            for s in range(SC_BUFS - 1):
                gather(s, s).start()

            @pl.loop(0, SC_TOK)
            def _(tk):
                for q in range(qsteps):
                    st = tk * qsteps + q
                    slot = q % SC_BUFS

                    @pl.when(st + SC_BUFS - 1 < nstep)
                    def _():
                        gather(st + SC_BUFS - 1, (q + SC_BUFS - 1) % SC_BUFS).start()

                    gather(st, slot).wait()
                    compute(tk, q, slot)

            pltpu.sync_copy(p_v, out_hbm.at[pl.ds(t0, SC_TOK)])

    return k(table3, idx, hn3)


def _sc_vsum(table3, idx, coefx):
    assert VS_GROUP == 2, "the loop body sums the products of exactly two experts in bf16"
    T = coefx.shape[0]
    info = plsc.get_sparse_core_info()
    nw = info.num_cores * info.num_subcores
    tpw = T // nw
    assert T % nw == 0 and tpw % SC_TOK == 0 and NSEL % SC_ROWS == 0 and SC_GROUP % VS_GROUP == 0
    nchunk = tpw // SC_TOK
    qsteps = NSEL // SC_ROWS
    nstep = SC_TOK * qsteps
    nlv = LANES // SC_LANES
    sub = SC_GROUP // VS_GROUP

    @functools.partial(
        pl.kernel, mesh=_sc_mesh(),
        out_type=jax.ShapeDtypeStruct((T, ROW_TILE * LANES), F32),
        scratch_types=[pltpu.VMEM((SC_TOK * NSEL,), I32),
                       pltpu.VMEM((SC_BUFS, SC_ROWS, WORDS, LANES), I32),
                       pltpu.VMEM((SC_TOK, PART_ROWS, LANES), I32),
                       pltpu.VMEM((SC_TOK, ROW_TILE * LANES), F32),
                       pltpu.SemaphoreType.DMA((SC_BUFS,))],
        compiler_params=_sc_params(),
        name="sc_vsum",
    )
    def k(table_hbm, idx_hbm, coef_hbm, out_hbm, idx_v, rows_v, c_v, y_v, sem_g):
        wid = lax.axis_index("s") * info.num_cores + lax.axis_index("c")
        tbase = wid * tpw

        def gather(st, slot):
            return pltpu.make_async_copy(table_hbm.at[idx_v.at[pl.ds(st * SC_ROWS, SC_ROWS)]],
                                         rows_v.at[slot], sem_g.at[slot])

        def compute(tk, q, slot):
            for c in range(WORDS):
                if q == 0:
                    init = tuple(jnp.zeros((SC_LANES,), F32) for _ in range(2 * nlv))
                else:
                    init = tuple(y_v[tk, pl.ds((2 * c + p) * LANES + lv * SC_LANES, SC_LANES)]
                                 for p in range(2) for lv in range(nlv))

                def body(g, acc):
                    acc = list(acc)
                    cb = [plsc.bitcast(c_v[tk, q * (SC_ROWS // SC_GROUP) + g // sub,
                                           pl.ds(((g % sub) * VS_GROUP + i) * SC_LANES, SC_LANES)], BF16)
                          for i in range(VS_GROUP)]
                    for lv in range(nlv):
                        pr = [cb[i] * plsc.bitcast(rows_v[slot, g * VS_GROUP + i, c,
                                                          pl.ds(lv * SC_LANES, SC_LANES)], BF16)
                              for i in range(VS_GROUP)]
                        lo, hi = _sc_unpack(plsc.bitcast(pr[0] + pr[1], I32))
                        acc[lv] = acc[lv] + lo
                        acc[nlv + lv] = acc[nlv + lv] + hi
                    return tuple(acc)

                acc = lax.fori_loop(0, SC_ROWS // VS_GROUP, body, init)
                for p in range(2):
                    for lv in range(nlv):
                        y_v[tk, pl.ds((2 * c + p) * LANES + lv * SC_LANES, SC_LANES)] = acc[p * nlv + lv]

        @pl.loop(0, nchunk)
        def _(ch):
            t0 = tbase + ch * SC_TOK
            pltpu.sync_copy(idx_hbm.at[pl.ds(t0 * NSEL, SC_TOK * NSEL)], idx_v)
            pltpu.sync_copy(coef_hbm.at[pl.ds(t0, SC_TOK)], c_v)
            for s in range(SC_BUFS - 1):
                gather(s, s).start()

            @pl.loop(0, SC_TOK)
            def _(tk):
                for q in range(qsteps):
                    st = tk * qsteps + q
                    slot = q % SC_BUFS

                    @pl.when(st + SC_BUFS - 1 < nstep)
                    def _():
                        gather(st + SC_BUFS - 1, (q + SC_BUFS - 1) % SC_BUFS).start()

                    gather(st, slot).wait()
                    compute(tk, q, slot)

            pltpu.sync_copy(y_v, out_hbm.at[pl.ds(t0, SC_TOK)])

    return k(table3, idx, coefx)


def _coef_kernel(p_ref, w_ref, cx_ref, *, tt):
    row = lax.broadcasted_iota(I32, (LANES, NSEL), 0)
    col = lax.broadcasted_iota(I32, (LANES, NSEL), 1)
    per_row = LANES // SC_LANES
    act = jnp.zeros((tt, NSEL), F32)
    for s in range(PART_ROWS):
        fold = (col == per_row * s + row // SC_LANES).astype(BF16)
        p1, p2, p3 = _split3(p_ref[pl.ds(s, tt, stride=PART_ROWS), :])
        act = act + (jnp.dot(p1, fold, preferred_element_type=F32) + jnp.dot(p2, fold, preferred_element_type=F32)
                     + jnp.dot(p3, fold, preferred_element_type=F32))
    coef = w_ref[...] * (0.5 * act * (1.0 + lax.erf(act * (2.0 ** -0.5))))
    cb = coef.astype(BF16)
    for s in range(PART_ROWS):
        spread = (row == per_row * s + col // SC_LANES).astype(BF16)
        bits = pltpu.bitcast(jnp.dot(cb, spread, preferred_element_type=F32), I32)
        cx_ref[pl.ds(s, tt, stride=PART_ROWS), :] = bits | lax.shift_right_logical(bits, 16)


def _coef(p2d, w):
    T = p2d.shape[0] // PART_ROWS
    tt = 256
    return pl.pallas_call(
        functools.partial(_coef_kernel, tt=tt),
        grid=(T // tt,),
        in_specs=[pl.BlockSpec((tt * PART_ROWS, LANES), lambda i: (i, 0)),
                  pl.BlockSpec((tt, NSEL), lambda i: (i, 0))],
        out_specs=pl.BlockSpec((tt * PART_ROWS, LANES), lambda i: (i, 0)),
        out_shape=jax.ShapeDtypeStruct((T * PART_ROWS, LANES), I32),
        compiler_params=_cparams(("arbitrary",)),
        name="peer_coef",
    )(p2d, w)


SC_FRACTION = 0.375
SPLIT_UNIT = 1024
WD_BUFS = 4
WD_TOK = 16


def _sc_wdense(e_flat, w_flat, n_skip, n_tok, nexp):
    info = plsc.get_sparse_core_info()
    nw = info.num_cores * info.num_subcores
    tpw = n_tok // nw
    assert n_tok % nw == 0 and tpw % WD_TOK == 0 and WD_TOK % WD_BUFS == 0 and nexp % SC_LANES == 0
    nchunk = tpw // WD_TOK
    heads = NSEL // SC_LANES

    @functools.partial(
        pl.kernel, mesh=_sc_mesh(),
        out_type=jax.ShapeDtypeStruct((n_tok, nexp), F32),
        scratch_types=[pltpu.VMEM((WD_TOK * NSEL,), I32), pltpu.VMEM((WD_TOK * NSEL,), F32)]
        + [pltpu.VMEM((nexp,), F32) for _ in range(WD_BUFS)]
        + [pltpu.SemaphoreType.DMA((WD_BUFS,))],
        compiler_params=_sc_params(),
        name="sc_wdense",
    )
    def k(e_hbm, w_hbm, out_hbm, idx_v, w_v, *rest):
        rows, sem = rest[:WD_BUFS], rest[WD_BUFS]
        wid = lax.axis_index("s") * info.num_cores + lax.axis_index("c")
        tbase = wid * tpw
        zeros = jnp.zeros((SC_LANES,), F32)

        for s in range(WD_BUFS):
            @pl.loop(0, nexp // SC_LANES)
            def _(i):
                rows[s][pl.ds(i * SC_LANES, SC_LANES)] = zeros

        def put(tok, s):
            return pltpu.make_async_copy(rows[s], out_hbm.at[tok], sem.at[s])

        def scatter(tk, s, clear):
            for h in range(heads):
                sl = pl.ds(tk * NSEL + h * SC_LANES, SC_LANES)
                if clear:
                    plsc.store_scatter(rows[s], [idx_v[sl]], zeros)
                else:
                    plsc.addupdate_scatter(rows[s], [idx_v[sl]], w_v[sl])

        @pl.loop(0, nchunk)
        def _(ch):
            t0 = tbase + ch * WD_TOK
            pltpu.sync_copy(e_hbm.at[pl.ds((n_skip + t0) * NSEL, WD_TOK * NSEL)], idx_v)
            pltpu.sync_copy(w_hbm.at[pl.ds((n_skip + t0) * NSEL, WD_TOK * NSEL)], w_v)

            @pl.loop(0, WD_TOK, step=WD_BUFS)
            def _(tk0):
                for s in range(WD_BUFS):
                    tk = tk0 + s

                    @pl.when(tk0 > 0)
                    def _():
                        put(0, s).wait()
                        scatter(tk - WD_BUFS, s, True)

                    scatter(tk, s, False)
                    put(t0 + tk, s).start()

            for s in range(WD_BUFS):
                put(0, s).wait()
                scatter(WD_TOK - WD_BUFS + s, s, True)

    return k(e_flat, w_flat)


def _dense_kernel(hp_ref, wd_ref, u_ref, v_ref, y_ref, acc_ref):
    k = pl.program_id(1)

    @pl.when(k == 0)
    def _():
        acc_ref[...] = jnp.zeros_like(acc_ref)

    words = hp_ref[...]
    lo = pltpu.bitcast(lax.shift_left(words, 16), F32)
    hi = pltpu.bitcast(words & jnp.int32(-65536), F32)
    parts = []
    for c in range(WORDS):
        parts += [lo[:, c * LANES:(c + 1) * LANES], hi[:, c * LANES:(c + 1) * LANES]]
    hn = jnp.concatenate(parts, axis=-1).astype(BF16)
    act = _dot_nt(hn, u_ref[...])
    wd = wd_ref[...]
    coef = jnp.where(wd != 0.0, wd * (0.5 * act * (1.0 + lax.erf(act * (2.0 ** -0.5)))), 0.0)
    acc_ref[...] += jnp.dot(coef.astype(BF16), v_ref[...], preferred_element_type=F32)

    @pl.when(k == pl.num_programs(1) - 1)
    def _():
        y_ref[...] = acc_ref[...]


def _dense_experts(hp, wd, u_bf16, v_bf16, n_skip):
    n_tok, nexp = wd.shape
    dw = hp.shape[1]
    d = u_bf16.shape[1]
    tt = min(SPLIT_UNIT, n_tok)
    eb = 1024
    skip = n_skip // tt
    return pl.pallas_call(
        _dense_kernel,
        grid=(n_tok // tt, nexp // eb),
        in_specs=[pl.BlockSpec((tt, dw), lambda i, k: (skip + i, 0)),
                  pl.BlockSpec((tt, eb), lambda i, k: (i, k)),
                  pl.BlockSpec((eb, d), lambda i, k: (k, 0)),
                  pl.BlockSpec((eb, d), lambda i, k: (k, 0))],
        out_specs=pl.BlockSpec((tt, d), lambda i, k: (i, 0)),
        out_shape=jax.ShapeDtypeStruct((n_tok, d), F32),
        scratch_shapes=[pltpu.VMEM((tt, d), F32)],
        compiler_params=_cparams(("arbitrary", "arbitrary")),
        name="peer_dense",
    )(hp, wd, u_bf16, v_bf16)


def _final_kernel(h1_ref, ya_ref, yb_ref, g2_ref, fg_ref, fsh_ref, fsc_ref, *refs, na):
    o_ref = refs[-1]
    y = jnp.where(pl.program_id(1) < na, ya_ref[...], yb_ref[...])
    h = h1_ref[0] + g2_ref[0] * y
    o_ref[0] = _rms_mod(h, fg_ref[...], fsh_ref[0], fsc_ref[0])


def _final(h1, ya, yb, mod3, fmod3, final_g, out_prev, b0, nb):
    _, L, d = h1.shape
    tt = 512
    na, nbk = ya.shape[0] // tt, yb.shape[0] // tt
    in_specs = [pl.BlockSpec((1, tt, d), lambda b, j: (0, j, 0)),
                pl.BlockSpec((tt, d), lambda b, j: (jnp.minimum(j, na - 1), 0)),
                pl.BlockSpec((tt, d), lambda b, j: (jnp.clip(j - na, 0, nbk - 1), 0)),
                pl.BlockSpec((1, 1, d), lambda b, j: (0, 0, 5)),
                pl.BlockSpec((1, d), lambda b, j: (0, 0)),
                pl.BlockSpec((1, 1, d), lambda b, j: (0, 0, 0)),
                pl.BlockSpec((1, 1, d), lambda b, j: (0, 0, 1))]
    args = [h1, ya, yb, mod3, final_g.reshape(1, d), fmod3, fmod3]
    aliases = {}
    if out_prev is not None:
        in_specs.append(pl.BlockSpec(memory_space=pl.ANY))
        aliases = {len(args): 0}
        args.append(out_prev)
    return pl.pallas_call(
        functools.partial(_final_kernel, na=na),
        grid=(1, L // tt),
        in_specs=in_specs,
        out_specs=pl.BlockSpec((1, tt, d), lambda b, j: (b0, j, 0)),
        out_shape=jax.ShapeDtypeStruct((nb, L, d), F32),
        input_output_aliases=aliases,
        compiler_params=_cparams(("arbitrary", "arbitrary")),
        name="final_norm",
    )(*args)


def kernel(x, c, ada_w, ada_b, norm_mix_g, w_in, lb_gamma_fwd, lb_gamma_bwd, hgrn_norm_g, conv_w,
           conv_b, conv_ln_g, conv_ln_b, w_out, norm_ffn_g, peer_wq, peer_keys1, peer_keys2, peer_u,
           peer_v, final_ada_w, final_ada_b, final_norm_g):
    nb, L, d = x.shape
    assert ada_w.shape[0] == 1, "single-layer trunk"
    d_hgrn = lb_gamma_fwd.shape[1]

    c_pad = jnp.pad(c, ((0, SUBLANES - nb), (0, 0)))
    mod3 = _ada(c_pad, ada_w[0], ada_b[0])[:nb].reshape(nb, 1, 6 * d)
    fmod3 = _ada(c_pad, final_ada_w, final_ada_b)[:nb].reshape(nb, 1, 2 * d)

    d_conv = conv_w.shape[2]
    w_pad = jnp.pad(conv_w[0], ((0, 1), (0, 0)))
    w_in_bf, w_out_bf, wq_bf = w_in[0].astype(BF16), w_out[0].astype(BF16), peer_wq[0].astype(BF16)
    mb, (u_rows, u_bf) = lax.optimization_barrier((mod3[0:1], _pack_table(peer_u[0])))
    nexp = peer_u.shape[1]
    n_sc = int(L * SC_FRACTION) // SPLIT_UNIT * SPLIT_UNIT
    v_src, v_rows, v_bf = peer_v[0], None, None
    work = []

    def weighted_sum(item):
        return _sc_vsum(v_rows, item["idx"], item["cx"].reshape(n_sc, PART_ROWS, LANES))

    for b in range(nb):
        proj = _inproj(x, b, norm_mix_g[0], mb, w_in_bf)
        o_f, o_b = _hgrn(proj, lb_gamma_fwd, lb_gamma_bwd)
        o_c = _conv(proj, w_pad, conv_b[0], conv_ln_g[0], conv_ln_b[0], a_col=5 * d_hgrn // d_conv)
        h1, hp, e, w = _mix(o_f, o_b, proj, o_c, x, b, hgrn_norm_g[0], mb, norm_ffn_g[0], w_out_bf,
                            wq_bf, peer_keys1[0], peer_keys2[0], g_col=4)
        e, w = e.reshape(L, e.shape[2]), w.reshape(L, w.shape[2])
        item = dict(h1=h1, mb=mb, hp=hp.reshape(L, d // 2))
        tied = dict(e=e, w=w)
        if b + 1 < nb:
            tied["mb"] = mod3[b + 1:b + 2]
        if work:
            prev = work[-1]
            tied["wd"] = prev["wd"]
            tied["cx"] = _coef(prev["p"].reshape(n_sc * PART_ROWS, LANES), prev["w"])
            if len(work) > 1:
                tied["y"] = work[-2]["y_sc"]
        elif nb > 1:
            tied["v"] = v_src
        tied = lax.optimization_barrier(tied)
        e, w, mb = tied["e"], tied["w"], tied.get("mb")
        if "v" in tied:
            v_rows, v_bf = _pack_table(tied["v"])
        if work:
            prev["wd"], prev["cx"] = tied["wd"], tied["cx"]
            if "y" in tied:
                work[-2]["y_sc"] = tied["y"]
            prev["y_sc"] = weighted_sum(prev)
        item["idx"], item["w"] = e.reshape(L * NSEL), w
        item["p"] = _sc_udot(u_rows, item["idx"], item["hp"], n_sc)
        item["wd"] = _sc_wdense(item["idx"], w.reshape(L * NSEL), n_sc, L - n_sc, nexp)
        work.append(item)

    if v_rows is None:
        v_rows, v_bf = _pack_table(v_src)
    y_dn = [_dense_experts(item["hp"], item["wd"], u_bf, v_bf, n_sc) for item in work[:-1]]
    last = work[-1]
    last["p"], y_dn = lax.optimization_barrier((last["p"], y_dn))
    last["cx"] = _coef(last["p"].reshape(n_sc * PART_ROWS, LANES), last["w"])
    last["y_sc"] = weighted_sum(last)
    y_dn.append(_dense_experts(last["hp"], last["wd"], u_bf, v_bf, n_sc))
    out = None
    for b, item in enumerate(work):
        out = _final(item["h1"], item["y_sc"], y_dn[b], item["mb"], fmod3[b:b + 1], final_norm_g, out, b, nb)
    return out
```
